```python
import math
import jax, jax.numpy as jnp
from jax import lax
import numpy as np

D_MODEL = 1024
BATCH = 4
SEQ = 8192
DEPTH = 2

PLE_DIM = 256
NSA_HEADS = 8
NSA_HEAD_DIM = 64
NSA_GROUPS = 2
NSA_Q_PER_GROUP = NSA_HEADS // NSA_GROUPS
CMP_BLOCK = 32
CMP_STRIDE = 16
CMP_HIDDEN = 4 * NSA_HEAD_DIM
SLC_BLOCK = 64
N_SELECT = 16
WINDOW = 512
Q_BLOCK = 128
RET_HEADS = 4
RET_HEAD_DIM = 128
RET_CHUNK = 128
ROPE_BASE = 10000.0
D_FF = 4 * D_MODEL
EPS = 1e-6
MASK_VALUE = -1e30
FORCE_SCORE = 1e4

NSA_Q_W = NSA_HEADS * NSA_HEAD_DIM
NSA_KV_W = NSA_GROUPS * NSA_HEAD_DIM
NSA_GATE_W = 3 * NSA_HEADS
RET_W = RET_HEADS * RET_HEAD_DIM
IN_SPLITS = (NSA_Q_W, NSA_KV_W, NSA_KV_W, NSA_KV_W, NSA_KV_W, NSA_KV_W, NSA_KV_W, NSA_GATE_W,
             RET_W, RET_W, RET_W, RET_W, D_MODEL, D_MODEL)
D_IN = sum(IN_SPLITS)

kernel_name = "hybrid_nsa_retention_griffin_merge"


def split_cols(z, sizes):
    out = []
    off = 0
    for s in sizes:
        out.append(z[..., off:off + s])
        off += s
    return out


def rmsnorm(x, g):
    xf = x.astype(jnp.float32)
    y = xf * lax.rsqrt(jnp.mean(xf * xf, axis=-1, keepdims=True) + EPS)
    return (y * g.astype(jnp.float32)).astype(x.dtype)


def rms_noaffine(x):
    xf = x.astype(jnp.float32)
    return xf * lax.rsqrt(jnp.mean(xf * xf, axis=-1, keepdims=True) + EPS)


def masked_softmax(s, mask):
    s = jnp.where(mask, s, MASK_VALUE)
    m = jnp.max(s, axis=-1, keepdims=True)
    e = jnp.exp(s - m) * mask
    return e / jnp.maximum(jnp.sum(e, axis=-1, keepdims=True), 1e-30)


def rope(x, pos):
    half = x.shape[-1] // 2
    inv = ROPE_BASE ** (-jnp.arange(half, dtype=jnp.float32) / half)
    ang = pos[:, None] * inv[None, :]
    cos = jnp.cos(ang)[:, None, :]
    sin = jnp.sin(ang)[:, None, :]
    x1 = x[..., :half].astype(jnp.float32)
    x2 = x[..., half:].astype(jnp.float32)
    return jnp.concatenate([x1 * cos - x2 * sin, x1 * sin + x2 * cos], axis=-1)


def nsa_mixer(q_in, kc_in, vc_in, ks_in, vs_in, kw_in, vw_in, gate_logits,
              qn_g, kn_g, pos_k, pos_v, w1_k, w2_k, w1_v, w2_v):
    B, S = q_in.shape[:2]
    G, Hg, dh = NSA_GROUPS, NSA_Q_PER_GROUP, NSA_HEAD_DIM
    dtype = q_in.dtype
    scale = dh ** -0.5
    q = rmsnorm(q_in.reshape(B, S, G, Hg, dh).transpose(0, 2, 3, 1, 4), qn_g)

    def kv_heads(z):
        return z.reshape(B, S, G, dh).transpose(0, 2, 1, 3)

    k_c, v_c = kv_heads(kc_in), kv_heads(vc_in)
    k_s, v_s = rmsnorm(kv_heads(ks_in), kn_g), kv_heads(vs_in)
    k_w, v_w = rmsnorm(kv_heads(kw_in), kn_g), kv_heads(vw_in)
    gates = jax.nn.sigmoid(gate_logits.astype(jnp.float32)).reshape(B, S, 3, G, Hg).transpose(0, 3, 4, 1, 2)

    nc = S // CMP_STRIDE
    starts = jnp.arange(nc) * CMP_STRIDE
    idx = jnp.minimum(starts[:, None] + jnp.arange(CMP_BLOCK)[None, :], S - 1)
    cmp_end = starts + CMP_BLOCK - 1

    def compress(z, pos, w1, w2):
        zb = (z[:, :, idx] + pos).reshape(B, G, nc, CMP_BLOCK * dh)
        return jax.nn.gelu(zb @ w1) @ w2

    k_cmp = rmsnorm(compress(k_c, pos_k, w1_k, w2_k), kn_g)
    v_cmp = compress(v_c, pos_v, w1_v, w2_v)

    ns = S // SLC_BLOCK
    n_sel = min(N_SELECT, ns)
    ratio = SLC_BLOCK // CMP_STRIDE
    k_blk = k_s.reshape(B, G, ns, SLC_BLOCK, dh)
    v_blk = v_s.reshape(B, G, ns, SLC_BLOCK, dh)
    b_ix = jnp.arange(B)[:, None, None, None]
    g_ix = jnp.arange(G)[None, :, None, None]

    pad = ((0, 0), (0, 0), (WINDOW, 0), (0, 0))
    k_w_pad = jnp.pad(k_w, pad)
    v_w_pad = jnp.pad(v_w, pad)

    def query_block(c):
        q0 = c * Q_BLOCK
        qc = lax.dynamic_slice_in_dim(q, q0, Q_BLOCK, axis=3)
        gc = lax.dynamic_slice_in_dim(gates, q0, Q_BLOCK, axis=3)
        t = q0 + jnp.arange(Q_BLOCK)

        s = jnp.einsum('bghqd,bgnd->bghqn', qc, k_cmp).astype(jnp.float32) * scale
        p_cmp = masked_softmax(s, cmp_end[None, :] <= t[:, None])
        o_cmp = jnp.einsum('bghqn,bgnd->bghqd', p_cmp.astype(dtype), v_cmp)

        imp = jnp.sum(p_cmp, axis=2).reshape(B, G, Q_BLOCK, ns, ratio)
        imp_blk = jnp.sum(imp, axis=-1) + jnp.pad(imp[..., :-1, -1], ((0, 0), (0, 0), (0, 0), (1, 0)))
        blk = jnp.arange(ns)[None, :]
        cur = (t // SLC_BLOCK)[:, None]
        forced = (blk == 0) | (blk == cur) | (blk == cur - 1)
        score = jnp.where(forced, FORCE_SCORE, imp_blk)
        score = jnp.where(blk > cur, MASK_VALUE, score)
        _, sel = lax.top_k(score, n_sel)
        ks = k_blk[b_ix, g_ix, sel]
        vs = v_blk[b_ix, g_ix, sel]
        kpos = sel[..., None] * SLC_BLOCK + jnp.arange(SLC_BLOCK)
        smask = (kpos <= t[:, None, None]).reshape(B, G, Q_BLOCK, n_sel * SLC_BLOCK)[:, :, None]
        s = jnp.einsum('bghqd,bgqnkd->bghqnk', qc, ks).astype(jnp.float32) * scale
        p_slc = masked_softmax(s.reshape(B, G, Hg, Q_BLOCK, n_sel * SLC_BLOCK), smask)
        o_slc = jnp.einsum('bghqnk,bgqnkd->bghqd',
                           p_slc.reshape(B, G, Hg, Q_BLOCK, n_sel, SLC_BLOCK).astype(dtype), vs)

        kw = lax.dynamic_slice_in_dim(k_w_pad, q0, Q_BLOCK + WINDOW, axis=2)
        vw = lax.dynamic_slice_in_dim(v_w_pad, q0, Q_BLOCK + WINDOW, axis=2)
        wpos = q0 - WINDOW + jnp.arange(Q_BLOCK + WINDOW)
        dist = t[:, None] - wpos[None, :]
        wmask = (dist >= 0) & (dist < WINDOW) & (wpos[None, :] >= 0)
        s = jnp.einsum('bghqd,bgkd->bghqk', qc, kw).astype(jnp.float32) * scale
        p_win = masked_softmax(s, wmask)
        o_win = jnp.einsum('bghqk,bgkd->bghqd', p_win.astype(dtype), vw)

        return gc[..., 0:1] * o_cmp + gc[..., 1:2] * o_slc + gc[..., 2:3] * o_win

    out = lax.map(query_block, jnp.arange(S // Q_BLOCK))
    return out.transpose(1, 0, 4, 2, 3, 5).reshape(B, S, NSA_Q_W).astype(dtype)


def retention(q_in, k_in, v_in, g_in):
    B, S = q_in.shape[:2]
    H, dk, C = RET_HEADS, RET_HEAD_DIM, RET_CHUNK
    dtype = q_in.dtype
    pos = jnp.arange(S, dtype=jnp.float32)
    q = rope(q_in.reshape(B, S, H, dk), pos)
    k = rope(k_in.reshape(B, S, H, dk), pos) * (dk ** -0.5)
    v = v_in.reshape(B, S, H, dk).astype(jnp.float32)
    nch = S // C

    def chunks(z):
        return z.reshape(B, nch, C, H, dk).transpose(0, 3, 1, 2, 4)

    qc, kc, vc = chunks(q), chunks(k), chunks(v)
    gamma = 1.0 - 2.0 ** (-5.0 - jnp.arange(H, dtype=jnp.float32))
    lg = jnp.log(gamma)
    n = jnp.arange(C, dtype=jnp.float32)
    diff = n[:, None] - n[None, :]
    decay = jnp.where(diff >= 0, jnp.exp(lg[:, None, None] * jnp.maximum(diff, 0.0)), 0.0)
    xi = jnp.exp(lg[:, None] * (n + 1.0))[None, :, None, :, None]
    zeta = jnp.exp(lg[:, None] * (C - 1.0 - n))[None, :, None, :, None]
    g_chunk = jnp.exp(lg * C)[None, :, None, None]

    inner = jnp.einsum('bhcnd,bhcmd->bhcnm', qc, kc) * decay[None, :, None]
    inner = jnp.einsum('bhcnm,bhcme->bhcne', inner, vc)
    kv = jnp.einsum('bhcmd,bhcme->cbhde', kc * zeta, vc)

    def step(R, kv_c):
        return g_chunk * R + kv_c, R

    _, R_prev = lax.scan(step, jnp.zeros((B, H, dk, dk), jnp.float32), kv)
    cross = jnp.einsum('bhcnd,cbhde->bhcne', qc * xi, R_prev)
    y = rms_noaffine(inner + cross)
    y = y.transpose(0, 2, 3, 1, 4).reshape(B, S, RET_W)
    return (jax.nn.silu(g_in.astype(jnp.float32)) * y).astype(dtype)


def setup_inputs(seed: int = 0) -> dict:
    key = jax.random.key(seed)
    ks = jax.random.split(key, 21)
    f32 = jnp.float32

    def nrm(k, shape, scale):
        return jax.random.normal(k, shape, f32) * scale

    def gain(k, shape):
        return 1.0 + 0.1 * jax.random.normal(k, shape, f32)

    dh = NSA_HEAD_DIM
    return {
        "x": nrm(ks[0], (BATCH, SEQ, D_MODEL), 1.0),
        "p": nrm(ks[1], (DEPTH, BATCH, SEQ, PLE_DIM), 1.0),
        "norm_mix": gain(ks[2], (DEPTH, D_MODEL)),
        "w_in": nrm(ks[3], (DEPTH, D_MODEL, D_IN), D_MODEL ** -0.5),
        "nsa_q_norm": gain(ks[4], (DEPTH, dh)),
        "nsa_k_norm": gain(ks[5], (DEPTH, dh)),
        "cmp_pos_k": nrm(ks[6], (DEPTH, CMP_BLOCK, dh), 0.1),
        "cmp_pos_v": nrm(ks[7], (DEPTH, CMP_BLOCK, dh), 0.1),
        "cmp_w1_k": nrm(ks[8], (DEPTH, CMP_BLOCK * dh, CMP_HIDDEN), (CMP_BLOCK * dh) ** -0.5),
        "cmp_w2_k": nrm(ks[9], (DEPTH, CMP_HIDDEN, dh), CMP_HIDDEN ** -0.5),
        "cmp_w1_v": nrm(ks[10], (DEPTH, CMP_BLOCK * dh, CMP_HIDDEN), (CMP_BLOCK * dh) ** -0.5),
        "cmp_w2_v": nrm(ks[11], (DEPTH, CMP_HIDDEN, dh), CMP_HIDDEN ** -0.5),
        "w_up_nsa": nrm(ks[12], (DEPTH, NSA_Q_W, D_MODEL), NSA_Q_W ** -0.5),
        "w_up_ret": nrm(ks[13], (DEPTH, RET_W, D_MODEL), RET_W ** -0.5),
        "w_out": nrm(ks[14], (DEPTH, D_MODEL, D_MODEL), D_MODEL ** -0.5),
        "norm_mlp": gain(ks[15], (DEPTH, D_MODEL)),
        "w_ff1": nrm(ks[16], (DEPTH, D_MODEL, D_FF), D_MODEL ** -0.5),
        "w_ff2": nrm(ks[17], (DEPTH, D_FF, D_MODEL), D_FF ** -0.5),
        "norm_ple": gain(ks[18], (DEPTH, D_MODEL)),
        "w_ple": nrm(ks[19], (DEPTH, PLE_DIM, D_MODEL), PLE_DIM ** -0.5),
        "w_ple_gate": nrm(ks[20], (DEPTH, D_MODEL, D_MODEL), D_MODEL ** -0.5),
    }


def reference(x, p, norm_mix, w_in, nsa_q_norm, nsa_k_norm, cmp_pos_k, cmp_pos_v,
              cmp_w1_k, cmp_w2_k, cmp_w1_v, cmp_w2_v, w_up_nsa, w_up_ret, w_out,
              norm_mlp, w_ff1, w_ff2, norm_ple, w_ple, w_ple_gate):
    for i in range(DEPTH):
        h = rmsnorm(x, norm_mix[i])
        z = h @ w_in[i]
        (q_a, kc, vc, ks_, vs_, kw, vw, g_nsa, rq, rk, rv, rg, merge_a, merge_b) = split_cols(z, IN_SPLITS)
        y_a = nsa_mixer(q_a, kc, vc, ks_, vs_, kw, vw, g_nsa, nsa_q_norm[i], nsa_k_norm[i],
                        cmp_pos_k[i], cmp_pos_v[i], cmp_w1_k[i], cmp_w2_k[i], cmp_w1_v[i], cmp_w2_v[i]) @ w_up_nsa[i]
        y_b = retention(rq, rk, rv, rg) @ w_up_ret[i]
        mix = jax.nn.sigmoid(merge_a) * y_a + jax.nn.sigmoid(merge_b) * y_b
        x = x + mix @ w_out[i]
        h = rmsnorm(x, norm_mlp[i])
        x = x + jnp.square(jax.nn.relu(h @ w_ff1[i])) @ w_ff2[i]
        gate = jax.nn.sigmoid(rmsnorm(x, norm_ple[i]) @ w_ple_gate[i])
        x = x + gate * (p[i] @ w_ple[i])
    return x
```

```python
import functools
import math

import jax
import jax.numpy as jnp
from jax import lax
from jax.experimental import pallas as pl
from jax.experimental.pallas import tpu as pltpu

D_MODEL = 1024
PLE_DIM = 256
NSA_HEADS = 8
DH = 64
NG = 2
HG = NSA_HEADS // NG
CMP_BLOCK = 32
CMP_STRIDE = 16
CMP_HIDDEN = 4 * DH
SLC_BLOCK = 64
N_SELECT = 16
WINDOW = 512
QB = 128
RET_HEADS = 4
DK = 128
RET_CHUNK = 128
ROPE_BASE = 10000.0
D_FF = 4 * D_MODEL
EPS = 1e-6
MASK_VALUE = -1e30
FORCE_SCORE = 1e4

NSA_Q_W = NSA_HEADS * DH
NSA_KV_W = NG * DH
NSA_GATE_W = 3 * NSA_HEADS
RET_W = RET_HEADS * DK
NAT_W = 4 * RET_W + 2 * D_MODEL
GATE_ROWS = 32
T_ROWS = NSA_Q_W + 6 * NSA_KV_W + GATE_ROWS

VMEM_LIMIT = 56 * 1024 * 1024

F32 = jnp.float32
BF16 = jnp.bfloat16


def _dot(a, b):
    return jnp.dot(a, b, preferred_element_type=F32)


def _dot_nt(a, b):
    return lax.dot_general(a, b, (((1,), (1,)), ((), ())), preferred_element_type=F32)


def _dot_tn(a, b):
    return lax.dot_general(a, b, (((0,), (0,)), ((), ())), preferred_element_type=F32)


def _rms_rows(x, g):
    y = x * lax.rsqrt(jnp.mean(x * x, axis=-1, keepdims=True) + EPS)
    return y * g


def _params(n_axes):
    return pltpu.CompilerParams(dimension_semantics=("arbitrary",) * n_axes,
                                vmem_limit_bytes=VMEM_LIMIT)


def _proj_kernel(x_ref, g_ref, wn_ref, wt_ref, qg_ref, kg_ref,
                 zn_ref, qT_ref, cT_ref, kT_ref, vT_ref, gT_ref):
    h = _rms_rows(x_ref[...], g_ref[...]).astype(BF16)
    zn_ref[...] = _dot(h, wn_ref[...])
    zt = _dot_nt(wt_ref[...], h)
    qg = qg_ref[...]
    kg = kg_ref[...]

    def norm_cols(z, g):
        return z * lax.rsqrt(jnp.mean(z * z, axis=0, keepdims=True) + EPS) * g

    for hh in range(NSA_HEADS):
        rows = slice(hh * DH, (hh + 1) * DH)
        qT_ref[rows, :] = (norm_cols(zt[rows, :], qg) * (DH ** -0.5)).astype(BF16)
    o = NSA_Q_W
    cT_ref[...] = zt[o:o + 2 * NSA_KV_W, :]
    o += 2 * NSA_KV_W
    for j in range(2 * NG):
        rows = slice(o + j * DH, o + (j + 1) * DH)
        kT_ref[j * DH:(j + 1) * DH, :] = norm_cols(zt[rows, :], kg).astype(BF16)
    o += 2 * NSA_KV_W
    vT_ref[...] = zt[o:o + 2 * NSA_KV_W, :].astype(BF16)
    o += 2 * NSA_KV_W
    gT_ref[...] = jax.nn.sigmoid(zt[o:o + GATE_ROWS, :])


def _proj(x2, g, wn, wt, qg, kg, *, tm):
    n = x2.shape[0]
    full = lambda i: (0, 0)
    colblk = lambda i: (0, i)
    return pl.pallas_call(
        _proj_kernel,
        grid=(n // tm,),
        in_specs=[
            pl.BlockSpec((tm, D_MODEL), lambda i: (i, 0)),
            pl.BlockSpec((1, D_MODEL), full),
            pl.BlockSpec((D_MODEL, NAT_W), full),
            pl.BlockSpec((T_ROWS, D_MODEL), full),
            pl.BlockSpec((DH, 1), full),
            pl.BlockSpec((DH, 1), full),
        ],
        out_specs=[
            pl.BlockSpec((tm, NAT_W), lambda i: (i, 0)),
            pl.BlockSpec((NSA_Q_W, tm), colblk),
            pl.BlockSpec((2 * NSA_KV_W, tm), colblk),
            pl.BlockSpec((2 * NSA_KV_W, tm), colblk),
            pl.BlockSpec((2 * NSA_KV_W, tm), colblk),
            pl.BlockSpec((GATE_ROWS, tm), colblk),
        ],
        out_shape=[
            jax.ShapeDtypeStruct((n, NAT_W), F32),
            jax.ShapeDtypeStruct((NSA_Q_W, n), BF16),
            jax.ShapeDtypeStruct((2 * NSA_KV_W, n), F32),
            jax.ShapeDtypeStruct((2 * NSA_KV_W, n), BF16),
            jax.ShapeDtypeStruct((2 * NSA_KV_W, n), BF16),
            jax.ShapeDtypeStruct((GATE_ROWS, n), F32),
        ],
        compiler_params=_params(1),
        name="proj",
    )(x2, g, wn, wt, qg, kg)


def _gelu_tanh(x):
    cdf = 0.5 * (1.0 + jnp.tanh(math.sqrt(2.0 / math.pi) * (x + 0.044715 * (x ** 3))))
    return x * cdf


def _cmp_kernel(zk_ref, zv_ref, pk_ref, pv_ref, w1k_ref, w2k_ref, w1v_ref, w2vT_ref, kg_ref,
                kc_ref, vcT_ref, *, nc):
    half = CMP_STRIDE * DH

    def hidden(z_ref, p_ref, w1_ref):
        a = (z_ref[0, 0, 0:nc, :] + p_ref[:, 0:half]).astype(BF16)
        b = (z_ref[0, 0, 1:nc + 1, :] + p_ref[:, half:2 * half]).astype(BF16)
        return _gelu_tanh(_dot(a, w1_ref[0:half, :]) + _dot(b, w1_ref[half:2 * half, :]))

    hk = hidden(zk_ref, pk_ref, w1k_ref).astype(BF16)
    kc_ref[0, 0] = _rms_rows(_dot(hk, w2k_ref[...]), kg_ref[...]).astype(BF16)
    hv = hidden(zv_ref, pv_ref, w1v_ref).astype(BF16)
    vcT_ref[0, 0] = _dot_nt(w2vT_ref[...], hv).astype(BF16)


def _compress(zk, zv, pk, pv, w1k, w2k, w1v, w2vT, kg, *, nc):
    b = zk.shape[0]
    rows = zk.shape[2]
    half = CMP_STRIDE * DH
    zspec = pl.BlockSpec((1, 1, rows, half), lambda i, j: (i, j, 0, 0))
    full = lambda i, j: (0, 0)
    return pl.pallas_call(
        functools.partial(_cmp_kernel, nc=nc),
        grid=(b, NG),
        in_specs=[
            zspec, zspec,
            pl.BlockSpec((1, 2 * half), full),
            pl.BlockSpec((1, 2 * half), full),
            pl.BlockSpec((2 * half, CMP_HIDDEN), full),
            pl.BlockSpec((CMP_HIDDEN, DH), full),
            pl.BlockSpec((2 * half, CMP_HIDDEN), full),
            pl.BlockSpec((DH, CMP_HIDDEN), full),
            pl.BlockSpec((1, DH), full),
        ],
        out_specs=[
            pl.BlockSpec((1, 1, nc, DH), lambda i, j: (i, j, 0, 0)),
            pl.BlockSpec((1, 1, DH, nc), lambda i, j: (i, j, 0, 0)),
        ],
        out_shape=[
            jax.ShapeDtypeStruct((b, NG, nc, DH), BF16),
            jax.ShapeDtypeStruct((b, NG, DH, nc), BF16),
        ],
        compiler_params=_params(2),
        name="compress",
    )(zk, zv, pk, pv, w1k, w2k, w1v, w2vT, kg)


def _nsa_kernel(q_ref, kc_ref, vcT_ref, ks_ref, vsT_ref, kw_ref, vwT_ref, gate_ref, out_ref,
                ps_ref, score_ref, sel_ref, m_ref, l_ref, acc_ref, o_ref, *, nc, ns, n_sel):
    c = pl.program_id(2)
    ratio = SLC_BLOCK // CMP_STRIDE
    W = HG * QB
    q = jnp.concatenate([q_ref[h * DH:(h + 1) * DH, :] for h in range(HG)], axis=1)

    def gate_rows(br):
        return jnp.concatenate(
            [jnp.broadcast_to(gate_ref[0, 0, br * HG + h:br * HG + h + 1, :], (DH, QB)) for h in range(HG)],
            axis=1)

    s = _dot(kc_ref[0, 0], q)
    row = lax.broadcasted_iota(jnp.int32, (nc, W), 0)
    col = lax.broadcasted_iota(jnp.int32, (nc, W), 1)
    t = c * QB + (col & (QB - 1))
    valid = row * CMP_STRIDE + (CMP_BLOCK - 1) <= t
    s = jnp.where(valid, s, MASK_VALUE)
    mx = jnp.max(s, axis=0, keepdims=True)
    e = jnp.where(valid, jnp.exp(s - mx), 0.0)
    den = jnp.maximum(jnp.sum(e, axis=0, keepdims=True), 1e-30)
    p = e / den
    o_ref[...] = gate_rows(0) * _dot(vcT_ref[0, 0], p.astype(BF16))

    psum = p[:, 0:QB]
    for h in range(1, HG):
        psum = psum + p[:, h * QB:(h + 1) * QB]
    ps_ref[0:8, :] = jnp.zeros((8, QB), F32)
    ps_ref[8:8 + nc, :] = psum
    imp = ps_ref[pl.ds(8, ns, stride=ratio), :]
    for r in range(1, ratio):
        imp = imp + ps_ref[pl.ds(8 + r, ns, stride=ratio), :]
    imp = imp + ps_ref[pl.ds(7, ns, stride=ratio), :]
    blk = lax.broadcasted_iota(jnp.int32, (ns, QB), 0)
    tq = c * QB + lax.broadcasted_iota(jnp.int32, (ns, QB), 1)
    cur = tq // SLC_BLOCK
    forced = (blk == 0) | (blk == cur) | (blk == cur - 1)
    score = jnp.where(forced, FORCE_SCORE, imp)
    score = jnp.where(blk > cur, MASK_VALUE, score)
    score_ref[...] = score

    def rank_body(jp, rank):
        other = jnp.broadcast_to(score_ref[pl.ds(jp, 1), :], (ns, QB))
        beats = (other > score) | ((other == score) & (jp < blk))
        return rank + jnp.where(beats, 1.0, 0.0)

    nvis = jnp.minimum(2 * c + 2, ns)
    rank = lax.fori_loop(0, nvis, rank_body, jnp.zeros((ns, QB), F32))
    sel_ref[...] = jnp.where(rank < float(n_sel), 1.0, 0.0)

    rr = lax.broadcasted_iota(jnp.int32, (QB, QB), 0)
    cc = lax.broadcasted_iota(jnp.int32, (QB, QB), 1)

    def reset():
        m_ref[...] = jnp.full((1, W), MASK_VALUE, F32)
        l_ref[...] = jnp.zeros((1, W), F32)
        acc_ref[...] = jnp.zeros((DH, W), F32)

    def tile(k, vT, mask):
        s = _dot(k, q)
        if mask is not None:
            s = jnp.concatenate(
                [jnp.where(mask, s[:, h * QB:(h + 1) * QB], MASK_VALUE) for h in range(HG)], axis=1)
        m_old = m_ref[...]
        m_new = jnp.maximum(m_old, jnp.max(s, axis=0, keepdims=True))
        alpha = jnp.exp(m_old - m_new)
        pt = jnp.exp(s - m_new)
        l_ref[...] = alpha * l_ref[...] + jnp.sum(pt, axis=0, keepdims=True)
        acc_ref[...] = alpha * acc_ref[...] + _dot(vT, pt.astype(BF16))
        m_ref[...] = m_new

    def finish(br):
        o_ref[...] += gate_rows(br) * (acc_ref[...] / jnp.maximum(l_ref[...], 1e-30))

    def ktile(k_ref, kt):
        return k_ref[0, 0, pl.ds(pl.multiple_of(kt * QB, QB), QB), :]

    reset()
    tile(ktile(ks_ref, c), vsT_ref[0, 0, c], rr <= cc)

    def sel_body(kt, carry):
        b0 = jnp.broadcast_to(sel_ref[pl.ds(2 * kt, 1), :], (SLC_BLOCK, QB))
        b1 = jnp.broadcast_to(sel_ref[pl.ds(2 * kt + 1, 1), :], (SLC_BLOCK, QB))
        mask = jnp.concatenate([b0, b1], axis=0) > 0.5
        tile(ktile(ks_ref, kt), vsT_ref[0, 0, kt], mask)
        return carry

    lax.fori_loop(0, c, sel_body, 0)
    finish(1)

    reset()
    tile(ktile(kw_ref, c), vwT_ref[0, 0, c], rr <= cc)
    nwt = WINDOW // QB

    def win_body(kt, carry):
        tile(ktile(kw_ref, kt), vwT_ref[0, 0, kt], None)
        return carry

    lax.fori_loop(jnp.maximum(c - (nwt - 1), 0), c, win_body, 0)

    @pl.when(c >= nwt)
    def _():
        tile(ktile(kw_ref, c - nwt), vwT_ref[0, 0, c - nwt], rr > cc)

    finish(2)

    o = o_ref[...]
    for hp in range(HG // 2):
        pair = jnp.concatenate([o[:, (2 * hp) * QB:(2 * hp + 1) * QB],
                                o[:, (2 * hp + 1) * QB:(2 * hp + 2) * QB]], axis=0)
        out_ref[:, hp * 2 * DH:(hp + 1) * 2 * DH] = pair.T.astype(BF16)


def _nsa(qT, kc, vcT, ks, vsT, kw, vwT, gates, *, b, s):
    nq = s // QB
    nc = s // CMP_STRIDE
    ns = s // SLC_BLOCK
    n_sel = min(N_SELECT, ns)
    W = HG * QB
    bg = lambda i, j, c: (i, j, 0, 0)
    bg5 = lambda i, j, c: (i, j, 0, 0, 0)
    return pl.pallas_call(
        functools.partial(_nsa_kernel, nc=nc, ns=ns, n_sel=n_sel),
        grid=(b, NG, nq),
        in_specs=[
            pl.BlockSpec((HG * DH, QB), lambda i, j, c: (j, i * nq + c)),
            pl.BlockSpec((1, 1, nc, DH), bg),
            pl.BlockSpec((1, 1, DH, nc), bg),
            pl.BlockSpec((1, 1, s, DH), bg),
            pl.BlockSpec((1, 1, nq, DH, QB), bg5),
            pl.BlockSpec((1, 1, s, DH), bg),
            pl.BlockSpec((1, 1, nq, DH, QB), bg5),
            pl.BlockSpec((1, 1, 16, QB), lambda i, j, c: (i, j, 0, c)),
        ],
        out_specs=pl.BlockSpec((QB, HG * DH), lambda i, j, c: (i * nq + c, j)),
        out_shape=jax.ShapeDtypeStruct((b * s, NSA_Q_W), BF16),
        scratch_shapes=[
            pltpu.VMEM((8 + nc, QB), F32),
            pltpu.VMEM((ns, QB), F32),
            pltpu.VMEM((ns, QB), F32),
            pltpu.VMEM((1, W), F32),
            pltpu.VMEM((1, W), F32),
            pltpu.VMEM((DH, W), F32),
            pltpu.VMEM((DH, W), F32),
        ],
        compiler_params=_params(3),
        name="nsa",
    )(qT, kc, vcT, ks, vsT, kw, vwT, gates)


def _ret_kernel(q_ref, k_ref, v_ref, g_ref, cos_ref, sin_ref, decay_ref, xi_ref, zeta_ref, gch_ref,
                o_ref, r_ref, *, nchunk):
    @pl.when(pl.program_id(2) == 0)
    def _():
        r_ref[...] = jnp.zeros((DK, DK), F32)

    decay = decay_ref[0]
    xi = xi_ref[0]
    zeta = zeta_ref[0]
    gch = gch_ref[0, 0:1, :]
    for ci in range(nchunk):
        rows = slice(ci * RET_CHUNK, (ci + 1) * RET_CHUNK)
        cosv = cos_ref[rows, :]
        sinv = sin_ref[rows, :]

        def rope(x):
            return x * cosv + pltpu.roll(x, DK // 2, 1) * sinv

        q = rope(q_ref[rows, :])
        k = rope(k_ref[rows, :]) * (DK ** -0.5)
        v = v_ref[rows, :].astype(BF16)
        r_prev = r_ref[...]
        inner = _dot_nt(q.astype(BF16), k.astype(BF16)) * decay
        y = _dot(inner.astype(BF16), v) + _dot((q * xi).astype(BF16), r_prev.astype(BF16))
        y = y * lax.rsqrt(jnp.mean(y * y, axis=-1, keepdims=True) + EPS)
        g = g_ref[rows, :]
        o_ref[rows, :] = (g * jax.nn.sigmoid(g) * y).astype(BF16)
        r_ref[...] = gch * r_prev + _dot_tn((k * zeta).astype(BF16), v)


def _retention(zn, cosf, sinf, decay, xi, zeta, gch, *, b, s, ts):
    nst = s // ts
    col = lambda off: (lambda i, h, j: (i * nst + j, off + h))
    tab = lambda i, h, j: (j, 0)
    per_head = lambda i, h, j: (h, 0, 0)
    return pl.pallas_call(
        functools.partial(_ret_kernel, nchunk=ts // RET_CHUNK),
        grid=(b, RET_HEADS, nst),
        in_specs=[
            pl.BlockSpec((ts, DK), col(0)),
            pl.BlockSpec((ts, DK), col(RET_HEADS)),
            pl.BlockSpec((ts, DK), col(2 * RET_HEADS)),
            pl.BlockSpec((ts, DK), col(3 * RET_HEADS)),
            pl.BlockSpec((ts, DK), tab),
            pl.BlockSpec((ts, DK), tab),
            pl.BlockSpec((1, RET_CHUNK, RET_CHUNK), per_head),
            pl.BlockSpec((1, RET_CHUNK, DK), per_head),
            pl.BlockSpec((1, RET_CHUNK, DK), per_head),
            pl.BlockSpec((1, 8, DK), per_head),
        ],
        out_specs=pl.BlockSpec((ts, DK), lambda i, h, j: (i * nst + j, h)),
        out_shape=jax.ShapeDtypeStruct((b * s, RET_W), BF16),
        scratch_shapes=[pltpu.VMEM((DK, DK), F32)],
        compiler_params=_params(3),
        name="retention",
    )(zn, zn, zn, zn, cosf, sinf, decay, xi, zeta, gch)


def _mix_kernel(x_ref, ya_ref, yb_ref, ma_ref, mb_ref, wa_ref, wb_ref, wo_ref, o_ref):
    ua = _dot(ya_ref[...], wa_ref[...])
    ub = _dot(yb_ref[...], wb_ref[...])
    mix = jax.nn.sigmoid(ma_ref[...]) * ua + jax.nn.sigmoid(mb_ref[...]) * ub
    o_ref[...] = x_ref[...] + _dot(mix.astype(BF16), wo_ref[...])


def _mix(x2, ya, yb, zn, wa, wb, wo, *, tm):
    n = x2.shape[0]
    full = lambda i: (0, 0)
    ma_blk = 4 * RET_W // D_MODEL
    return pl.pallas_call(
        _mix_kernel,
        grid=(n // tm,),
        in_specs=[
            pl.BlockSpec((tm, D_MODEL), lambda i: (i, 0)),
            pl.BlockSpec((tm, NSA_Q_W), lambda i: (i, 0)),
            pl.BlockSpec((tm, RET_W), lambda i: (i, 0)),
            pl.BlockSpec((tm, D_MODEL), lambda i: (i, ma_blk)),
            pl.BlockSpec((tm, D_MODEL), lambda i: (i, ma_blk + 1)),
            pl.BlockSpec((NSA_Q_W, D_MODEL), full),
            pl.BlockSpec((RET_W, D_MODEL), full),
            pl.BlockSpec((D_MODEL, D_MODEL), full),
        ],
        out_specs=pl.BlockSpec((tm, D_MODEL), lambda i: (i, 0)),
        out_shape=jax.ShapeDtypeStruct((n, D_MODEL), F32),
        compiler_params=_params(1),
        name="mix",
    )(x2, ya, yb, zn, zn, wa, wb, wo)


def _ffn_kernel(x_ref, p_ref, gm_ref, w1_ref, w2_ref, gp_ref, wg_ref, wp_ref, o_ref, *, ff_chunk):
    x = x_ref[...]
    h = _rms_rows(x, gm_ref[...]).astype(BF16)
    for j in range(D_FF // ff_chunk):
        cols = slice(j * ff_chunk, (j + 1) * ff_chunk)
        u = jnp.square(jnp.maximum(_dot(h, w1_ref[:, cols]), 0.0)).astype(BF16)
        x = x + _dot(u, w2_ref[cols, :])
    gate = jax.nn.sigmoid(_dot(_rms_rows(x, gp_ref[...]).astype(BF16), wg_ref[...]))
    o_ref[...] = x + gate * _dot(p_ref[...].astype(BF16), wp_ref[...])


def _ffn(x2, p2, gm, w1, w2, gp, wg, wp, *, tm):
    n = x2.shape[0]
    full = lambda i: (0, 0)
    once = pl.Buffered(1)
    return pl.pallas_call(
        functools.partial(_ffn_kernel, ff_chunk=1024),
        grid=(n // tm,),
        in_specs=[
            pl.BlockSpec((tm, D_MODEL), lambda i: (i, 0)),
            pl.BlockSpec((tm, PLE_DIM), lambda i: (i, 0)),
            pl.BlockSpec((1, D_MODEL), full),
            pl.BlockSpec((D_MODEL, D_FF), full, pipeline_mode=once),
            pl.BlockSpec((D_FF, D_MODEL), full, pipeline_mode=once),
            pl.BlockSpec((1, D_MODEL), full),
            pl.BlockSpec((D_MODEL, D_MODEL), full, pipeline_mode=once),
            pl.BlockSpec((PLE_DIM, D_MODEL), full, pipeline_mode=once),
        ],
        out_specs=pl.BlockSpec((tm, D_MODEL), lambda i: (i, 0)),
        out_shape=jax.ShapeDtypeStruct((n, D_MODEL), F32),
        compiler_params=_params(1),
        name="ffn",
    )(x2, p2, gm, w1, w2, gp, wg, wp)


def _retention_tables(s):
    half = DK // 2
    pos = jnp.arange(s, dtype=F32)
    inv = ROPE_BASE ** (-jnp.arange(half, dtype=F32) / half)
    ang = pos[:, None] * inv[None, :]
    cos, sin = jnp.cos(ang), jnp.sin(ang)
    cosf = jnp.concatenate([cos, cos], axis=-1)
    sinf = jnp.concatenate([-sin, sin], axis=-1)
    C = RET_CHUNK
    gamma = 1.0 - 2.0 ** (-5.0 - jnp.arange(RET_HEADS, dtype=F32))
    lg = jnp.log(gamma)
    n = jnp.arange(C, dtype=F32)
    diff = n[:, None] - n[None, :]
    decay = jnp.where(diff >= 0, jnp.exp(lg[:, None, None] * jnp.maximum(diff, 0.0)), 0.0)
    xi = jnp.broadcast_to(jnp.exp(lg[:, None] * (n + 1.0))[:, :, None], (RET_HEADS, C, DK))
    zeta = jnp.broadcast_to(jnp.exp(lg[:, None] * (C - 1.0 - n))[:, :, None], (RET_HEADS, C, DK))
    gch = jnp.broadcast_to(jnp.exp(lg * C)[:, None, None], (RET_HEADS, 8, DK))
    return cosf, sinf, decay, xi, zeta, gch


def _layer(x2, p2, w, tabs, *, b, s):
    n = b * s
    nq = s // QB
    nc = s // CMP_STRIDE
    zn, qT, cT, kT, vT, gT = _proj(x2, w["norm_mix"], w["wn"], w["wt"], w["qg"], w["kgc"], tm=256)

    def per_group(a):
        return a.reshape(NG, DH, b, s).transpose(2, 0, 3, 1)

    def stride_rows(a):
        rows = a.reshape(b, NG, nc, CMP_STRIDE * DH)
        last = jnp.tile(a[:, :, s - 1:s, :], (1, 1, 1, CMP_STRIDE))
        pad = jnp.zeros((b, NG, 7, CMP_STRIDE * DH), a.dtype)
        return jnp.concatenate([rows, last, pad], axis=2)

    def key_tiles(a):
        return a.reshape(NG, DH, b, nq, QB).transpose(2, 0, 3, 1, 4)

    kc, vcT = _compress(stride_rows(per_group(cT[:NSA_KV_W])), stride_rows(per_group(cT[NSA_KV_W:])),
                        w["pos_k"], w["pos_v"], w["w1k"], w["w2k"], w["w1v"], w["w2vT"], w["kgr"], nc=nc)
    gates = gT[:NSA_GATE_W].reshape(3, NG, HG, b, s).transpose(3, 1, 0, 2, 4).reshape(b, NG, 3 * HG, s)
    gates = jnp.pad(gates, ((0, 0), (0, 0), (0, 16 - 3 * HG), (0, 0)))
    ya = _nsa(qT, kc, vcT, per_group(kT[:NSA_KV_W]), key_tiles(vT[:NSA_KV_W]),
              per_group(kT[NSA_KV_W:]), key_tiles(vT[NSA_KV_W:]), gates, b=b, s=s)
    yb = _retention(zn, *tabs, b=b, s=s, ts=min(s, 1024))
    x2 = _mix(x2, ya, yb, zn, w["wa"], w["wb"], w["wo"], tm=512)
    return _ffn(x2, p2, w["norm_mlp"], w["w1"], w["w2"], w["norm_ple"], w["wg"], w["wp"], tm=512)


def kernel(x, p, norm_mix, w_in, nsa_q_norm, nsa_k_norm, cmp_pos_k, cmp_pos_v, cmp_w1_k, cmp_w2_k,
           cmp_w1_v, cmp_w2_v, w_up_nsa, w_up_ret, w_out, norm_mlp, w_ff1, w_ff2, norm_ple, w_ple,
           w_ple_gate):
    b, s, _ = x.shape
    depth = w_in.shape[0]
    tabs = _retention_tables(s)
    x2 = x.reshape(b * s, D_MODEL)
    q_end = NSA_Q_W
    kv = NSA_KV_W
    g_end = NSA_Q_W + 6 * kv + NSA_GATE_W
    for i in range(depth):
        wi = w_in[i]
        wt = jnp.concatenate([
            wi[:, 0:q_end + 2 * kv],
            wi[:, q_end + 2 * kv:q_end + 3 * kv], wi[:, q_end + 4 * kv:q_end + 5 * kv],
            wi[:, q_end + 3 * kv:q_end + 4 * kv], wi[:, q_end + 5 * kv:q_end + 6 * kv],
            wi[:, q_end + 6 * kv:g_end],
            jnp.zeros((D_MODEL, GATE_ROWS - NSA_GATE_W), wi.dtype)], axis=1).T.astype(BF16)
        w = dict(
            norm_mix=norm_mix[i][None, :], wn=wi[:, g_end:].astype(BF16), wt=wt,
            qg=nsa_q_norm[i][:, None], kgc=nsa_k_norm[i][:, None], kgr=nsa_k_norm[i][None, :],
            pos_k=cmp_pos_k[i].reshape(1, CMP_BLOCK * DH), pos_v=cmp_pos_v[i].reshape(1, CMP_BLOCK * DH),
            w1k=cmp_w1_k[i].astype(BF16), w2k=cmp_w2_k[i].astype(BF16),
            w1v=cmp_w1_v[i].astype(BF16), w2vT=cmp_w2_v[i].T.astype(BF16),
            wa=w_up_nsa[i].astype(BF16), wb=w_up_ret[i].astype(BF16), wo=w_out[i].astype(BF16),
            norm_mlp=norm_mlp[i][None, :], w1=w_ff1[i].astype(BF16), w2=w_ff2[i].astype(BF16),
            norm_ple=norm_ple[i][None, :], wg=w_ple_gate[i].astype(BF16), wp=w_ple[i].astype(BF16),
        )
        x2 = _layer(x2, p[i].reshape(b * s, PLE_DIM), w, tabs, b=b, s=s)
    return x2.reshape(b, s, D_MODEL)
```

```python
import functools
import math

import jax
import jax.numpy as jnp
from jax import lax
from jax.experimental import pallas as pl
from jax.experimental.pallas import tpu as pltpu

D_MODEL = 1024
PLE_DIM = 256
NSA_HEADS = 8
DH = 64
NG = 2
HG = NSA_HEADS // NG
CMP_BLOCK = 32
CMP_STRIDE = 16
CMP_HIDDEN = 4 * DH
SLC_BLOCK = 64
N_SELECT = 16
WINDOW = 512
QB = 128
RET_HEADS = 4
DK = 128
RET_CHUNK = 128
ROPE_BASE = 10000.0
D_FF = 4 * D_MODEL
EPS = 1e-6
MASK_VALUE = -1e30
FORCE_SCORE = 1e4

NSA_Q_W = NSA_HEADS * DH
NSA_KV_W = NG * DH
NSA_GATE_W = 3 * NSA_HEADS
RET_W = RET_HEADS * DK
NAT_W = 4 * RET_W + 2 * D_MODEL
GATE_ROWS = 32
T_ROWS = NSA_Q_W + 6 * NSA_KV_W + GATE_ROWS

KT = 512
NB = KT // SLC_BLOCK
KA = 128
VROWS = DH + 16

VMEM_LIMIT = 56 * 1024 * 1024

F32 = jnp.float32
BF16 = jnp.bfloat16


def _dot(a, b):
    return jnp.dot(a, b, preferred_element_type=F32)


def _dot_nt(a, b):
    return lax.dot_general(a, b, (((1,), (1,)), ((), ())), preferred_element_type=F32)


def _dot_tn(a, b):
    return lax.dot_general(a, b, (((0,), (0,)), ((), ())), preferred_element_type=F32)


def _rms_rows(x, g):
    y = x * lax.rsqrt(jnp.mean(x * x, axis=-1, keepdims=True) + EPS)
    return y * g


def _params(n_axes):
    return pltpu.CompilerParams(dimension_semantics=("arbitrary",) * n_axes,
                                vmem_limit_bytes=VMEM_LIMIT)


def _proj_kernel(x_ref, g_ref, wn_ref, wt_ref, qg_ref, kg_ref,
                 zn_ref, qT_ref, cT_ref, kT_ref, vT_ref, gT_ref):
    h = _rms_rows(x_ref[...], g_ref[...]).astype(BF16)
    zn_ref[...] = _dot(h, wn_ref[...])
    zt = _dot_nt(wt_ref[...], h)
    qg = qg_ref[...]
    kg = kg_ref[...]

    def norm_cols(z, g):
        return z * lax.rsqrt(jnp.mean(z * z, axis=0, keepdims=True) + EPS) * g

    for hh in range(NSA_HEADS):
        rows = slice(hh * DH, (hh + 1) * DH)
        qT_ref[rows, :] = (norm_cols(zt[rows, :], qg) * (DH ** -0.5)).astype(BF16)
    o = NSA_Q_W
    cT_ref[...] = zt[o:o + 2 * NSA_KV_W, :]
    o += 2 * NSA_KV_W
    for j in range(2 * NG):
        rows = slice(o + j * DH, o + (j + 1) * DH)
        kT_ref[j * DH:(j + 1) * DH, :] = norm_cols(zt[rows, :], kg).astype(BF16)
    o += 2 * NSA_KV_W
    vT_ref[...] = zt[o:o + 2 * NSA_KV_W, :].astype(BF16)
    o += 2 * NSA_KV_W
    gT_ref[...] = jax.nn.sigmoid(zt[o:o + GATE_ROWS, :])


def _proj(x2, g, wn, wt, qg, kg, *, tm):
    n = x2.shape[0]
    full = lambda i: (0, 0)
    colblk = lambda i: (0, i)
    return pl.pallas_call(
        _proj_kernel,
        grid=(n // tm,),
        in_specs=[
            pl.BlockSpec((tm, D_MODEL), lambda i: (i, 0)),
            pl.BlockSpec((1, D_MODEL), full),
            pl.BlockSpec((D_MODEL, NAT_W), full),
            pl.BlockSpec((T_ROWS, D_MODEL), full),
            pl.BlockSpec((DH, 1), full),
            pl.BlockSpec((DH, 1), full),
        ],
        out_specs=[
            pl.BlockSpec((tm, NAT_W), lambda i: (i, 0)),
            pl.BlockSpec((NSA_Q_W, tm), colblk),
            pl.BlockSpec((2 * NSA_KV_W, tm), colblk),
            pl.BlockSpec((2 * NSA_KV_W, tm), colblk),
            pl.BlockSpec((2 * NSA_KV_W, tm), colblk),
            pl.BlockSpec((GATE_ROWS, tm), colblk),
        ],
        out_shape=[
            jax.ShapeDtypeStruct((n, NAT_W), F32),
            jax.ShapeDtypeStruct((NSA_Q_W, n), BF16),
            jax.ShapeDtypeStruct((2 * NSA_KV_W, n), F32),
            jax.ShapeDtypeStruct((2 * NSA_KV_W, n), BF16),
            jax.ShapeDtypeStruct((2 * NSA_KV_W, n), BF16),
            jax.ShapeDtypeStruct((GATE_ROWS, n), F32),
        ],
        compiler_params=_params(1),
        name="proj",
    )(x2, g, wn, wt, qg, kg)


def _gelu_tanh(x):
    cdf = 0.5 * (1.0 + jnp.tanh(math.sqrt(2.0 / math.pi) * (x + 0.044715 * (x ** 3))))
    return x * cdf


def _cmp_kernel(zk_ref, zv_ref, pk_ref, pv_ref, w1k_ref, w2k_ref, w1v_ref, w2vT_ref, kg_ref,
                kc_ref, vcT_ref, *, nc):
    half = CMP_STRIDE * DH

    def hidden(z_ref, p_ref, w1_ref):
        a = (z_ref[0, 0, 0:nc, :] + p_ref[:, 0:half]).astype(BF16)
        b = (z_ref[0, 0, 1:nc + 1, :] + p_ref[:, half:2 * half]).astype(BF16)
        return _gelu_tanh(_dot(a, w1_ref[0:half, :]) + _dot(b, w1_ref[half:2 * half, :]))

    hk = hidden(zk_ref, pk_ref, w1k_ref).astype(BF16)
    kc_ref[0, 0] = _rms_rows(_dot(hk, w2k_ref[...]), kg_ref[...]).astype(BF16)
    hv = hidden(zv_ref, pv_ref, w1v_ref).astype(BF16)
    vcT_ref[0, 0] = _dot_nt(w2vT_ref[...], hv).astype(BF16)


def _compress(zk, zv, pk, pv, w1k, w2k, w1v, w2vT, kg, *, nc):
    b = zk.shape[0]
    rows = zk.shape[2]
    half = CMP_STRIDE * DH
    zspec = pl.BlockSpec((1, 1, rows, half), lambda i, j: (i, j, 0, 0))
    full = lambda i, j: (0, 0)
    return pl.pallas_call(
        functools.partial(_cmp_kernel, nc=nc),
        grid=(b, NG),
        in_specs=[
            zspec, zspec,
            pl.BlockSpec((1, 2 * half), full),
            pl.BlockSpec((1, 2 * half), full),
            pl.BlockSpec((2 * half, CMP_HIDDEN), full),
            pl.BlockSpec((CMP_HIDDEN, DH), full),
            pl.BlockSpec((2 * half, CMP_HIDDEN), full),
            pl.BlockSpec((DH, CMP_HIDDEN), full),
            pl.BlockSpec((1, DH), full),
        ],
        out_specs=[
            pl.BlockSpec((1, 1, nc, DH), lambda i, j: (i, j, 0, 0)),
            pl.BlockSpec((1, 1, DH, nc), lambda i, j: (i, j, 0, 0)),
        ],
        out_shape=[
            jax.ShapeDtypeStruct((b, NG, nc, DH), BF16),
            jax.ShapeDtypeStruct((b, NG, DH, nc), BF16),
        ],
        compiler_params=_params(2),
        name="compress",
    )(zk, zv, pk, pv, w1k, w2k, w1v, w2vT, kg)


def _nsa_kernel(q_ref, kc_ref, vcT_ref, ks_ref, vsT_ref, kw_ref, vwT_ref, gate_ref, out_ref,
                qz_ref, qw_ref, qs0_ref, qs1_ref, ps_ref, score_ref, sel_ref,
                ms_ref, as_ref, mw_ref, aw_ref, o_ref, *, nc, ns, n_sel):
    c = pl.program_id(2)
    ratio = SLC_BLOCK // CMP_STRIDE
    W = HG * QB
    nwt = WINDOW // QB
    q = jnp.concatenate([q_ref[h * DH:(h + 1) * DH, :] for h in range(HG)], axis=1)
    qz_ref[...] = jnp.concatenate([q, jnp.zeros((KA - DH, W), BF16)], axis=0)
    for r in (qw_ref, qs0_ref, qs1_ref):
        r[0:DH, :] = q
        r[DH + 16:KA, :] = jnp.zeros((KA - DH - 16, W), BF16)

    def set_bias(ref, rows):
        b = jnp.concatenate([rows] * HG, axis=1)
        ref[DH:DH + 16, :] = jnp.concatenate([b, jnp.zeros((16 - NB, W), F32)], axis=0).astype(BF16)

    def gate_rows(br):
        return jnp.concatenate(
            [jnp.broadcast_to(gate_ref[0, 0, br * HG + h:br * HG + h + 1, :], (DH, QB)) for h in range(HG)],
            axis=1)

    rr = lax.broadcasted_iota(jnp.int32, (QB, QB), 0)
    cc = lax.broadcasted_iota(jnp.int32, (QB, QB), 1)

    def reset(m_ref, a_ref):
        m_ref[...] = jnp.full((1, W), MASK_VALUE, F32)
        a_ref[...] = jnp.zeros((VROWS, W), F32)

    def attend(qa, k, vT, m_ref, a_ref, emask=None):
        s = _dot(k, qa)
        if emask is not None:
            top = jnp.concatenate(
                [jnp.where(emask, s[0:QB, h * QB:(h + 1) * QB], MASK_VALUE) for h in range(HG)], axis=1)
            s = top if s.shape[0] == QB else jnp.concatenate([top, s[QB:, :]], axis=0)
        m_old = m_ref[...]
        m_new = jnp.maximum(m_old, jnp.max(s, axis=0, keepdims=True))
        pt = jnp.exp(s - m_new).astype(BF16)
        a_ref[...] = jnp.exp(m_old - m_new) * a_ref[...] + _dot(vT, pt)
        m_ref[...] = m_new

    def normalised(a_ref):
        a = a_ref[...]
        return a[0:DH, :] * (1.0 / jnp.maximum(a[DH:DH + 1, :], 1e-30))

    diag = pl.ds(pl.multiple_of(c * QB, QB), QB)
    qz = qz_ref[...]
    reset(ms_ref, as_ref)
    attend(qz, ks_ref[0, 0, diag, :], vsT_ref[0, 0, c], ms_ref, as_ref, emask=rr <= cc)
    reset(mw_ref, aw_ref)
    attend(qz, kw_ref[0, 0, diag, :], vwT_ref[0, 0, c], mw_ref, aw_ref, emask=rr <= cc)

    kt0 = jnp.maximum(c - nwt, 0)
    b8 = lax.broadcasted_iota(jnp.int32, (NB, QB), 0)
    tile_of_col = ((b8 - 2 * kt0) & (NB - 1)) >> 1
    set_bias(qw_ref, jnp.where(tile_of_col < jnp.minimum(c, nwt), 0.0, MASK_VALUE))
    kwc = kw_ref[0, 0, pl.ds(pl.multiple_of(kt0 * QB, QB), nwt * QB), :]
    vwc = jnp.concatenate([vwT_ref[0, 0, kt0 + i] for i in range(nwt)], axis=1)
    attend(qw_ref[...], kwc, vwc, mw_ref, aw_ref, emask=(rr > cc) | (c < nwt))

    s = _dot(kc_ref[0, 0], q)
    row = lax.broadcasted_iota(jnp.int32, (nc, W), 0)
    col = lax.broadcasted_iota(jnp.int32, (nc, W), 1)
    t = c * QB + (col & (QB - 1))
    valid = row * CMP_STRIDE + (CMP_BLOCK - 1) <= t
    s = jnp.where(valid, s, MASK_VALUE)
    mx = jnp.max(s, axis=0, keepdims=True)
    e = jnp.where(valid, jnp.exp(s - mx), 0.0)
    p = e * (1.0 / jnp.maximum(jnp.sum(e, axis=0, keepdims=True), 1e-30))
    o_ref[...] = gate_rows(0) * _dot(vcT_ref[0, 0], p.astype(BF16))

    psum = p[:, 0:QB]
    for h in range(1, HG):
        psum = psum + p[:, h * QB:(h + 1) * QB]
    ps_ref[0:8, :] = jnp.zeros((8, QB), F32)
    ps_ref[8:8 + nc, :] = psum
    imp = ps_ref[pl.ds(8, ns, stride=ratio), :]
    for r in range(1, ratio):
        imp = imp + ps_ref[pl.ds(8 + r, ns, stride=ratio), :]
    imp = imp + ps_ref[pl.ds(7, ns, stride=ratio), :]
    blk = lax.broadcasted_iota(jnp.int32, (ns, QB), 0)
    tq = c * QB + lax.broadcasted_iota(jnp.int32, (ns, QB), 1)
    cur = tq // SLC_BLOCK
    score = jnp.where(blk == cur, FORCE_SCORE, imp)
    score = jnp.where(blk == cur - 1, 2 * FORCE_SCORE, score)
    score = jnp.where(blk == 0, 3 * FORCE_SCORE, score)
    score = jnp.where(blk > cur, MASK_VALUE, score)
    score_ref[...] = score

    before_diag = blk < 2 * c
    work = score
    thr = None
    for _ in range(n_sel):
        thr = jnp.max(work, axis=0, keepdims=True)
        work = jnp.where(work == thr, -jnp.inf, work)
    fast = score >= thr
    cnt = jnp.sum(jnp.where(fast, 1.0, 0.0), axis=0, keepdims=True)
    n_tied = jnp.sum(jnp.where(cnt != float(n_sel), 1.0, 0.0))
    all_selected = 2 * c + 2 <= n_sel
    sel_ref[...] = jnp.where(before_diag & (fast | all_selected), 1.0, 0.0)

    @pl.when(jnp.logical_and(jnp.logical_not(all_selected), n_tied > 0.0))
    def _():
        def rank_body(jp, rank):
            other = jnp.broadcast_to(score_ref[pl.ds(jp, 1), :], (ns, QB))
            beats = (other > score) | ((other == score) & (jp < blk))
            return rank + jnp.where(beats, 1.0, 0.0)

        rank = lax.fori_loop(0, jnp.minimum(2 * c + 2, ns), rank_body, jnp.zeros((ns, QB), F32))
        sel_ref[...] = jnp.where(before_diag & (rank < float(n_sel)), 1.0, 0.0)

    def sel_tile(qs_ref, t4):
        rows = sel_ref[pl.ds(pl.multiple_of(t4 * NB, NB), NB), :]
        set_bias(qs_ref, jnp.where(rows > 0.5, 0.0, MASK_VALUE))
        k = ks_ref[0, 0, pl.ds(pl.multiple_of(t4 * KT, KT), KT), :]
        vT = jnp.concatenate([vsT_ref[0, 0, (KT // QB) * t4 + i] for i in range(KT // QB)], axis=1)
        attend(qs_ref[...], k, vT, ms_ref, as_ref)

    def sel_body(j, carry):
        sel_tile(qs0_ref, 2 * j)
        sel_tile(qs1_ref, 2 * j + 1)
        return carry

    n_tiles = (c * QB + KT - 1) // KT
    lax.fori_loop(0, (n_tiles + 1) // 2, sel_body, 0)

    o = o_ref[...] + gate_rows(1) * normalised(as_ref) + gate_rows(2) * normalised(aw_ref)
    for hp in range(HG // 2):
        pair = jnp.concatenate([o[:, (2 * hp) * QB:(2 * hp + 1) * QB],
                                o[:, (2 * hp + 1) * QB:(2 * hp + 2) * QB]], axis=0)
        out_ref[:, hp * 2 * DH:(hp + 1) * 2 * DH] = pair.T.astype(BF16)


def _nsa(qT, kc, vcT, ks, vsT, kw, vwT, gates, *, b, s):
    assert s % (2 * KT) == 0 and s >= WINDOW + QB
    nq = s // QB
    nc = s // CMP_STRIDE
    ns = s // SLC_BLOCK
    n_sel = min(N_SELECT, ns)
    W = HG * QB
    bg = lambda i, j, c: (i, j, 0, 0)
    bg5 = lambda i, j, c: (i, j, 0, 0, 0)
    qa = pltpu.VMEM((KA, W), BF16)
    return pl.pallas_call(
        functools.partial(_nsa_kernel, nc=nc, ns=ns, n_sel=n_sel),
        grid=(b, NG, nq),
        in_specs=[
            pl.BlockSpec((HG * DH, QB), lambda i, j, c: (j, i * nq + c)),
            pl.BlockSpec((1, 1, nc, DH), bg),
            pl.BlockSpec((1, 1, DH, nc), bg),
            pl.BlockSpec((1, 1, s, KA), bg),
            pl.BlockSpec((1, 1, nq, VROWS, QB), bg5),
            pl.BlockSpec((1, 1, s, KA), bg),
            pl.BlockSpec((1, 1, nq, VROWS, QB), bg5),
            pl.BlockSpec((1, 1, 16, QB), lambda i, j, c: (i, j, 0, c)),
        ],
        out_specs=pl.BlockSpec((QB, HG * DH), lambda i, j, c: (i * nq + c, j)),
        out_shape=jax.ShapeDtypeStruct((b * s, NSA_Q_W), BF16),
        scratch_shapes=[
            qa, qa, qa, qa,
            pltpu.VMEM((8 + nc, QB), F32),
            pltpu.VMEM((ns, QB), F32),
            pltpu.VMEM((ns, QB), F32),
            pltpu.VMEM((1, W), F32),
            pltpu.VMEM((VROWS, W), F32),
            pltpu.VMEM((1, W), F32),
            pltpu.VMEM((VROWS, W), F32),
            pltpu.VMEM((DH, W), F32),
        ],
        compiler_params=_params(3),
        name="nsa",
    )(qT, kc, vcT, ks, vsT, kw, vwT, gates)


def _ret_kernel(q_ref, k_ref, v_ref, g_ref, cos_ref, sin_ref, decay_ref, xi_ref, zeta_ref, gch_ref,
                o_ref, r_ref, *, nchunk):
    @pl.when(pl.program_id(2) == 0)
    def _():
        r_ref[...] = jnp.zeros((DK, DK), F32)

    decay = decay_ref[0]
    xi = xi_ref[0]
    zeta = zeta_ref[0]
    gch = gch_ref[0, 0:1, :]
    for ci in range(nchunk):
        rows = slice(ci * RET_CHUNK, (ci + 1) * RET_CHUNK)
        cosv = cos_ref[rows, :]
        sinv = sin_ref[rows, :]

        def rope(x):
            return x * cosv + pltpu.roll(x, DK // 2, 1) * sinv

        q = rope(q_ref[rows, :])
        k = rope(k_ref[rows, :]) * (DK ** -0.5)
        v = v_ref[rows, :].astype(BF16)
        r_prev = r_ref[...]
        inner = _dot_nt(q.astype(BF16), k.astype(BF16)) * decay
        y = _dot(inner.astype(BF16), v) + _dot((q * xi).astype(BF16), r_prev.astype(BF16))
        y = y * lax.rsqrt(jnp.mean(y * y, axis=-1, keepdims=True) + EPS)
        g = g_ref[rows, :]
        o_ref[rows, :] = (g * jax.nn.sigmoid(g) * y).astype(BF16)
        r_ref[...] = gch * r_prev + _dot_tn((k * zeta).astype(BF16), v)


def _retention(zn, cosf, sinf, decay, xi, zeta, gch, *, b, s, ts):
    nst = s // ts
    col = lambda off: (lambda i, h, j: (i * nst + j, off + h))
    tab = lambda i, h, j: (j, 0)
    per_head = lambda i, h, j: (h, 0, 0)
    return pl.pallas_call(
        functools.partial(_ret_kernel, nchunk=ts // RET_CHUNK),
        grid=(b, RET_HEADS, nst),
        in_specs=[
            pl.BlockSpec((ts, DK), col(0)),
            pl.BlockSpec((ts, DK), col(RET_HEADS)),
            pl.BlockSpec((ts, DK), col(2 * RET_HEADS)),
            pl.BlockSpec((ts, DK), col(3 * RET_HEADS)),
            pl.BlockSpec((ts, DK), tab),
            pl.BlockSpec((ts, DK), tab),
            pl.BlockSpec((1, RET_CHUNK, RET_CHUNK), per_head),
            pl.BlockSpec((1, RET_CHUNK, DK), per_head),
            pl.BlockSpec((1, RET_CHUNK, DK), per_head),
            pl.BlockSpec((1, 8, DK), per_head),
        ],
        out_specs=pl.BlockSpec((ts, DK), lambda i, h, j: (i * nst + j, h)),
        out_shape=jax.ShapeDtypeStruct((b * s, RET_W), BF16),
        scratch_shapes=[pltpu.VMEM((DK, DK), F32)],
        compiler_params=_params(3),
        name="retention",
    )(zn, zn, zn, zn, cosf, sinf, decay, xi, zeta, gch)


def _mix_kernel(x_ref, ya_ref, yb_ref, ma_ref, mb_ref, wa_ref, wb_ref, wo_ref, o_ref):
    ua = _dot(ya_ref[...], wa_ref[...])
    ub = _dot(yb_ref[...], wb_ref[...])
    mix = jax.nn.sigmoid(ma_ref[...]) * ua + jax.nn.sigmoid(mb_ref[...]) * ub
    o_ref[...] = x_ref[...] + _dot(mix.astype(BF16), wo_ref[...])


def _mix(x2, ya, yb, zn, wa, wb, wo, *, tm):
    n = x2.shape[0]
    full = lambda i: (0, 0)
    ma_blk = 4 * RET_W // D_MODEL
    return pl.pallas_call(
        _mix_kernel,
        grid=(n // tm,),
        in_specs=[
            pl.BlockSpec((tm, D_MODEL), lambda i: (i, 0)),
            pl.BlockSpec((tm, NSA_Q_W), lambda i: (i, 0)),
            pl.BlockSpec((tm, RET_W), lambda i: (i, 0)),
            pl.BlockSpec((tm, D_MODEL), lambda i: (i, ma_blk)),
            pl.BlockSpec((tm, D_MODEL), lambda i: (i, ma_blk + 1)),
            pl.BlockSpec((NSA_Q_W, D_MODEL), full),
            pl.BlockSpec((RET_W, D_MODEL), full),
            pl.BlockSpec((D_MODEL, D_MODEL), full),
        ],
        out_specs=pl.BlockSpec((tm, D_MODEL), lambda i: (i, 0)),
        out_shape=jax.ShapeDtypeStruct((n, D_MODEL), F32),
        compiler_params=_params(1),
        name="mix",
    )(x2, ya, yb, zn, zn, wa, wb, wo)


def _ffn_kernel(x_ref, p_ref, gm_ref, w1_ref, w2_ref, gp_ref, wg_ref, wp_ref, o_ref, *, ff_chunk):
    x = x_ref[...]
    h = _rms_rows(x, gm_ref[...]).astype(BF16)
    for j in range(D_FF // ff_chunk):
        cols = slice(j * ff_chunk, (j + 1) * ff_chunk)
        u = jnp.square(jnp.maximum(_dot(h, w1_ref[:, cols]), 0.0)).astype(BF16)
        x = x + _dot(u, w2_ref[cols, :])
    gate = jax.nn.sigmoid(_dot(_rms_rows(x, gp_ref[...]).astype(BF16), wg_ref[...]))
    o_ref[...] = x + gate * _dot(p_ref[...].astype(BF16), wp_ref[...])


def _ffn(x2, p2, gm, w1, w2, gp, wg, wp, *, tm):
    n = x2.shape[0]
    full = lambda i: (0, 0)
    once = pl.Buffered(1)
    return pl.pallas_call(
        functools.partial(_ffn_kernel, ff_chunk=1024),
        grid=(n // tm,),
        in_specs=[
            pl.BlockSpec((tm, D_MODEL), lambda i: (i, 0)),
            pl.BlockSpec((tm, PLE_DIM), lambda i: (i, 0)),
            pl.BlockSpec((1, D_MODEL), full),
            pl.BlockSpec((D_MODEL, D_FF), full, pipeline_mode=once),
            pl.BlockSpec((D_FF, D_MODEL), full, pipeline_mode=once),
            pl.BlockSpec((1, D_MODEL), full),
            pl.BlockSpec((D_MODEL, D_MODEL), full, pipeline_mode=once),
            pl.BlockSpec((PLE_DIM, D_MODEL), full, pipeline_mode=once),
        ],
        out_specs=pl.BlockSpec((tm, D_MODEL), lambda i: (i, 0)),
        out_shape=jax.ShapeDtypeStruct((n, D_MODEL), F32),
        compiler_params=_params(1),
        name="ffn",
    )(x2, p2, gm, w1, w2, gp, wg, wp)


def _retention_tables(s):
    half = DK // 2
    pos = jnp.arange(s, dtype=F32)
    inv = ROPE_BASE ** (-jnp.arange(half, dtype=F32) / half)
    ang = pos[:, None] * inv[None, :]
    cos, sin = jnp.cos(ang), jnp.sin(ang)
    cosf = jnp.concatenate([cos, cos], axis=-1)
    sinf = jnp.concatenate([-sin, sin], axis=-1)
    C = RET_CHUNK
    gamma = 1.0 - 2.0 ** (-5.0 - jnp.arange(RET_HEADS, dtype=F32))
    lg = jnp.log(gamma)
    n = jnp.arange(C, dtype=F32)
    diff = n[:, None] - n[None, :]
    decay = jnp.where(diff >= 0, jnp.exp(lg[:, None, None] * jnp.maximum(diff, 0.0)), 0.0)
    xi = jnp.broadcast_to(jnp.exp(lg[:, None] * (n + 1.0))[:, :, None], (RET_HEADS, C, DK))
    zeta = jnp.broadcast_to(jnp.exp(lg[:, None] * (C - 1.0 - n))[:, :, None], (RET_HEADS, C, DK))
    gch = jnp.broadcast_to(jnp.exp(lg * C)[:, None, None], (RET_HEADS, 8, DK))
    return cosf, sinf, decay, xi, zeta, gch


def _layer(x2, p2, w, tabs, *, b, s):
    nq = s // QB
    nc = s // CMP_STRIDE
    zn, qT, cT, kT, vT, gT = _proj(x2, w["norm_mix"], w["wn"], w["wt"], w["qg"], w["kgc"], tm=256)

    def per_group(a):
        return a.reshape(NG, DH, b, s).transpose(2, 0, 3, 1)

    def stride_rows(a):
        rows = a.reshape(b, NG, nc, CMP_STRIDE * DH)
        last = jnp.tile(a[:, :, s - 1:s, :], (1, 1, 1, CMP_STRIDE))
        pad = jnp.zeros((b, NG, 7, CMP_STRIDE * DH), a.dtype)
        return jnp.concatenate([rows, last, pad], axis=2)

    onehot = (jnp.arange(s)[:, None] // SLC_BLOCK % NB == jnp.arange(NB)[None, :]).astype(BF16)
    ktail = jnp.concatenate([onehot, jnp.zeros((s, KA - DH - NB), BF16)], axis=1)

    def keys_aug(a):
        return jnp.concatenate([per_group(a), jnp.broadcast_to(ktail, (b, NG, s, KA - DH))], axis=-1)

    def value_tiles(a):
        a = a.reshape(NG, DH, b, s)
        tail = jnp.concatenate([jnp.ones((NG, 1, b, s), BF16), jnp.zeros((NG, VROWS - DH - 1, b, s), BF16)], axis=1)
        return jnp.concatenate([a, tail], axis=1).reshape(NG, VROWS, b, nq, QB).transpose(2, 0, 3, 1, 4)

    kc, vcT = _compress(stride_rows(per_group(cT[:NSA_KV_W])), stride_rows(per_group(cT[NSA_KV_W:])),
                        w["pos_k"], w["pos_v"], w["w1k"], w["w2k"], w["w1v"], w["w2vT"], w["kgr"], nc=nc)
    gates = gT[:NSA_GATE_W].reshape(3, NG, HG, b, s).transpose(3, 1, 0, 2, 4).reshape(b, NG, 3 * HG, s)
    gates = jnp.pad(gates, ((0, 0), (0, 0), (0, 16 - 3 * HG), (0, 0)))
    ya = _nsa(qT, kc, vcT, keys_aug(kT[:NSA_KV_W]), value_tiles(vT[:NSA_KV_W]),
              keys_aug(kT[NSA_KV_W:]), value_tiles(vT[NSA_KV_W:]), gates, b=b, s=s)
    yb = _retention(zn, *tabs, b=b, s=s, ts=min(s, 1024))
    x2 = _mix(x2, ya, yb, zn, w["wa"], w["wb"], w["wo"], tm=512)
    return _ffn(x2, p2, w["norm_mlp"], w["w1"], w["w2"], w["norm_ple"], w["wg"], w["wp"], tm=512)


def kernel(x, p, norm_mix, w_in, nsa_q_norm, nsa_k_norm, cmp_pos_k, cmp_pos_v, cmp_w1_k, cmp_w2_k,
           cmp_w1_v, cmp_w2_v, w_up_nsa, w_up_ret, w_out, norm_mlp, w_ff1, w_ff2, norm_ple, w_ple,
           w_ple_gate):
    b, s, _ = x.shape
    depth = w_in.shape[0]
    tabs = _retention_tables(s)
    x2 = x.reshape(b * s, D_MODEL)
    q_end = NSA_Q_W
    kv = NSA_KV_W
    g_end = NSA_Q_W + 6 * kv + NSA_GATE_W
    for i in range(depth):
        wi = w_in[i]
        wt = jnp.concatenate([
            wi[:, 0:q_end + 2 * kv],
            wi[:, q_end + 2 * kv:q_end + 3 * kv], wi[:, q_end + 4 * kv:q_end + 5 * kv],
            wi[:, q_end + 3 * kv:q_end + 4 * kv], wi[:, q_end + 5 * kv:q_end + 6 * kv],
            wi[:, q_end + 6 * kv:g_end],
            jnp.zeros((D_MODEL, GATE_ROWS - NSA_GATE_W), wi.dtype)], axis=1).T.astype(BF16)
        w = dict(
            norm_mix=norm_mix[i][None, :], wn=wi[:, g_end:].astype(BF16), wt=wt,
            qg=nsa_q_norm[i][:, None], kgc=nsa_k_norm[i][:, None], kgr=nsa_k_norm[i][None, :],
            pos_k=cmp_pos_k[i].reshape(1, CMP_BLOCK * DH), pos_v=cmp_pos_v[i].reshape(1, CMP_BLOCK * DH),
            w1k=cmp_w1_k[i].astype(BF16), w2k=cmp_w2_k[i].astype(BF16),
            w1v=cmp_w1_v[i].astype(BF16), w2vT=cmp_w2_v[i].T.astype(BF16),
            wa=w_up_nsa[i].astype(BF16), wb=w_up_ret[i].astype(BF16), wo=w_out[i].astype(BF16),
            norm_mlp=norm_mlp[i][None, :], w1=w_ff1[i].astype(BF16), w2=w_ff2[i].astype(BF16),
            norm_ple=norm_ple[i][None, :], wg=w_ple_gate[i].astype(BF16), wp=w_ple[i].astype(BF16),
        )
        x2 = _layer(x2, p[i].reshape(b * s, PLE_DIM), w, tabs, b=b, s=s)
    return x2.reshape(b, s, D_MODEL)
```

```python
import functools
import math

import jax
import jax.numpy as jnp
from jax import lax
from jax.experimental import pallas as pl
from jax.experimental.pallas import tpu as pltpu

D_MODEL = 1024
PLE_DIM = 256
NSA_HEADS = 8
DH = 64
NG = 2
HG = NSA_HEADS // NG
CMP_BLOCK = 32
CMP_STRIDE = 16
CMP_HIDDEN = 4 * DH
SLC_BLOCK = 64
N_SELECT = 16
WINDOW = 512
QB = 128
RET_HEADS = 4
DK = 128
RET_CHUNK = 128
ROPE_BASE = 10000.0
D_FF = 4 * D_MODEL
EPS = 1e-6
MASK_VALUE = -1e30
FORCE_SCORE = 1e4
LOG2E = math.log2(math.e)

NSA_Q_W = NSA_HEADS * DH
NSA_KV_W = NG * DH
NSA_GATE_W = 3 * NSA_HEADS
RET_W = RET_HEADS * DK
NAT_W = 4 * RET_W + 2 * D_MODEL
GATE_ROWS = 32
T_ROWS = NSA_Q_W + 6 * NSA_KV_W + GATE_ROWS

KT = 512
NB = KT // SLC_BLOCK
KA = 128
VROWS = DH + 16

VMEM_LIMIT = 56 * 1024 * 1024

F32 = jnp.float32
BF16 = jnp.bfloat16


def _dot(a, b):
    return jnp.dot(a, b, preferred_element_type=F32)


def _dot_nt(a, b):
    return lax.dot_general(a, b, (((1,), (1,)), ((), ())), preferred_element_type=F32)


def _dot_tn(a, b):
    return lax.dot_general(a, b, (((0,), (0,)), ((), ())), preferred_element_type=F32)


def _rms_rows(x, g):
    y = x * lax.rsqrt(jnp.mean(x * x, axis=-1, keepdims=True) + EPS)
    return y * g


def _params(n_axes):
    return pltpu.CompilerParams(dimension_semantics=("arbitrary",) * n_axes,
                                vmem_limit_bytes=VMEM_LIMIT)


def _proj_kernel(x_ref, g_ref, wn_ref, wt_ref, qg_ref, kg_ref,
                 zn_ref, qT_ref, cT_ref, kT_ref, vT_ref, gT_ref):
    h = _rms_rows(x_ref[...], g_ref[...]).astype(BF16)
    zn_ref[...] = _dot(h, wn_ref[...])
    zt = _dot_nt(wt_ref[...], h)
    qg = qg_ref[...]
    kg = kg_ref[...]

    def norm_cols(z, g):
        return z * lax.rsqrt(jnp.mean(z * z, axis=0, keepdims=True) + EPS) * g

    for hh in range(NSA_HEADS):
        rows = slice(hh * DH, (hh + 1) * DH)
        qT_ref[rows, :] = (norm_cols(zt[rows, :], qg) * (DH ** -0.5 * LOG2E)).astype(BF16)
    o = NSA_Q_W
    cT_ref[...] = zt[o:o + 2 * NSA_KV_W, :]
    o += 2 * NSA_KV_W
    for j in range(2 * NG):
        rows = slice(o + j * DH, o + (j + 1) * DH)
        kT_ref[j * DH:(j + 1) * DH, :] = norm_cols(zt[rows, :], kg).astype(BF16)
    o += 2 * NSA_KV_W
    vT_ref[...] = zt[o:o + 2 * NSA_KV_W, :].astype(BF16)
    o += 2 * NSA_KV_W
    gT_ref[...] = jax.nn.sigmoid(zt[o:o + GATE_ROWS, :])


def _proj(x2, g, wn, wt, qg, kg, *, tm):
    n = x2.shape[0]
    full = lambda i: (0, 0)
    colblk = lambda i: (0, i)
    return pl.pallas_call(
        _proj_kernel,
        grid=(n // tm,),
        in_specs=[
            pl.BlockSpec((tm, D_MODEL), lambda i: (i, 0)),
            pl.BlockSpec((1, D_MODEL), full),
            pl.BlockSpec((D_MODEL, NAT_W), full),
            pl.BlockSpec((T_ROWS, D_MODEL), full),
            pl.BlockSpec((DH, 1), full),
            pl.BlockSpec((DH, 1), full),
        ],
        out_specs=[
            pl.BlockSpec((tm, NAT_W), lambda i: (i, 0)),
            pl.BlockSpec((NSA_Q_W, tm), colblk),
            pl.BlockSpec((2 * NSA_KV_W, tm), colblk),
            pl.BlockSpec((2 * NSA_KV_W, tm), colblk),
            pl.BlockSpec((2 * NSA_KV_W, tm), colblk),
            pl.BlockSpec((GATE_ROWS, tm), colblk),
        ],
        out_shape=[
            jax.ShapeDtypeStruct((n, NAT_W), F32),
            jax.ShapeDtypeStruct((NSA_Q_W, n), BF16),
            jax.ShapeDtypeStruct((2 * NSA_KV_W, n), F32),
            jax.ShapeDtypeStruct((2 * NSA_KV_W, n), BF16),
            jax.ShapeDtypeStruct((2 * NSA_KV_W, n), BF16),
            jax.ShapeDtypeStruct((GATE_ROWS, n), F32),
        ],
        compiler_params=_params(1),
        name="proj",
    )(x2, g, wn, wt, qg, kg)


def _gelu_tanh(x):
    cdf = 0.5 * (1.0 + jnp.tanh(math.sqrt(2.0 / math.pi) * (x + 0.044715 * (x ** 3))))
    return x * cdf


def _cmp_kernel(zk_ref, zv_ref, pk_ref, pv_ref, w1k_ref, w2k_ref, w1v_ref, w2vT_ref, kg_ref,
                kc_ref, vcT_ref, *, nc):
    half = CMP_STRIDE * DH

    def hidden(z_ref, p_ref, w1_ref):
        a = (z_ref[0, 0, 0:nc, :] + p_ref[:, 0:half]).astype(BF16)
        b = (z_ref[0, 0, 1:nc + 1, :] + p_ref[:, half:2 * half]).astype(BF16)
        return _gelu_tanh(_dot(a, w1_ref[0:half, :]) + _dot(b, w1_ref[half:2 * half, :]))

    hk = hidden(zk_ref, pk_ref, w1k_ref).astype(BF16)
    kc_ref[0, 0] = _rms_rows(_dot(hk, w2k_ref[...]), kg_ref[...]).astype(BF16)
    hv = hidden(zv_ref, pv_ref, w1v_ref).astype(BF16)
    vcT_ref[0, 0] = _dot_nt(w2vT_ref[...], hv).astype(BF16)


def _compress(zk, zv, pk, pv, w1k, w2k, w1v, w2vT, kg, *, nc):
    b = zk.shape[0]
    rows = zk.shape[2]
    half = CMP_STRIDE * DH
    zspec = pl.BlockSpec((1, 1, rows, half), lambda i, j: (i, j, 0, 0))
    full = lambda i, j: (0, 0)
    return pl.pallas_call(
        functools.partial(_cmp_kernel, nc=nc),
        grid=(b, NG),
        in_specs=[
            zspec, zspec,
            pl.BlockSpec((1, 2 * half), full),
            pl.BlockSpec((1, 2 * half), full),
            pl.BlockSpec((2 * half, CMP_HIDDEN), full),
            pl.BlockSpec((CMP_HIDDEN, DH), full),
            pl.BlockSpec((2 * half, CMP_HIDDEN), full),
            pl.BlockSpec((DH, CMP_HIDDEN), full),
            pl.BlockSpec((1, DH), full),
        ],
        out_specs=[
            pl.BlockSpec((1, 1, nc, DH), lambda i, j: (i, j, 0, 0)),
            pl.BlockSpec((1, 1, DH, nc), lambda i, j: (i, j, 0, 0)),
        ],
        out_shape=[
            jax.ShapeDtypeStruct((b, NG, nc, DH), BF16),
            jax.ShapeDtypeStruct((b, NG, DH, nc), BF16),
        ],
        compiler_params=_params(2),
        name="compress",
    )(zk, zv, pk, pv, w1k, w2k, w1v, w2vT, kg)


def _nsa_kernel(q_ref, kc_ref, vcT_ref, ks_ref, vsT_ref, kw_ref, vwT_ref, gate_ref, out_ref,
                qz_ref, qw_ref, qs0_ref, qs1_ref, ps_ref, score_ref, sel_ref,
                ms_ref, as_ref, mw_ref, aw_ref, o_ref, s0_ref, s1_ref, mx0_ref, mx1_ref, *, nc, ns, n_sel):
    c = pl.program_id(2)
    ratio = SLC_BLOCK // CMP_STRIDE
    W = HG * QB
    nwt = WINDOW // QB
    q = jnp.concatenate([q_ref[h * DH:(h + 1) * DH, :] for h in range(HG)], axis=1)
    qz_ref[...] = jnp.concatenate([q, jnp.zeros((KA - DH, W), BF16)], axis=0)
    for r in (qw_ref, qs0_ref, qs1_ref):
        r[0:DH, :] = q
        r[DH + 16:KA, :] = jnp.zeros((KA - DH - 16, W), BF16)

    def set_bias(ref, rows):
        b = jnp.concatenate([rows] * HG, axis=1)
        ref[DH:DH + 16, :] = jnp.concatenate([b, jnp.zeros((16 - NB, W), F32)], axis=0).astype(BF16)

    def gate_rows(br):
        return jnp.concatenate(
            [jnp.broadcast_to(gate_ref[0, 0, br * HG + h:br * HG + h + 1, :], (DH, QB)) for h in range(HG)],
            axis=1)

    rr = lax.broadcasted_iota(jnp.int32, (QB, QB), 0)
    cc = lax.broadcasted_iota(jnp.int32, (QB, QB), 1)

    def reset(m_ref, a_ref):
        m_ref[...] = jnp.full((1, W), MASK_VALUE, F32)
        a_ref[...] = jnp.zeros((VROWS, W), F32)

    def attend(qa, k, vT, m_ref, a_ref, emask=None):
        s = _dot(k, qa)
        if emask is not None:
            top = jnp.concatenate(
                [jnp.where(emask, s[0:QB, h * QB:(h + 1) * QB], MASK_VALUE) for h in range(HG)], axis=1)
            s = top if s.shape[0] == QB else jnp.concatenate([top, s[QB:, :]], axis=0)
        m_old = m_ref[...]
        m_new = jnp.maximum(m_old, jnp.max(s, axis=0, keepdims=True))
        pt = jnp.exp2(s - m_new).astype(BF16)
        a_ref[...] = jnp.exp2(m_old - m_new) * a_ref[...] + _dot(vT, pt)
        m_ref[...] = m_new

    def normalised(a_ref):
        a = a_ref[...]
        return a[0:DH, :] * (1.0 / jnp.maximum(a[DH:DH + 1, :], 1e-30))

    diag = pl.ds(pl.multiple_of(c * QB, QB), QB)
    qz = qz_ref[...]
    reset(ms_ref, as_ref)
    attend(qz, ks_ref[0, 0, diag, :], vsT_ref[0, 0, c], ms_ref, as_ref, emask=rr <= cc)
    reset(mw_ref, aw_ref)
    attend(qz, kw_ref[0, 0, diag, :], vwT_ref[0, 0, c], mw_ref, aw_ref, emask=rr <= cc)

    kt0 = jnp.maximum(c - nwt, 0)
    b8 = lax.broadcasted_iota(jnp.int32, (NB, QB), 0)
    tile_of_col = ((b8 - 2 * kt0) & (NB - 1)) >> 1
    set_bias(qw_ref, jnp.where(tile_of_col < jnp.minimum(c, nwt), 0.0, MASK_VALUE))
    kwc = kw_ref[0, 0, pl.ds(pl.multiple_of(kt0 * QB, QB), nwt * QB), :]
    vwc = jnp.concatenate([vwT_ref[0, 0, kt0 + i] for i in range(nwt)], axis=1)
    attend(qw_ref[...], kwc, vwc, mw_ref, aw_ref, emask=(rr > cc) | (c < nwt))

    s = _dot(kc_ref[0, 0], q)
    row = lax.broadcasted_iota(jnp.int32, (nc, W), 0)
    col = lax.broadcasted_iota(jnp.int32, (nc, W), 1)
    t = c * QB + (col & (QB - 1))
    valid = row * CMP_STRIDE + (CMP_BLOCK - 1) <= t
    s = jnp.where(valid, s, MASK_VALUE)
    mx = jnp.max(s, axis=0, keepdims=True)
    e = jnp.where(valid, jnp.exp2(s - mx), 0.0)
    p = e * (1.0 / jnp.maximum(jnp.sum(e, axis=0, keepdims=True), 1e-30))
    o_ref[...] = gate_rows(0) * _dot(vcT_ref[0, 0], p.astype(BF16))

    psum = p[:, 0:QB]
    for h in range(1, HG):
        psum = psum + p[:, h * QB:(h + 1) * QB]
    ps_ref[0:8, :] = jnp.zeros((8, QB), F32)
    ps_ref[8:8 + nc, :] = psum
    imp = ps_ref[pl.ds(8, ns, stride=ratio), :]
    for r in range(1, ratio):
        imp = imp + ps_ref[pl.ds(8 + r, ns, stride=ratio), :]
    imp = imp + ps_ref[pl.ds(7, ns, stride=ratio), :]
    blk = lax.broadcasted_iota(jnp.int32, (ns, QB), 0)
    tq = c * QB + lax.broadcasted_iota(jnp.int32, (ns, QB), 1)
    cur = tq // SLC_BLOCK
    score = jnp.where(blk == cur, FORCE_SCORE, imp)
    score = jnp.where(blk == cur - 1, 2 * FORCE_SCORE, score)
    score = jnp.where(blk == 0, 3 * FORCE_SCORE, score)
    score = jnp.where(blk > cur, MASK_VALUE, score)
    score_ref[...] = score

    before_diag = blk < 2 * c
    work = score
    thr = None
    for _ in range(n_sel):
        thr = jnp.max(work, axis=0, keepdims=True)
        work = jnp.where(work == thr, -jnp.inf, work)
    fast = score >= thr
    cnt = jnp.sum(jnp.where(fast, 1.0, 0.0), axis=0, keepdims=True)
    n_tied = jnp.sum(jnp.where(cnt != float(n_sel), 1.0, 0.0))
    all_selected = 2 * c + 2 <= n_sel
    sel_ref[...] = jnp.where(before_diag & (fast | all_selected), 1.0, 0.0)

    @pl.when(jnp.logical_and(jnp.logical_not(all_selected), n_tied > 0.0))
    def _():
        def rank_body(jp, rank):
            other = jnp.broadcast_to(score_ref[pl.ds(jp, 1), :], (ns, QB))
            beats = (other > score) | ((other == score) & (jp < blk))
            return rank + jnp.where(beats, 1.0, 0.0)

        rank = lax.fori_loop(0, jnp.minimum(2 * c + 2, ns), rank_body, jnp.zeros((ns, QB), F32))
        sel_ref[...] = jnp.where(before_diag & (rank < float(n_sel)), 1.0, 0.0)

    last_tile = ns // NB - 1

    def scores(qs_ref, s_ref, mx_ref, t4):
        t4 = jnp.minimum(t4, last_tile)
        rows = sel_ref[pl.ds(pl.multiple_of(t4 * NB, NB), NB), :]
        set_bias(qs_ref, jnp.where(rows > 0.5, 0.0, MASK_VALUE))
        s = _dot(ks_ref[0, 0, pl.ds(pl.multiple_of(t4 * KT, KT), KT), :], qs_ref[...])
        s_ref[...] = s
        mx_ref[...] = jnp.max(s, axis=0, keepdims=True)

    def accumulate(s_ref, mx_ref, t4):
        m_old = ms_ref[...]
        m_new = jnp.maximum(m_old, mx_ref[...])
        pt = jnp.exp2(s_ref[...] - m_new).astype(BF16)
        vT = jnp.concatenate([vsT_ref[0, 0, (KT // QB) * t4 + i] for i in range(KT // QB)], axis=1)
        as_ref[...] = jnp.exp2(m_old - m_new) * as_ref[...] + _dot(vT, pt)
        ms_ref[...] = m_new

    scores(qs0_ref, s0_ref, mx0_ref, 0)

    def sel_body(j, carry):
        scores(qs1_ref, s1_ref, mx1_ref, 2 * j + 1)
        accumulate(s0_ref, mx0_ref, 2 * j)
        scores(qs0_ref, s0_ref, mx0_ref, 2 * j + 2)
        accumulate(s1_ref, mx1_ref, 2 * j + 1)
        return carry

    n_tiles = (c * QB + KT - 1) // KT
    lax.fori_loop(0, (n_tiles + 1) // 2, sel_body, 0)

    o = o_ref[...] + gate_rows(1) * normalised(as_ref) + gate_rows(2) * normalised(aw_ref)
    for hp in range(HG // 2):
        pair = jnp.concatenate([o[:, (2 * hp) * QB:(2 * hp + 1) * QB],
                                o[:, (2 * hp + 1) * QB:(2 * hp + 2) * QB]], axis=0)
        out_ref[:, hp * 2 * DH:(hp + 1) * 2 * DH] = pair.T.astype(BF16)


def _nsa(qT, kc, vcT, ks, vsT, kw, vwT, gates, *, b, s):
    assert s % (2 * KT) == 0 and s >= WINDOW + QB
    nq = s // QB
    nc = s // CMP_STRIDE
    ns = s // SLC_BLOCK
    n_sel = min(N_SELECT, ns)
    W = HG * QB
    bg = lambda i, j, c: (i, j, 0, 0)
    bg5 = lambda i, j, c: (i, j, 0, 0, 0)
    qa = pltpu.VMEM((KA, W), BF16)
    return pl.pallas_call(
        functools.partial(_nsa_kernel, nc=nc, ns=ns, n_sel=n_sel),
        grid=(b, NG, nq),
        in_specs=[
            pl.BlockSpec((HG * DH, QB), lambda i, j, c: (j, i * nq + c)),
            pl.BlockSpec((1, 1, nc, DH), bg),
            pl.BlockSpec((1, 1, DH, nc), bg),
            pl.BlockSpec((1, 1, s, KA), bg),
            pl.BlockSpec((1, 1, nq, VROWS, QB), bg5),
            pl.BlockSpec((1, 1, s, KA), bg),
            pl.BlockSpec((1, 1, nq, VROWS, QB), bg5),
            pl.BlockSpec((1, 1, 16, QB), lambda i, j, c: (i, j, 0, c)),
        ],
        out_specs=pl.BlockSpec((QB, HG * DH), lambda i, j, c: (i * nq + c, j)),
        out_shape=jax.ShapeDtypeStruct((b * s, NSA_Q_W), BF16),
        scratch_shapes=[
            qa, qa, qa, qa,
            pltpu.VMEM((8 + nc, QB), F32),
            pltpu.VMEM((ns, QB), F32),
            pltpu.VMEM((ns, QB), F32),
            pltpu.VMEM((1, W), F32),
            pltpu.VMEM((VROWS, W), F32),
            pltpu.VMEM((1, W), F32),
            pltpu.VMEM((VROWS, W), F32),
            pltpu.VMEM((DH, W), F32),
            pltpu.VMEM((KT, W), F32),
            pltpu.VMEM((KT, W), F32),
            pltpu.VMEM((1, W), F32),
            pltpu.VMEM((1, W), F32),
        ],
        compiler_params=_params(3),
        name="nsa",
    )(qT, kc, vcT, ks, vsT, kw, vwT, gates)


def _ret_kernel(q_ref, k_ref, v_ref, g_ref, cos_ref, sin_ref, decay_ref, xi_ref, zeta_ref, gch_ref,
                o_ref, r_ref, *, nchunk):
    @pl.when(pl.program_id(2) == 0)
    def _():
        r_ref[...] = jnp.zeros((DK, DK), F32)

    decay = decay_ref[0]
    xi = xi_ref[0]
    zeta = zeta_ref[0]
    gch = gch_ref[0, 0:1, :]
    for ci in range(nchunk):
        rows = slice(ci * RET_CHUNK, (ci + 1) * RET_CHUNK)
        cosv = cos_ref[rows, :]
        sinv = sin_ref[rows, :]

        def rope(x):
            return x * cosv + pltpu.roll(x, DK // 2, 1) * sinv

        q = rope(q_ref[rows, :])
        k = rope(k_ref[rows, :]) * (DK ** -0.5)
        v = v_ref[rows, :].astype(BF16)
        r_prev = r_ref[...]
        inner = _dot_nt(q.astype(BF16), k.astype(BF16)) * decay
        y = _dot(inner.astype(BF16), v) + _dot((q * xi).astype(BF16), r_prev.astype(BF16))
        y = y * lax.rsqrt(jnp.mean(y * y, axis=-1, keepdims=True) + EPS)
        g = g_ref[rows, :]
        o_ref[rows, :] = (g * jax.nn.sigmoid(g) * y).astype(BF16)
        r_ref[...] = gch * r_prev + _dot_tn((k * zeta).astype(BF16), v)


def _retention(zn, cosf, sinf, decay, xi, zeta, gch, *, b, s, ts):
    nst = s // ts
    col = lambda off: (lambda i, h, j: (i * nst + j, off + h))
    tab = lambda i, h, j: (j, 0)
    per_head = lambda i, h, j: (h, 0, 0)
    return pl.pallas_call(
        functools.partial(_ret_kernel, nchunk=ts // RET_CHUNK),
        grid=(b, RET_HEADS, nst),
        in_specs=[
            pl.BlockSpec((ts, DK), col(0)),
            pl.BlockSpec((ts, DK), col(RET_HEADS)),
            pl.BlockSpec((ts, DK), col(2 * RET_HEADS)),
            pl.BlockSpec((ts, DK), col(3 * RET_HEADS)),
            pl.BlockSpec((ts, DK), tab),
            pl.BlockSpec((ts, DK), tab),
            pl.BlockSpec((1, RET_CHUNK, RET_CHUNK), per_head),
            pl.BlockSpec((1, RET_CHUNK, DK), per_head),
            pl.BlockSpec((1, RET_CHUNK, DK), per_head),
            pl.BlockSpec((1, 8, DK), per_head),
        ],
        out_specs=pl.BlockSpec((ts, DK), lambda i, h, j: (i * nst + j, h)),
        out_shape=jax.ShapeDtypeStruct((b * s, RET_W), BF16),
        scratch_shapes=[pltpu.VMEM((DK, DK), F32)],
        compiler_params=_params(3),
        name="retention",
    )(zn, zn, zn, zn, cosf, sinf, decay, xi, zeta, gch)


def _mix_kernel(x_ref, ya_ref, yb_ref, ma_ref, mb_ref, wa_ref, wb_ref, wo_ref, o_ref):
    ua = _dot(ya_ref[...], wa_ref[...])
    ub = _dot(yb_ref[...], wb_ref[...])
    mix = jax.nn.sigmoid(ma_ref[...]) * ua + jax.nn.sigmoid(mb_ref[...]) * ub
    o_ref[...] = x_ref[...] + _dot(mix.astype(BF16), wo_ref[...])


def _mix(x2, ya, yb, zn, wa, wb, wo, *, tm):
    n = x2.shape[0]
    full = lambda i: (0, 0)
    ma_blk = 4 * RET_W // D_MODEL
    return pl.pallas_call(
        _mix_kernel,
        grid=(n // tm,),
        in_specs=[
            pl.BlockSpec((tm, D_MODEL), lambda i: (i, 0)),
            pl.BlockSpec((tm, NSA_Q_W), lambda i: (i, 0)),
            pl.BlockSpec((tm, RET_W), lambda i: (i, 0)),
            pl.BlockSpec((tm, D_MODEL), lambda i: (i, ma_blk)),
            pl.BlockSpec((tm, D_MODEL), lambda i: (i, ma_blk + 1)),
            pl.BlockSpec((NSA_Q_W, D_MODEL), full),
            pl.BlockSpec((RET_W, D_MODEL), full),
            pl.BlockSpec((D_MODEL, D_MODEL), full),
        ],
        out_specs=pl.BlockSpec((tm, D_MODEL), lambda i: (i, 0)),
        out_shape=jax.ShapeDtypeStruct((n, D_MODEL), F32),
        compiler_params=_params(1),
        name="mix",
    )(x2, ya, yb, zn, zn, wa, wb, wo)


def _ffn_kernel(x_ref, p_ref, gm_ref, w1_ref, w2_ref, gp_ref, wg_ref, wp_ref, o_ref, *, ff_chunk):
    x = x_ref[...]
    h = _rms_rows(x, gm_ref[...]).astype(BF16)
    for j in range(D_FF // ff_chunk):
        cols = slice(j * ff_chunk, (j + 1) * ff_chunk)
        u = jnp.square(jnp.maximum(_dot(h, w1_ref[:, cols]), 0.0)).astype(BF16)
        x = x + _dot(u, w2_ref[cols, :])
    gate = jax.nn.sigmoid(_dot(_rms_rows(x, gp_ref[...]).astype(BF16), wg_ref[...]))
    o_ref[...] = x + gate * _dot(p_ref[...].astype(BF16), wp_ref[...])


def _ffn(x2, p2, gm, w1, w2, gp, wg, wp, *, tm):
    n = x2.shape[0]
    full = lambda i: (0, 0)
    once = pl.Buffered(1)
    return pl.pallas_call(
        functools.partial(_ffn_kernel, ff_chunk=1024),
        grid=(n // tm,),
        in_specs=[
            pl.BlockSpec((tm, D_MODEL), lambda i: (i, 0)),
            pl.BlockSpec((tm, PLE_DIM), lambda i: (i, 0)),
            pl.BlockSpec((1, D_MODEL), full),
            pl.BlockSpec((D_MODEL, D_FF), full, pipeline_mode=once),
            pl.BlockSpec((D_FF, D_MODEL), full, pipeline_mode=once),
            pl.BlockSpec((1, D_MODEL), full),
            pl.BlockSpec((D_MODEL, D_MODEL), full, pipeline_mode=once),
            pl.BlockSpec((PLE_DIM, D_MODEL), full, pipeline_mode=once),
        ],
        out_specs=pl.BlockSpec((tm, D_MODEL), lambda i: (i, 0)),
        out_shape=jax.ShapeDtypeStruct((n, D_MODEL), F32),
        compiler_params=_params(1),
        name="ffn",
    )(x2, p2, gm, w1, w2, gp, wg, wp)


def _retention_tables(s):
    half = DK // 2
    pos = jnp.arange(s, dtype=F32)
    inv = ROPE_BASE ** (-jnp.arange(half, dtype=F32) / half)
    ang = pos[:, None] * inv[None, :]
    cos, sin = jnp.cos(ang), jnp.sin(ang)
    cosf = jnp.concatenate([cos, cos], axis=-1)
    sinf = jnp.concatenate([-sin, sin], axis=-1)
    C = RET_CHUNK
    gamma = 1.0 - 2.0 ** (-5.0 - jnp.arange(RET_HEADS, dtype=F32))
    lg = jnp.log(gamma)
    n = jnp.arange(C, dtype=F32)
    diff = n[:, None] - n[None, :]
    decay = jnp.where(diff >= 0, jnp.exp(lg[:, None, None] * jnp.maximum(diff, 0.0)), 0.0)
    xi = jnp.broadcast_to(jnp.exp(lg[:, None] * (n + 1.0))[:, :, None], (RET_HEADS, C, DK))
    zeta = jnp.broadcast_to(jnp.exp(lg[:, None] * (C - 1.0 - n))[:, :, None], (RET_HEADS, C, DK))
    gch = jnp.broadcast_to(jnp.exp(lg * C)[:, None, None], (RET_HEADS, 8, DK))
    return cosf, sinf, decay, xi, zeta, gch


def _layer(x2, p2, w, tabs, *, b, s):
    nq = s // QB
    nc = s // CMP_STRIDE
    zn, qT, cT, kT, vT, gT = _proj(x2, w["norm_mix"], w["wn"], w["wt"], w["qg"], w["kgc"], tm=256)

    def per_group(a):
        return a.reshape(NG, DH, b, s).transpose(2, 0, 3, 1)

    def stride_rows(a):
        rows = a.reshape(b, NG, nc, CMP_STRIDE * DH)
        last = jnp.tile(a[:, :, s - 1:s, :], (1, 1, 1, CMP_STRIDE))
        pad = jnp.zeros((b, NG, 7, CMP_STRIDE * DH), a.dtype)
        return jnp.concatenate([rows, last, pad], axis=2)

    onehot = (jnp.arange(s)[:, None] // SLC_BLOCK % NB == jnp.arange(NB)[None, :]).astype(BF16)
    ktail = jnp.concatenate([onehot, jnp.zeros((s, KA - DH - NB), BF16)], axis=1)

    def keys_aug(a):
        return jnp.concatenate([per_group(a), jnp.broadcast_to(ktail, (b, NG, s, KA - DH))], axis=-1)

    def value_tiles(a):
        a = a.reshape(NG, DH, b, s)
        tail = jnp.concatenate([jnp.ones((NG, 1, b, s), BF16), jnp.zeros((NG, VROWS - DH - 1, b, s), BF16)], axis=1)
        return jnp.concatenate([a, tail], axis=1).reshape(NG, VROWS, b, nq, QB).transpose(2, 0, 3, 1, 4)

    kc, vcT = _compress(stride_rows(per_group(cT[:NSA_KV_W])), stride_rows(per_group(cT[NSA_KV_W:])),
                        w["pos_k"], w["pos_v"], w["w1k"], w["w2k"], w["w1v"], w["w2vT"], w["kgr"], nc=nc)
    gates = gT[:NSA_GATE_W].reshape(3, NG, HG, b, s).transpose(3, 1, 0, 2, 4).reshape(b, NG, 3 * HG, s)
    gates = jnp.pad(gates, ((0, 0), (0, 0), (0, 16 - 3 * HG), (0, 0)))
    ya = _nsa(qT, kc, vcT, keys_aug(kT[:NSA_KV_W]), value_tiles(vT[:NSA_KV_W]),
              keys_aug(kT[NSA_KV_W:]), value_tiles(vT[NSA_KV_W:]), gates, b=b, s=s)
    yb = _retention(zn, *tabs, b=b, s=s, ts=min(s, 1024))
    x2 = _mix(x2, ya, yb, zn, w["wa"], w["wb"], w["wo"], tm=512)
    return _ffn(x2, p2, w["norm_mlp"], w["w1"], w["w2"], w["norm_ple"], w["wg"], w["wp"], tm=512)


def kernel(x, p, norm_mix, w_in, nsa_q_norm, nsa_k_norm, cmp_pos_k, cmp_pos_v, cmp_w1_k, cmp_w2_k,
           cmp_w1_v, cmp_w2_v, w_up_nsa, w_up_ret, w_out, norm_mlp, w_ff1, w_ff2, norm_ple, w_ple,
           w_ple_gate):
    b, s, _ = x.shape
    depth = w_in.shape[0]
    tabs = _retention_tables(s)
    x2 = x.reshape(b * s, D_MODEL)
    q_end = NSA_Q_W
    kv = NSA_KV_W
    g_end = NSA_Q_W + 6 * kv + NSA_GATE_W
    for i in range(depth):
        wi = w_in[i]
        wt = jnp.concatenate([
            wi[:, 0:q_end + 2 * kv],
            wi[:, q_end + 2 * kv:q_end + 3 * kv], wi[:, q_end + 4 * kv:q_end + 5 * kv],
            wi[:, q_end + 3 * kv:q_end + 4 * kv], wi[:, q_end + 5 * kv:q_end + 6 * kv],
            wi[:, q_end + 6 * kv:g_end],
            jnp.zeros((D_MODEL, GATE_ROWS - NSA_GATE_W), wi.dtype)], axis=1).T.astype(BF16)
        w = dict(
            norm_mix=norm_mix[i][None, :], wn=wi[:, g_end:].astype(BF16), wt=wt,
            qg=nsa_q_norm[i][:, None], kgc=nsa_k_norm[i][:, None], kgr=nsa_k_norm[i][None, :],
            pos_k=cmp_pos_k[i].reshape(1, CMP_BLOCK * DH), pos_v=cmp_pos_v[i].reshape(1, CMP_BLOCK * DH),
            w1k=cmp_w1_k[i].astype(BF16), w2k=cmp_w2_k[i].astype(BF16),
            w1v=cmp_w1_v[i].astype(BF16), w2vT=cmp_w2_v[i].T.astype(BF16),
            wa=w_up_nsa[i].astype(BF16), wb=w_up_ret[i].astype(BF16), wo=w_out[i].astype(BF16),
            norm_mlp=norm_mlp[i][None, :], w1=w_ff1[i].astype(BF16), w2=w_ff2[i].astype(BF16),
            norm_ple=norm_ple[i][None, :], wg=w_ple_gate[i].astype(BF16), wp=w_ple[i].astype(BF16),
        )
        x2 = _layer(x2, p[i].reshape(b * s, PLE_DIM), w, tabs, b=b, s=s)
    return x2.reshape(b, s, D_MODEL)
```

```python
import functools
import math

import jax
import jax.numpy as jnp
from jax import lax
from jax.experimental import pallas as pl
from jax.experimental.pallas import tpu as pltpu

D_MODEL = 1024
PLE_DIM = 256
NSA_HEADS = 8
DH = 64
NG = 2
HG = NSA_HEADS // NG
CMP_BLOCK = 32
CMP_STRIDE = 16
CMP_HIDDEN = 4 * DH
SLC_BLOCK = 64
N_SELECT = 16
WINDOW = 512
QB = 128
RET_HEADS = 4
DK = 128
RET_CHUNK = 128
ROPE_BASE = 10000.0
D_FF = 4 * D_MODEL
EPS = 1e-6
MASK_VALUE = -1e30
FORCE_SCORE = 1e4
LOG2E = math.log2(math.e)

NSA_Q_W = NSA_HEADS * DH
NSA_KV_W = NG * DH
NSA_GATE_W = 3 * NSA_HEADS
RET_W = RET_HEADS * DK
NAT_W = 4 * RET_W + 2 * D_MODEL
GATE_ROWS = 32
T_ROWS = NSA_Q_W + 6 * NSA_KV_W + GATE_ROWS

KT = 512
NB = KT // SLC_BLOCK
KA = 128
VROWS = DH + 16
BIAS_ROWS = 16

VMEM_LIMIT = 56 * 1024 * 1024

F32 = jnp.float32
BF16 = jnp.bfloat16


def _dot(a, b):
    return jnp.dot(a, b, preferred_element_type=F32)


def _dot_nt(a, b):
    return lax.dot_general(a, b, (((1,), (1,)), ((), ())), preferred_element_type=F32)


def _dot_tn(a, b):
    return lax.dot_general(a, b, (((0,), (0,)), ((), ())), preferred_element_type=F32)


def _rms_rows(x, g):
    y = x * lax.rsqrt(jnp.mean(x * x, axis=-1, keepdims=True) + EPS)
    return y * g


def _params(n_axes):
    return pltpu.CompilerParams(dimension_semantics=("arbitrary",) * n_axes,
                                vmem_limit_bytes=VMEM_LIMIT)


def _proj_kernel(x_ref, g_ref, wn_ref, wt_ref, qg_ref, kg_ref,
                 zn_ref, qT_ref, cT_ref, kT_ref, vT_ref, gT_ref):
    h = _rms_rows(x_ref[...], g_ref[...]).astype(BF16)
    zn_ref[...] = _dot(h, wn_ref[...])
    zt = _dot_nt(wt_ref[...], h)
    qg = qg_ref[...]
    kg = kg_ref[...]

    def norm_cols(z, g):
        return z * lax.rsqrt(jnp.mean(z * z, axis=0, keepdims=True) + EPS) * g

    for hh in range(NSA_HEADS):
        rows = slice(hh * DH, (hh + 1) * DH)
        qT_ref[rows, :] = (norm_cols(zt[rows, :], qg) * (DH ** -0.5 * LOG2E)).astype(BF16)
    o = NSA_Q_W
    cT_ref[...] = zt[o:o + 2 * NSA_KV_W, :]
    o += 2 * NSA_KV_W
    for j in range(2 * NG):
        rows = slice(o + j * DH, o + (j + 1) * DH)
        kT_ref[j * DH:(j + 1) * DH, :] = norm_cols(zt[rows, :], kg).astype(BF16)
    o += 2 * NSA_KV_W
    vT_ref[...] = zt[o:o + 2 * NSA_KV_W, :].astype(BF16)
    o += 2 * NSA_KV_W
    gT_ref[...] = jax.nn.sigmoid(zt[o:o + GATE_ROWS, :])


def _proj(x2, g, wn, wt, qg, kg, *, tm):
    n = x2.shape[0]
    full = lambda i: (0, 0)
    colblk = lambda i: (0, i)
    return pl.pallas_call(
        _proj_kernel,
        grid=(n // tm,),
        in_specs=[
            pl.BlockSpec((tm, D_MODEL), lambda i: (i, 0)),
            pl.BlockSpec((1, D_MODEL), full),
            pl.BlockSpec((D_MODEL, NAT_W), full),
            pl.BlockSpec((T_ROWS, D_MODEL), full),
            pl.BlockSpec((DH, 1), full),
            pl.BlockSpec((DH, 1), full),
        ],
        out_specs=[
            pl.BlockSpec((tm, NAT_W), lambda i: (i, 0)),
            pl.BlockSpec((NSA_Q_W, tm), colblk),
            pl.BlockSpec((2 * NSA_KV_W, tm), colblk),
            pl.BlockSpec((2 * NSA_KV_W, tm), colblk),
            pl.BlockSpec((2 * NSA_KV_W, tm), colblk),
            pl.BlockSpec((GATE_ROWS, tm), colblk),
        ],
        out_shape=[
            jax.ShapeDtypeStruct((n, NAT_W), F32),
            jax.ShapeDtypeStruct((NSA_Q_W, n), BF16),
            jax.ShapeDtypeStruct((2 * NSA_KV_W, n), F32),
            jax.ShapeDtypeStruct((2 * NSA_KV_W, n), BF16),
            jax.ShapeDtypeStruct((2 * NSA_KV_W, n), BF16),
            jax.ShapeDtypeStruct((GATE_ROWS, n), F32),
        ],
        compiler_params=_params(1),
        name="proj",
    )(x2, g, wn, wt, qg, kg)


def _gelu_tanh(x):
    cdf = 0.5 * (1.0 + jnp.tanh(math.sqrt(2.0 / math.pi) * (x + 0.044715 * (x ** 3))))
    return x * cdf


def _cmp_kernel(zk_ref, zv_ref, pk_ref, pv_ref, w1k_ref, w2k_ref, w1v_ref, w2vT_ref, kg_ref,
                kc_ref, vcT_ref, *, nc):
    half = CMP_STRIDE * DH

    def hidden(z_ref, p_ref, w1_ref):
        a = (z_ref[0, 0, 0:nc, :] + p_ref[:, 0:half]).astype(BF16)
        b = (z_ref[0, 0, 1:nc + 1, :] + p_ref[:, half:2 * half]).astype(BF16)
        return _gelu_tanh(_dot(a, w1_ref[0:half, :]) + _dot(b, w1_ref[half:2 * half, :]))

    hk = hidden(zk_ref, pk_ref, w1k_ref).astype(BF16)
    k = _dot(hk, w2k_ref[...])
    k = k * lax.rsqrt(jnp.sum(k * k, axis=-1, keepdims=True) * (1.0 / DH) + EPS) * kg_ref[...]
    row = lax.broadcasted_iota(jnp.int32, (nc, KA), 0)
    lane = lax.broadcasted_iota(jnp.int32, (nc, KA), 1)
    kc_ref[0, 0] = (k + jnp.where(lane - DH == (row >> 3), 1.0, 0.0)).astype(BF16)
    hv = hidden(zv_ref, pv_ref, w1v_ref).astype(BF16)
    vcT_ref[0, 0] = _dot_nt(w2vT_ref[...], hv).astype(BF16)


def _compress(zk, zv, pk, pv, w1k, w2k, w1v, w2vT, kg, *, nc):
    b = zk.shape[0]
    rows = zk.shape[2]
    half = CMP_STRIDE * DH
    zspec = pl.BlockSpec((1, 1, rows, half), lambda i, j: (i, j, 0, 0))
    full = lambda i, j: (0, 0)
    return pl.pallas_call(
        functools.partial(_cmp_kernel, nc=nc),
        grid=(b, NG),
        in_specs=[
            zspec, zspec,
            pl.BlockSpec((1, 2 * half), full),
            pl.BlockSpec((1, 2 * half), full),
            pl.BlockSpec((2 * half, CMP_HIDDEN), full),
            pl.BlockSpec((CMP_HIDDEN, KA), full),
            pl.BlockSpec((2 * half, CMP_HIDDEN), full),
            pl.BlockSpec((DH, CMP_HIDDEN), full),
            pl.BlockSpec((1, KA), full),
        ],
        out_specs=[
            pl.BlockSpec((1, 1, nc, KA), lambda i, j: (i, j, 0, 0)),
            pl.BlockSpec((1, 1, DH, nc), lambda i, j: (i, j, 0, 0)),
        ],
        out_shape=[
            jax.ShapeDtypeStruct((b, NG, nc, KA), BF16),
            jax.ShapeDtypeStruct((b, NG, DH, nc), BF16),
        ],
        compiler_params=_params(2),
        name="compress",
    )(zk, zv, pk, pv, w1k, w2k, w1v, w2vT, kg)


def _nsa_kernel(q_ref, qn_ref, kc_ref, vcT_ref, ks_ref, vsT_ref, kw_ref, vwT_ref, gate_ref, out_ref,
                qz_ref, qw_ref, qs0_ref, qs1_ref, qc_ref, sc_ref, ps_ref, score_ref, sel_ref, seln_ref,
                ms_ref, as_ref, sw_ref, aw_ref, sd_ref, oc_ref, ocn_ref, s0_ref, s1_ref, mx0_ref, mx1_ref,
                *, nc, ns, n_sel):
    c = pl.program_id(2)
    ratio = SLC_BLOCK // CMP_STRIDE
    W = HG * QB
    nwt = WINDOW // QB

    def heads_on_lanes(ref):
        return jnp.concatenate([ref[h * DH:(h + 1) * DH, :] for h in range(HG)], axis=1)

    def set_bias(ref, rows):
        b = jnp.concatenate([rows] * HG, axis=1)
        ref[DH:DH + BIAS_ROWS, :] = jnp.concatenate(
            [b, jnp.zeros((BIAS_ROWS - NB, W), F32)], axis=0).astype(BF16)

    def compressed_scores(ct, qt):
        grp = lax.broadcasted_iota(jnp.int32, (KA - DH, W), 0)
        qc_ref[0:DH, :] = qt
        qc_ref[DH:KA, :] = jnp.where(grp <= ct, 0.0, MASK_VALUE).astype(BF16)
        sc_ref[...] = _dot(kc_ref[0, 0], qc_ref[...])

    def select_blocks(ct):
        r0 = pl.multiple_of(jnp.maximum(ct - 1, 0) * 8, 8)
        row = r0 + lax.broadcasted_iota(jnp.int32, (16, W), 0)
        tcol = ct * QB + (lax.broadcasted_iota(jnp.int32, (16, W), 1) & (QB - 1))
        edge = sc_ref[pl.ds(r0, 16), :]
        sc_ref[pl.ds(r0, 16), :] = jnp.where(row * CMP_STRIDE + (CMP_BLOCK - 1) <= tcol, edge, MASK_VALUE)
        s = sc_ref[...]
        e = jnp.exp2(s - jnp.max(s, axis=0, keepdims=True))
        t1 = ct * QB + (lax.broadcasted_iota(jnp.int32, (1, W), 1) & (QB - 1))
        inv = jnp.where(t1 >= CMP_BLOCK - 1, 1.0 / jnp.maximum(jnp.sum(e, axis=0, keepdims=True), 1e-30), 0.0)
        ocn_ref[...] = _dot(vcT_ref[0, 0], e.astype(BF16)) * inv

        psum = e[:, 0:QB] * inv[:, 0:QB]
        for h in range(1, HG):
            psum = psum + e[:, h * QB:(h + 1) * QB] * inv[:, h * QB:(h + 1) * QB]
        ps_ref[0:8, :] = jnp.zeros((8, QB), F32)
        ps_ref[8:8 + nc, :] = psum
        imp = ps_ref[pl.ds(8, ns, stride=ratio), :]
        for r in range(1, ratio):
            imp = imp + ps_ref[pl.ds(8 + r, ns, stride=ratio), :]
        imp = imp + ps_ref[pl.ds(7, ns, stride=ratio), :]
        blk = lax.broadcasted_iota(jnp.int32, (ns, QB), 0)
        tq = ct * QB + lax.broadcasted_iota(jnp.int32, (ns, QB), 1)
        cur = tq // SLC_BLOCK
        score = jnp.where(blk == cur, FORCE_SCORE, imp)
        score = jnp.where(blk == cur - 1, 2 * FORCE_SCORE, score)
        score = jnp.where(blk == 0, 3 * FORCE_SCORE, score)
        score = jnp.where(blk > cur, MASK_VALUE, score)
        score_ref[...] = score

        before_diag = blk < 2 * ct
        work = score
        thr = None
        for _ in range(n_sel):
            thr = jnp.max(work, axis=0, keepdims=True)
            work = jnp.where(work == thr, -jnp.inf, work)
        fast = score >= thr
        cnt = jnp.sum(jnp.where(fast, 1.0, 0.0), axis=0, keepdims=True)
        n_tied = jnp.sum(jnp.where(cnt != float(n_sel), 1.0, 0.0))
        all_selected = 2 * ct + 2 <= n_sel
        seln_ref[...] = jnp.where(before_diag & (fast | all_selected), 1.0, 0.0)

        @pl.when(jnp.logical_and(jnp.logical_not(all_selected), n_tied > 0.0))
        def _():
            def rank_body(jp, rank):
                other = jnp.broadcast_to(score_ref[pl.ds(jp, 1), :], (ns, QB))
                beats = (other > score) | ((other == score) & (jp < blk))
                return rank + jnp.where(beats, 1.0, 0.0)

            rank = lax.fori_loop(0, jnp.minimum(2 * ct + 2, ns), rank_body, jnp.zeros((ns, QB), F32))
            seln_ref[...] = jnp.where(before_diag & (rank < float(n_sel)), 1.0, 0.0)

    @pl.when(c == 0)
    def _():
        compressed_scores(c, heads_on_lanes(q_ref))
        select_blocks(c)

    sel_ref[...] = seln_ref[...]
    oc_ref[...] = ocn_ref[...]
    q = heads_on_lanes(q_ref)
    qz_ref[...] = jnp.concatenate([q, jnp.zeros((KA - DH, W), BF16)], axis=0)
    for r in (qw_ref, qs0_ref, qs1_ref):
        r[0:DH, :] = q
        r[DH + BIAS_ROWS:KA, :] = jnp.zeros((KA - DH - BIAS_ROWS, W), BF16)

    def gate_rows(br):
        return jnp.concatenate(
            [jnp.broadcast_to(gate_ref[0, 0, br * HG + h:br * HG + h + 1, :], (DH, QB)) for h in range(HG)],
            axis=1)

    rr = lax.broadcasted_iota(jnp.int32, (QB, QB), 0)
    cc = lax.broadcasted_iota(jnp.int32, (QB, QB), 1)

    def mask_first_tile(s, emask):
        top = jnp.concatenate(
            [jnp.where(emask, s[0:QB, h * QB:(h + 1) * QB], MASK_VALUE) for h in range(HG)], axis=1)
        return top if s.shape[0] == QB else jnp.concatenate([top, s[QB:, :]], axis=0)

    def normalised(a_ref):
        a = a_ref[...]
        return a[0:DH, :] * (1.0 / jnp.maximum(a[DH:DH + 1, :], 1e-30))

    last_tile = ns // NB - 1

    def scores(qs_ref, s_ref, mx_ref, t4):
        t4 = jnp.minimum(t4, last_tile)
        rows = sel_ref[pl.ds(pl.multiple_of(t4 * NB, NB), NB), :]
        set_bias(qs_ref, jnp.where(rows > 0.5, 0.0, MASK_VALUE))
        s = _dot(ks_ref[0, 0, pl.ds(pl.multiple_of(t4 * KT, KT), KT), :], qs_ref[...])
        s_ref[...] = s
        mx_ref[...] = jnp.max(s, axis=0, keepdims=True)

    def accumulate(s_ref, mx_ref, t4):
        m_old = ms_ref[...]
        m_new = jnp.maximum(m_old, mx_ref[...])
        pt = jnp.exp2(s_ref[...] - m_new).astype(BF16)
        vT = jnp.concatenate([vsT_ref[0, 0, (KT // QB) * t4 + i] for i in range(KT // QB)], axis=1)
        as_ref[...] = jnp.exp2(m_old - m_new) * as_ref[...] + _dot(vT, pt)
        ms_ref[...] = m_new

    nxt = jnp.minimum(c + 1, pl.num_programs(2) - 1)
    compressed_scores(nxt, heads_on_lanes(qn_ref))
    scores(qs0_ref, s0_ref, mx0_ref, 0)

    diag = pl.ds(pl.multiple_of(c * QB, QB), QB)
    qz = qz_ref[...]
    kt0 = jnp.maximum(c - nwt, 0)
    b8 = lax.broadcasted_iota(jnp.int32, (NB, QB), 0)
    tile_of_col = ((b8 - 2 * kt0) & (NB - 1)) >> 1
    set_bias(qw_ref, jnp.where(tile_of_col < jnp.minimum(c, nwt), 0.0, MASK_VALUE))
    kwc = kw_ref[0, 0, pl.ds(pl.multiple_of(kt0 * QB, QB), nwt * QB), :]
    sw_ref[0:nwt * QB, :] = mask_first_tile(_dot(kwc, qw_ref[...]), (rr > cc) | (c < nwt))
    sw_ref[nwt * QB:, :] = mask_first_tile(_dot(kw_ref[0, 0, diag, :], qz), rr <= cc)
    sd_ref[...] = mask_first_tile(_dot(ks_ref[0, 0, diag, :], qz), rr <= cc)

    sw = sw_ref[...]
    pw = jnp.exp2(sw - jnp.max(sw, axis=0, keepdims=True)).astype(BF16)
    vw = jnp.concatenate([vwT_ref[0, 0, kt0 + i] for i in range(nwt)] + [vwT_ref[0, 0, c]], axis=1)
    aw_ref[...] = _dot(vw, pw)
    sd = sd_ref[...]
    md = jnp.max(sd, axis=0, keepdims=True)
    ms_ref[...] = md
    as_ref[...] = _dot(vsT_ref[0, 0, c], jnp.exp2(sd - md).astype(BF16))
    select_blocks(nxt)

    def sel_body(j, carry):
        scores(qs1_ref, s1_ref, mx1_ref, 2 * j + 1)
        accumulate(s0_ref, mx0_ref, 2 * j)
        scores(qs0_ref, s0_ref, mx0_ref, 2 * j + 2)
        accumulate(s1_ref, mx1_ref, 2 * j + 1)
        return carry

    n_tiles = (c * QB + KT - 1) // KT
    lax.fori_loop(0, (n_tiles + 1) // 2, sel_body, 0)

    o = gate_rows(0) * oc_ref[...] + gate_rows(1) * normalised(as_ref) + gate_rows(2) * normalised(aw_ref)
    for hp in range(HG // 2):
        pair = jnp.concatenate([o[:, (2 * hp) * QB:(2 * hp + 1) * QB],
                                o[:, (2 * hp + 1) * QB:(2 * hp + 2) * QB]], axis=0)
        out_ref[:, hp * 2 * DH:(hp + 1) * 2 * DH] = pair.T.astype(BF16)


def _nsa(qT, kc, vcT, ks, vsT, kw, vwT, gates, *, b, s):
    assert s % (2 * KT) == 0 and s >= WINDOW + QB and s // CMP_STRIDE // 8 <= KA - DH
    nq = s // QB
    nc = s // CMP_STRIDE
    ns = s // SLC_BLOCK
    n_sel = min(N_SELECT, ns)
    W = HG * QB
    bg = lambda i, j, c: (i, j, 0, 0)
    bg5 = lambda i, j, c: (i, j, 0, 0, 0)
    qa = pltpu.VMEM((KA, W), BF16)
    return pl.pallas_call(
        functools.partial(_nsa_kernel, nc=nc, ns=ns, n_sel=n_sel),
        grid=(b, NG, nq),
        in_specs=[
            pl.BlockSpec((HG * DH, QB), lambda i, j, c: (j, i * nq + c)),
            pl.BlockSpec((HG * DH, QB), lambda i, j, c: (j, i * nq + jnp.minimum(c + 1, nq - 1))),
            pl.BlockSpec((1, 1, nc, KA), bg),
            pl.BlockSpec((1, 1, DH, nc), bg),
            pl.BlockSpec((1, 1, s, KA), bg),
            pl.BlockSpec((1, 1, nq, VROWS, QB), bg5),
            pl.BlockSpec((1, 1, s, KA), bg),
            pl.BlockSpec((1, 1, nq, VROWS, QB), bg5),
            pl.BlockSpec((1, 1, 16, QB), lambda i, j, c: (i, j, 0, c)),
        ],
        out_specs=pl.BlockSpec((QB, HG * DH), lambda i, j, c: (i * nq + c, j)),
        out_shape=jax.ShapeDtypeStruct((b * s, NSA_Q_W), BF16),
        scratch_shapes=[
            qa, qa, qa, qa, qa,
            pltpu.VMEM((nc, W), F32),
            pltpu.VMEM((8 + nc, QB), F32),
            pltpu.VMEM((ns, QB), F32),
            pltpu.VMEM((ns, QB), F32),
            pltpu.VMEM((ns, QB), F32),
            pltpu.VMEM((1, W), F32),
            pltpu.VMEM((VROWS, W), F32),
            pltpu.VMEM((WINDOW + QB, W), F32),
            pltpu.VMEM((VROWS, W), F32),
            pltpu.VMEM((QB, W), F32),
            pltpu.VMEM((DH, W), F32),
            pltpu.VMEM((DH, W), F32),
            pltpu.VMEM((KT, W), F32),
            pltpu.VMEM((KT, W), F32),
            pltpu.VMEM((1, W), F32),
            pltpu.VMEM((1, W), F32),
        ],
        compiler_params=_params(3),
        name="nsa",
    )(qT, qT, kc, vcT, ks, vsT, kw, vwT, gates)


def _ret_kernel(q_ref, k_ref, v_ref, g_ref, cos_ref, sin_ref, decay_ref, xi_ref, zeta_ref, gch_ref,
                o_ref, r_ref, *, nchunk):
    @pl.when(pl.program_id(2) == 0)
    def _():
        r_ref[...] = jnp.zeros((DK, DK), F32)

    decay = decay_ref[0]
    xi = xi_ref[0]
    zeta = zeta_ref[0]
    gch = gch_ref[0, 0:1, :]
    for ci in range(nchunk):
        rows = slice(ci * RET_CHUNK, (ci + 1) * RET_CHUNK)
        cosv = cos_ref[rows, :]
        sinv = sin_ref[rows, :]

        def rope(x):
            return x * cosv + pltpu.roll(x, DK // 2, 1) * sinv

        q = rope(q_ref[rows, :])
        k = rope(k_ref[rows, :]) * (DK ** -0.5)
        v = v_ref[rows, :].astype(BF16)
        r_prev = r_ref[...]
        inner = _dot_nt(q.astype(BF16), k.astype(BF16)) * decay
        y = _dot(inner.astype(BF16), v) + _dot((q * xi).astype(BF16), r_prev.astype(BF16))
        y = y * lax.rsqrt(jnp.mean(y * y, axis=-1, keepdims=True) + EPS)
        g = g_ref[rows, :]
        o_ref[rows, :] = (g * jax.nn.sigmoid(g) * y).astype(BF16)
        r_ref[...] = gch * r_prev + _dot_tn((k * zeta).astype(BF16), v)


def _retention(zn, cosf, sinf, decay, xi, zeta, gch, *, b, s, ts):
    nst = s // ts
    col = lambda off: (lambda i, h, j: (i * nst + j, off + h))
    tab = lambda i, h, j: (j, 0)
    per_head = lambda i, h, j: (h, 0, 0)
    return pl.pallas_call(
        functools.partial(_ret_kernel, nchunk=ts // RET_CHUNK),
        grid=(b, RET_HEADS, nst),
        in_specs=[
            pl.BlockSpec((ts, DK), col(0)),
            pl.BlockSpec((ts, DK), col(RET_HEADS)),
            pl.BlockSpec((ts, DK), col(2 * RET_HEADS)),
            pl.BlockSpec((ts, DK), col(3 * RET_HEADS)),
            pl.BlockSpec((ts, DK), tab),
            pl.BlockSpec((ts, DK), tab),
            pl.BlockSpec((1, RET_CHUNK, RET_CHUNK), per_head),
            pl.BlockSpec((1, RET_CHUNK, DK), per_head),
            pl.BlockSpec((1, RET_CHUNK, DK), per_head),
            pl.BlockSpec((1, 8, DK), per_head),
        ],
        out_specs=pl.BlockSpec((ts, DK), lambda i, h, j: (i * nst + j, h)),
        out_shape=jax.ShapeDtypeStruct((b * s, RET_W), BF16),
        scratch_shapes=[pltpu.VMEM((DK, DK), F32)],
        compiler_params=_params(3),
        name="retention",
    )(zn, zn, zn, zn, cosf, sinf, decay, xi, zeta, gch)


def _mix_kernel(x_ref, ya_ref, yb_ref, ma_ref, mb_ref, wa_ref, wb_ref, wo_ref, o_ref):
    ua = _dot(ya_ref[...], wa_ref[...])
    ub = _dot(yb_ref[...], wb_ref[...])
    mix = jax.nn.sigmoid(ma_ref[...]) * ua + jax.nn.sigmoid(mb_ref[...]) * ub
    o_ref[...] = x_ref[...] + _dot(mix.astype(BF16), wo_ref[...])


def _mix(x2, ya, yb, zn, wa, wb, wo, *, tm):
    n = x2.shape[0]
    full = lambda i: (0, 0)
    ma_blk = 4 * RET_W // D_MODEL
    return pl.pallas_call(
        _mix_kernel,
        grid=(n // tm,),
        in_specs=[
            pl.BlockSpec((tm, D_MODEL), lambda i: (i, 0)),
            pl.BlockSpec((tm, NSA_Q_W), lambda i: (i, 0)),
            pl.BlockSpec((tm, RET_W), lambda i: (i, 0)),
            pl.BlockSpec((tm, D_MODEL), lambda i: (i, ma_blk)),
            pl.BlockSpec((tm, D_MODEL), lambda i: (i, ma_blk + 1)),
            pl.BlockSpec((NSA_Q_W, D_MODEL), full),
            pl.BlockSpec((RET_W, D_MODEL), full),
            pl.BlockSpec((D_MODEL, D_MODEL), full),
        ],
        out_specs=pl.BlockSpec((tm, D_MODEL), lambda i: (i, 0)),
        out_shape=jax.ShapeDtypeStruct((n, D_MODEL), F32),
        compiler_params=_params(1),
        name="mix",
    )(x2, ya, yb, zn, zn, wa, wb, wo)


def _ffn_kernel(x_ref, p_ref, gm_ref, w1_ref, w2_ref, gp_ref, wg_ref, wp_ref, o_ref, *, ff_chunk):
    x = x_ref[...]
    h = _rms_rows(x, gm_ref[...]).astype(BF16)
    for j in range(D_FF // ff_chunk):
        cols = slice(j * ff_chunk, (j + 1) * ff_chunk)
        u = jnp.square(jnp.maximum(_dot(h, w1_ref[:, cols]), 0.0)).astype(BF16)
        x = x + _dot(u, w2_ref[cols, :])
    gate = jax.nn.sigmoid(_dot(_rms_rows(x, gp_ref[...]).astype(BF16), wg_ref[...]))
    o_ref[...] = x + gate * _dot(p_ref[...].astype(BF16), wp_ref[...])


def _ffn(x2, p2, gm, w1, w2, gp, wg, wp, *, tm):
    n = x2.shape[0]
    full = lambda i: (0, 0)
    once = pl.Buffered(1)
    return pl.pallas_call(
        functools.partial(_ffn_kernel, ff_chunk=1024),
        grid=(n // tm,),
        in_specs=[
            pl.BlockSpec((tm, D_MODEL), lambda i: (i, 0)),
            pl.BlockSpec((tm, PLE_DIM), lambda i: (i, 0)),
            pl.BlockSpec((1, D_MODEL), full),
            pl.BlockSpec((D_MODEL, D_FF), full, pipeline_mode=once),
            pl.BlockSpec((D_FF, D_MODEL), full, pipeline_mode=once),
            pl.BlockSpec((1, D_MODEL), full),
            pl.BlockSpec((D_MODEL, D_MODEL), full, pipeline_mode=once),
            pl.BlockSpec((PLE_DIM, D_MODEL), full, pipeline_mode=once),
        ],
        out_specs=pl.BlockSpec((tm, D_MODEL), lambda i: (i, 0)),
        out_shape=jax.ShapeDtypeStruct((n, D_MODEL), F32),
        compiler_params=_params(1),
        name="ffn",
    )(x2, p2, gm, w1, w2, gp, wg, wp)


def _retention_tables(s):
    half = DK // 2
    pos = jnp.arange(s, dtype=F32)
    inv = ROPE_BASE ** (-jnp.arange(half, dtype=F32) / half)
    ang = pos[:, None] * inv[None, :]
    cos, sin = jnp.cos(ang), jnp.sin(ang)
    cosf = jnp.concatenate([cos, cos], axis=-1)
    sinf = jnp.concatenate([-sin, sin], axis=-1)
    C = RET_CHUNK
    gamma = 1.0 - 2.0 ** (-5.0 - jnp.arange(RET_HEADS, dtype=F32))
    lg = jnp.log(gamma)
    n = jnp.arange(C, dtype=F32)
    diff = n[:, None] - n[None, :]
    decay = jnp.where(diff >= 0, jnp.exp(lg[:, None, None] * jnp.maximum(diff, 0.0)), 0.0)
    xi = jnp.broadcast_to(jnp.exp(lg[:, None] * (n + 1.0))[:, :, None], (RET_HEADS, C, DK))
    zeta = jnp.broadcast_to(jnp.exp(lg[:, None] * (C - 1.0 - n))[:, :, None], (RET_HEADS, C, DK))
    gch = jnp.broadcast_to(jnp.exp(lg * C)[:, None, None], (RET_HEADS, 8, DK))
    return cosf, sinf, decay, xi, zeta, gch


def _layer(x2, p2, w, tabs, *, b, s):
    nq = s // QB
    nc = s // CMP_STRIDE
    zn, qT, cT, kT, vT, gT = _proj(x2, w["norm_mix"], w["wn"], w["wt"], w["qg"], w["kgc"], tm=256)

    def per_group(a):
        return a.reshape(NG, DH, b, s).transpose(2, 0, 3, 1)

    def stride_rows(a):
        rows = a.reshape(b, NG, nc, CMP_STRIDE * DH)
        last = jnp.tile(a[:, :, s - 1:s, :], (1, 1, 1, CMP_STRIDE))
        pad = jnp.zeros((b, NG, 7, CMP_STRIDE * DH), a.dtype)
        return jnp.concatenate([rows, last, pad], axis=2)

    onehot = (jnp.arange(s)[:, None] // SLC_BLOCK % NB == jnp.arange(NB)[None, :]).astype(BF16)
    ktail = jnp.concatenate([onehot, jnp.zeros((s, KA - DH - NB), BF16)], axis=1)

    def keys_aug(a):
        return jnp.concatenate([per_group(a), jnp.broadcast_to(ktail, (b, NG, s, KA - DH))], axis=-1)

    def value_tiles(a):
        a = a.reshape(NG, DH, b, s)
        tail = jnp.concatenate([jnp.ones((NG, 1, b, s), BF16), jnp.zeros((NG, VROWS - DH - 1, b, s), BF16)], axis=1)
        return jnp.concatenate([a, tail], axis=1).reshape(NG, VROWS, b, nq, QB).transpose(2, 0, 3, 1, 4)

    kc, vcT = _compress(stride_rows(per_group(cT[:NSA_KV_W])), stride_rows(per_group(cT[NSA_KV_W:])),
                        w["pos_k"], w["pos_v"], w["w1k"], w["w2k"], w["w1v"], w["w2vT"], w["kgr"], nc=nc)
    gates = gT[:NSA_GATE_W].reshape(3, NG, HG, b, s).transpose(3, 1, 0, 2, 4).reshape(b, NG, 3 * HG, s)
    gates = jnp.pad(gates, ((0, 0), (0, 0), (0, 16 - 3 * HG), (0, 0)))
    ya = _nsa(qT, kc, vcT, keys_aug(kT[:NSA_KV_W]), value_tiles(vT[:NSA_KV_W]),
              keys_aug(kT[NSA_KV_W:]), value_tiles(vT[NSA_KV_W:]), gates, b=b, s=s)
    yb = _retention(zn, *tabs, b=b, s=s, ts=min(s, 1024))
    x2 = _mix(x2, ya, yb, zn, w["wa"], w["wb"], w["wo"], tm=512)
    return _ffn(x2, p2, w["norm_mlp"], w["w1"], w["w2"], w["norm_ple"], w["wg"], w["wp"], tm=512)


def kernel(x, p, norm_mix, w_in, nsa_q_norm, nsa_k_norm, cmp_pos_k, cmp_pos_v, cmp_w1_k, cmp_w2_k,
           cmp_w1_v, cmp_w2_v, w_up_nsa, w_up_ret, w_out, norm_mlp, w_ff1, w_ff2, norm_ple, w_ple,
           w_ple_gate):
    b, s, _ = x.shape
    depth = w_in.shape[0]
    tabs = _retention_tables(s)
    x2 = x.reshape(b * s, D_MODEL)
    q_end = NSA_Q_W
    kv = NSA_KV_W
    g_end = NSA_Q_W + 6 * kv + NSA_GATE_W
    for i in range(depth):
        wi = w_in[i]
        wt = jnp.concatenate([
            wi[:, 0:q_end + 2 * kv],
            wi[:, q_end + 2 * kv:q_end + 3 * kv], wi[:, q_end + 4 * kv:q_end + 5 * kv],
            wi[:, q_end + 3 * kv:q_end + 4 * kv], wi[:, q_end + 5 * kv:q_end + 6 * kv],
            wi[:, q_end + 6 * kv:g_end],
            jnp.zeros((D_MODEL, GATE_ROWS - NSA_GATE_W), wi.dtype)], axis=1).T.astype(BF16)
        w = dict(
            norm_mix=norm_mix[i][None, :], wn=wi[:, g_end:].astype(BF16), wt=wt,
            qg=nsa_q_norm[i][:, None], kgc=nsa_k_norm[i][:, None],
            kgr=jnp.pad(nsa_k_norm[i][None, :], ((0, 0), (0, KA - DH))),
            pos_k=cmp_pos_k[i].reshape(1, CMP_BLOCK * DH), pos_v=cmp_pos_v[i].reshape(1, CMP_BLOCK * DH),
            w1k=cmp_w1_k[i].astype(BF16), w2k=jnp.pad(cmp_w2_k[i], ((0, 0), (0, KA - DH))).astype(BF16),
            w1v=cmp_w1_v[i].astype(BF16), w2vT=cmp_w2_v[i].T.astype(BF16),
            wa=w_up_nsa[i].astype(BF16), wb=w_up_ret[i].astype(BF16), wo=w_out[i].astype(BF16),
            norm_mlp=norm_mlp[i][None, :], w1=w_ff1[i].astype(BF16), w2=w_ff2[i].astype(BF16),
            norm_ple=norm_ple[i][None, :], wg=w_ple_gate[i].astype(BF16), wp=w_ple[i].astype(BF16),
        )
        x2 = _layer(x2, p[i].reshape(b * s, PLE_DIM), w, tabs, b=b, s=s)
    return x2.reshape(b, s, D_MODEL)
```

```python
import functools
import math

import jax
import jax.numpy as jnp
from jax import lax
from jax.experimental import pallas as pl
from jax.experimental.pallas import tpu as pltpu

D_MODEL = 1024
PLE_DIM = 256
NSA_HEADS = 8
DH = 64
NG = 2
HG = NSA_HEADS // NG
CMP_BLOCK = 32
CMP_STRIDE = 16
CMP_HIDDEN = 4 * DH
SLC_BLOCK = 64
N_SELECT = 16
WINDOW = 512
QB = 128
RET_HEADS = 4
DK = 128
RET_CHUNK = 128
ROPE_BASE = 10000.0
D_FF = 4 * D_MODEL
EPS = 1e-6
MASK_VALUE = -1e30
FORCE_SCORE = 1e4
LOG2E = math.log2(math.e)

NSA_Q_W = NSA_HEADS * DH
NSA_KV_W = NG * DH
NSA_GATE_W = 3 * NSA_HEADS
RET_W = RET_HEADS * DK
NAT_W = 4 * RET_W + 2 * D_MODEL
GATE_ROWS = 32
T_ROWS = NSA_Q_W + 6 * NSA_KV_W + GATE_ROWS

KT = 512
NB = KT // SLC_BLOCK
KA = 128
VROWS = DH + 16
BIAS_ROWS = 16

VMEM_LIMIT = 56 * 1024 * 1024

F32 = jnp.float32
BF16 = jnp.bfloat16


def _dot(a, b):
    return jnp.dot(a, b, preferred_element_type=F32)


def _dot_nt(a, b):
    return lax.dot_general(a, b, (((1,), (1,)), ((), ())), preferred_element_type=F32)


def _dot_tn(a, b):
    return lax.dot_general(a, b, (((0,), (0,)), ((), ())), preferred_element_type=F32)


def _rms_rows(x, g):
    y = x * lax.rsqrt(jnp.mean(x * x, axis=-1, keepdims=True) + EPS)
    return y * g


def _params(n_axes):
    return pltpu.CompilerParams(dimension_semantics=("arbitrary",) * n_axes,
                                vmem_limit_bytes=VMEM_LIMIT)


def _proj_kernel(x_ref, g_ref, wn_ref, wt_ref, qg_ref, kg_ref,
                 zn_ref, qT_ref, cz_ref, ksa_ref, kwa_ref, vsa_ref, vwa_ref, gT_ref):
    h = _rms_rows(x_ref[...], g_ref[...]).astype(BF16)
    zn_ref[...] = _dot(h, wn_ref[...])
    zt = _dot_nt(wt_ref[...], h)
    qg = qg_ref[...]
    kg = kg_ref[...]

    def norm_cols(z, g):
        return z * lax.rsqrt(jnp.mean(z * z, axis=0, keepdims=True) + EPS) * g

    for hh in range(NSA_HEADS):
        rows = slice(hh * DH, (hh + 1) * DH)
        qT_ref[rows, :] = (norm_cols(zt[rows, :], qg) * (DH ** -0.5 * LOG2E)).astype(BF16)
    tm = zt.shape[1]
    zpad = jnp.zeros((KA - DH, tm), F32)
    o = NSA_Q_W
    for j in range(2 * NG):
        rows = slice(o + j * DH, o + (j + 1) * DH)
        cz_ref[j] = jnp.concatenate([zt[rows, :], zpad], axis=0).T[:, 0:DH]
    o += 2 * NSA_KV_W
    lane_row = lax.broadcasted_iota(jnp.int32, (KA - DH, tm), 0)
    pos = pl.program_id(0) * tm + lax.broadcasted_iota(jnp.int32, (KA - DH, tm), 1)
    ind = jnp.where(lane_row == ((pos // SLC_BLOCK) & (NB - 1)), 1.0, 0.0)
    for j, ref in enumerate((ksa_ref, ksa_ref, kwa_ref, kwa_ref)):
        rows = slice(o + j * DH, o + (j + 1) * DH)
        ref[j % NG] = jnp.concatenate([norm_cols(zt[rows, :], kg), ind], axis=0).T.astype(BF16)
    o += 2 * NSA_KV_W
    tail_row = lax.broadcasted_iota(jnp.int32, (VROWS - DH, tm), 0)
    vtail = jnp.where(tail_row == 0, 1.0, 0.0)
    for j, ref in enumerate((vsa_ref, vsa_ref, vwa_ref, vwa_ref)):
        rows = slice(o + j * DH, o + (j + 1) * DH)
        va = jnp.concatenate([zt[rows, :], vtail], axis=0).astype(BF16)
        for u in range(tm // QB):
            ref[j % NG, u] = va[:, u * QB:(u + 1) * QB]
    o += 2 * NSA_KV_W
    gT_ref[...] = jax.nn.sigmoid(zt[o:o + GATE_ROWS, :])


def _proj(x2, g, wn, wt, qg, kg, *, tm):
    n = x2.shape[0]
    full = lambda i: (0, 0)
    colblk = lambda i: (0, i)
    rowblk3 = lambda i: (0, i, 0)
    rowblk4 = lambda i: (0, i, 0, 0)
    return pl.pallas_call(
        _proj_kernel,
        grid=(n // tm,),
        in_specs=[
            pl.BlockSpec((tm, D_MODEL), lambda i: (i, 0)),
            pl.BlockSpec((1, D_MODEL), full),
            pl.BlockSpec((D_MODEL, NAT_W), full),
            pl.BlockSpec((T_ROWS, D_MODEL), full),
            pl.BlockSpec((DH, 1), full),
            pl.BlockSpec((DH, 1), full),
        ],
        out_specs=[
            pl.BlockSpec((tm, NAT_W), lambda i: (i, 0)),
            pl.BlockSpec((NSA_Q_W, tm), colblk),
            pl.BlockSpec((2 * NG, tm, DH), rowblk3),
            pl.BlockSpec((NG, tm, KA), rowblk3),
            pl.BlockSpec((NG, tm, KA), rowblk3),
            pl.BlockSpec((NG, tm // QB, VROWS, QB), rowblk4),
            pl.BlockSpec((NG, tm // QB, VROWS, QB), rowblk4),
            pl.BlockSpec((GATE_ROWS, tm), colblk),
        ],
        out_shape=[
            jax.ShapeDtypeStruct((n, NAT_W), F32),
            jax.ShapeDtypeStruct((NSA_Q_W, n), BF16),
            jax.ShapeDtypeStruct((2 * NG, n, DH), F32),
            jax.ShapeDtypeStruct((NG, n, KA), BF16),
            jax.ShapeDtypeStruct((NG, n, KA), BF16),
            jax.ShapeDtypeStruct((NG, n // QB, VROWS, QB), BF16),
            jax.ShapeDtypeStruct((NG, n // QB, VROWS, QB), BF16),
            jax.ShapeDtypeStruct((GATE_ROWS, n), F32),
        ],
        compiler_params=_params(1),
        name="proj",
    )(x2, g, wn, wt, qg, kg)


def _gelu_tanh(x):
    cdf = 0.5 * (1.0 + jnp.tanh(math.sqrt(2.0 / math.pi) * (x + 0.044715 * (x ** 3))))
    return x * cdf


def _cmp_kernel(zk_ref, zv_ref, pk_ref, pv_ref, w1k_ref, w2k_ref, w1v_ref, w2vT_ref, kg_ref,
                kc_ref, vcT_ref, *, nc):
    half = CMP_STRIDE * DH

    def hidden(z_ref, p_ref, w1_ref):
        a = (z_ref[0, 0, 0:nc, :] + p_ref[:, 0:half]).astype(BF16)
        b = (z_ref[0, 0, 1:nc + 1, :] + p_ref[:, half:2 * half]).astype(BF16)
        return _gelu_tanh(_dot(a, w1_ref[0:half, :]) + _dot(b, w1_ref[half:2 * half, :]))

    hk = hidden(zk_ref, pk_ref, w1k_ref).astype(BF16)
    k = _dot(hk, w2k_ref[...])
    k = k * lax.rsqrt(jnp.sum(k * k, axis=-1, keepdims=True) * (1.0 / DH) + EPS) * kg_ref[...]
    row = lax.broadcasted_iota(jnp.int32, (nc, KA), 0)
    lane = lax.broadcasted_iota(jnp.int32, (nc, KA), 1)
    kc_ref[0, 0] = (k + jnp.where(lane - DH == (row >> 3), 1.0, 0.0)).astype(BF16)
    hv = hidden(zv_ref, pv_ref, w1v_ref).astype(BF16)
    vcT_ref[0, 0] = _dot_nt(w2vT_ref[...], hv).astype(BF16)


def _compress(cz, pk, pv, w1k, w2k, w1v, w2vT, kg, *, nc):
    b = cz.shape[1]
    rows = cz.shape[2]
    half = CMP_STRIDE * DH
    full = lambda i, j: (0, 0)
    return pl.pallas_call(
        functools.partial(_cmp_kernel, nc=nc),
        grid=(b, NG),
        in_specs=[
            pl.BlockSpec((1, 1, rows, half), lambda i, j: (j, i, 0, 0)),
            pl.BlockSpec((1, 1, rows, half), lambda i, j: (NG + j, i, 0, 0)),
            pl.BlockSpec((1, 2 * half), full),
            pl.BlockSpec((1, 2 * half), full),
            pl.BlockSpec((2 * half, CMP_HIDDEN), full),
            pl.BlockSpec((CMP_HIDDEN, KA), full),
            pl.BlockSpec((2 * half, CMP_HIDDEN), full),
            pl.BlockSpec((DH, CMP_HIDDEN), full),
            pl.BlockSpec((1, KA), full),
        ],
        out_specs=[
            pl.BlockSpec((1, 1, nc, KA), lambda i, j: (i, j, 0, 0)),
            pl.BlockSpec((1, 1, DH, nc), lambda i, j: (i, j, 0, 0)),
        ],
        out_shape=[
            jax.ShapeDtypeStruct((b, NG, nc, KA), BF16),
            jax.ShapeDtypeStruct((b, NG, DH, nc), BF16),
        ],
        compiler_params=_params(2),
        name="compress",
    )(cz, cz, pk, pv, w1k, w2k, w1v, w2vT, kg)


def _nsa_kernel(q_ref, qn_ref, kc_ref, vcT_ref, ks_ref, vsT_ref, kw_ref, vwT_ref, gate_ref, out_ref,
                qz_ref, qw_ref, qs0_ref, qs1_ref, qc_ref, sc_ref, ps_ref, score_ref, sel_ref, seln_ref,
                ms_ref, as_ref, sw_ref, aw_ref, sd_ref, oc_ref, ocn_ref, s0_ref, s1_ref, mx0_ref, mx1_ref,
                *, nc, ns, n_sel):
    c = pl.program_id(2)
    ratio = SLC_BLOCK // CMP_STRIDE
    W = HG * QB
    nwt = WINDOW // QB

    def heads_on_lanes(ref):
        return jnp.concatenate([ref[h * DH:(h + 1) * DH, :] for h in range(HG)], axis=1)

    def set_bias(ref, rows):
        b = jnp.concatenate([rows] * HG, axis=1)
        ref[DH:DH + BIAS_ROWS, :] = jnp.concatenate(
            [b, jnp.zeros((BIAS_ROWS - NB, W), F32)], axis=0).astype(BF16)

    def compressed_scores(ct, qt):
        grp = lax.broadcasted_iota(jnp.int32, (KA - DH, W), 0)
        qc_ref[0:DH, :] = qt
        qc_ref[DH:KA, :] = jnp.where(grp <= ct, 0.0, MASK_VALUE).astype(BF16)
        sc_ref[...] = _dot(kc_ref[0, 0], qc_ref[...])

    def select_blocks(ct):
        r0 = pl.multiple_of(jnp.maximum(ct - 1, 0) * 8, 8)
        row = r0 + lax.broadcasted_iota(jnp.int32, (16, W), 0)
        tcol = ct * QB + (lax.broadcasted_iota(jnp.int32, (16, W), 1) & (QB - 1))
        edge = sc_ref[pl.ds(r0, 16), :]
        sc_ref[pl.ds(r0, 16), :] = jnp.where(row * CMP_STRIDE + (CMP_BLOCK - 1) <= tcol, edge, MASK_VALUE)
        s = sc_ref[...]
        e = jnp.exp2(s - jnp.max(s, axis=0, keepdims=True))
        t1 = ct * QB + (lax.broadcasted_iota(jnp.int32, (1, W), 1) & (QB - 1))
        inv = jnp.where(t1 >= CMP_BLOCK - 1, 1.0 / jnp.maximum(jnp.sum(e, axis=0, keepdims=True), 1e-30), 0.0)
        ocn_ref[...] = _dot(vcT_ref[0, 0], e.astype(BF16)) * inv

        psum = e[:, 0:QB] * inv[:, 0:QB]
        for h in range(1, HG):
            psum = psum + e[:, h * QB:(h + 1) * QB] * inv[:, h * QB:(h + 1) * QB]
        ps_ref[0:8, :] = jnp.zeros((8, QB), F32)
        ps_ref[8:8 + nc, :] = psum
        imp = ps_ref[pl.ds(8, ns, stride=ratio), :]
        for r in range(1, ratio):
            imp = imp + ps_ref[pl.ds(8 + r, ns, stride=ratio), :]
        imp = imp + ps_ref[pl.ds(7, ns, stride=ratio), :]
        blk = lax.broadcasted_iota(jnp.int32, (ns, QB), 0)
        tq = ct * QB + lax.broadcasted_iota(jnp.int32, (ns, QB), 1)
        cur = tq // SLC_BLOCK
        score = jnp.where(blk == cur, FORCE_SCORE, imp)
        score = jnp.where(blk == cur - 1, 2 * FORCE_SCORE, score)
        score = jnp.where(blk == 0, 3 * FORCE_SCORE, score)
        score = jnp.where(blk > cur, MASK_VALUE, score)
        score_ref[...] = score

        before_diag = blk < 2 * ct
        work = score
        thr = None
        for _ in range(n_sel):
            thr = jnp.max(work, axis=0, keepdims=True)
            work = jnp.where(work == thr, -jnp.inf, work)
        fast = score >= thr
        cnt = jnp.sum(jnp.where(fast, 1.0, 0.0), axis=0, keepdims=True)
        n_tied = jnp.sum(jnp.where(cnt != float(n_sel), 1.0, 0.0))
        all_selected = 2 * ct + 2 <= n_sel
        seln_ref[...] = jnp.where(before_diag & (fast | all_selected), 1.0, 0.0)

        @pl.when(jnp.logical_and(jnp.logical_not(all_selected), n_tied > 0.0))
        def _():
            def rank_body(jp, rank):
                other = jnp.broadcast_to(score_ref[pl.ds(jp, 1), :], (ns, QB))
                beats = (other > score) | ((other == score) & (jp < blk))
                return rank + jnp.where(beats, 1.0, 0.0)

            rank = lax.fori_loop(0, jnp.minimum(2 * ct + 2, ns), rank_body, jnp.zeros((ns, QB), F32))
            seln_ref[...] = jnp.where(before_diag & (rank < float(n_sel)), 1.0, 0.0)

    @pl.when(c == 0)
    def _():
        compressed_scores(c, heads_on_lanes(q_ref))
        select_blocks(c)

    sel_ref[...] = seln_ref[...]
    oc_ref[...] = ocn_ref[...]
    q = heads_on_lanes(q_ref)
    qz_ref[...] = jnp.concatenate([q, jnp.zeros((KA - DH, W), BF16)], axis=0)
    for r in (qw_ref, qs0_ref, qs1_ref):
        r[0:DH, :] = q
        r[DH + BIAS_ROWS:KA, :] = jnp.zeros((KA - DH - BIAS_ROWS, W), BF16)

    def gate_rows(br):
        return jnp.concatenate(
            [jnp.broadcast_to(gate_ref[br * HG + h:br * HG + h + 1, :], (DH, QB)) for h in range(HG)],
            axis=1)

    rr = lax.broadcasted_iota(jnp.int32, (QB, QB), 0)
    cc = lax.broadcasted_iota(jnp.int32, (QB, QB), 1)

    def mask_first_tile(s, emask):
        top = jnp.concatenate(
            [jnp.where(emask, s[0:QB, h * QB:(h + 1) * QB], MASK_VALUE) for h in range(HG)], axis=1)
        return top if s.shape[0] == QB else jnp.concatenate([top, s[QB:, :]], axis=0)

    def normalised(a_ref):
        a = a_ref[...]
        return a[0:DH, :] * (1.0 / jnp.maximum(a[DH:DH + 1, :], 1e-30))

    last_tile = ns // NB - 1

    def scores(qs_ref, s_ref, mx_ref, t4):
        t4 = jnp.minimum(t4, last_tile)
        rows = sel_ref[pl.ds(pl.multiple_of(t4 * NB, NB), NB), :]
        set_bias(qs_ref, jnp.where(rows > 0.5, 0.0, MASK_VALUE))
        s = _dot(ks_ref[0,pl.ds(pl.multiple_of(t4 * KT, KT), KT), :], qs_ref[...])
        s_ref[...] = s
        mx_ref[...] = jnp.max(s, axis=0, keepdims=True)

    def accumulate(s_ref, mx_ref, t4):
        m_old = ms_ref[...]
        m_new = jnp.maximum(m_old, mx_ref[...])
        pt = jnp.exp2(s_ref[...] - m_new).astype(BF16)
        vT = jnp.concatenate([vsT_ref[0,(KT // QB) * t4 + i] for i in range(KT // QB)], axis=1)
        as_ref[...] = jnp.exp2(m_old - m_new) * as_ref[...] + _dot(vT, pt)
        ms_ref[...] = m_new

    nxt = jnp.minimum(c + 1, pl.num_programs(2) - 1)
    compressed_scores(nxt, heads_on_lanes(qn_ref))
    scores(qs0_ref, s0_ref, mx0_ref, 0)

    diag = pl.ds(pl.multiple_of(c * QB, QB), QB)
    qz = qz_ref[...]
    kt0 = jnp.maximum(c - nwt, 0)
    b8 = lax.broadcasted_iota(jnp.int32, (NB, QB), 0)
    tile_of_col = ((b8 - 2 * kt0) & (NB - 1)) >> 1
    set_bias(qw_ref, jnp.where(tile_of_col < jnp.minimum(c, nwt), 0.0, MASK_VALUE))
    kwc = kw_ref[0,pl.ds(pl.multiple_of(kt0 * QB, QB), nwt * QB), :]
    sw_ref[0:nwt * QB, :] = mask_first_tile(_dot(kwc, qw_ref[...]), (rr > cc) | (c < nwt))
    sw_ref[nwt * QB:, :] = mask_first_tile(_dot(kw_ref[0,diag, :], qz), rr <= cc)
    sd_ref[...] = mask_first_tile(_dot(ks_ref[0,diag, :], qz), rr <= cc)

    sw = sw_ref[...]
    pw = jnp.exp2(sw - jnp.max(sw, axis=0, keepdims=True)).astype(BF16)
    vw = jnp.concatenate([vwT_ref[0,kt0 + i] for i in range(nwt)] + [vwT_ref[0,c]], axis=1)
    aw_ref[...] = _dot(vw, pw)
    sd = sd_ref[...]
    md = jnp.max(sd, axis=0, keepdims=True)
    ms_ref[...] = md
    as_ref[...] = _dot(vsT_ref[0,c], jnp.exp2(sd - md).astype(BF16))
    select_blocks(nxt)

    def sel_body(j, carry):
        scores(qs1_ref, s1_ref, mx1_ref, 2 * j + 1)
        accumulate(s0_ref, mx0_ref, 2 * j)
        scores(qs0_ref, s0_ref, mx0_ref, 2 * j + 2)
        accumulate(s1_ref, mx1_ref, 2 * j + 1)
        return carry

    n_tiles = (c * QB + KT - 1) // KT
    lax.fori_loop(0, (n_tiles + 1) // 2, sel_body, 0)

    o = gate_rows(0) * oc_ref[...] + gate_rows(1) * normalised(as_ref) + gate_rows(2) * normalised(aw_ref)
    for hp in range(HG // 2):
        pair = jnp.concatenate([o[:, (2 * hp) * QB:(2 * hp + 1) * QB],
                                o[:, (2 * hp + 1) * QB:(2 * hp + 2) * QB]], axis=0)
        out_ref[:, hp * 2 * DH:(hp + 1) * 2 * DH] = pair.T.astype(BF16)


def _nsa(qT, kc, vcT, ks, vsT, kw, vwT, gates, *, b, s):
    assert s % (2 * KT) == 0 and s >= WINDOW + QB and s // CMP_STRIDE // 8 <= KA - DH
    nq = s // QB
    nc = s // CMP_STRIDE
    ns = s // SLC_BLOCK
    n_sel = min(N_SELECT, ns)
    W = HG * QB
    bg = lambda i, j, c: (i, j, 0, 0)
    keys = pl.BlockSpec((1, s, KA), lambda i, j, c: (j, i, 0))
    values = pl.BlockSpec((1, nq, VROWS, QB), lambda i, j, c: (j, i, 0, 0))
    qa = pltpu.VMEM((KA, W), BF16)
    return pl.pallas_call(
        functools.partial(_nsa_kernel, nc=nc, ns=ns, n_sel=n_sel),
        grid=(b, NG, nq),
        in_specs=[
            pl.BlockSpec((HG * DH, QB), lambda i, j, c: (j, i * nq + c)),
            pl.BlockSpec((HG * DH, QB), lambda i, j, c: (j, i * nq + jnp.minimum(c + 1, nq - 1))),
            pl.BlockSpec((1, 1, nc, KA), bg),
            pl.BlockSpec((1, 1, DH, nc), bg),
            keys, values, keys, values,
            pl.BlockSpec((GATE_ROWS // NG, QB), lambda i, j, c: (j, i * nq + c)),
        ],
        out_specs=pl.BlockSpec((QB, HG * DH), lambda i, j, c: (i * nq + c, j)),
        out_shape=jax.ShapeDtypeStruct((b * s, NSA_Q_W), BF16),
        scratch_shapes=[
            qa, qa, qa, qa, qa,
            pltpu.VMEM((nc, W), F32),
            pltpu.VMEM((8 + nc, QB), F32),
            pltpu.VMEM((ns, QB), F32),
            pltpu.VMEM((ns, QB), F32),
            pltpu.VMEM((ns, QB), F32),
            pltpu.VMEM((1, W), F32),
            pltpu.VMEM((VROWS, W), F32),
            pltpu.VMEM((WINDOW + QB, W), F32),
            pltpu.VMEM((VROWS, W), F32),
            pltpu.VMEM((QB, W), F32),
            pltpu.VMEM((DH, W), F32),
            pltpu.VMEM((DH, W), F32),
            pltpu.VMEM((KT, W), F32),
            pltpu.VMEM((KT, W), F32),
            pltpu.VMEM((1, W), F32),
            pltpu.VMEM((1, W), F32),
        ],
        compiler_params=_params(3),
        name="nsa",
    )(qT, qT, kc, vcT, ks, vsT, kw, vwT, gates)


def _ret_kernel(q_ref, k_ref, v_ref, g_ref, cos_ref, sin_ref, decay_ref, xi_ref, zeta_ref, gch_ref,
                o_ref, r_ref, *, nchunk):
    @pl.when(pl.program_id(1) == 0)
    def _():
        r_ref[...] = jnp.zeros((RET_HEADS, DK, DK), F32)

    for ci in range(nchunk):
        rows = slice(ci * RET_CHUNK, (ci + 1) * RET_CHUNK)
        cosv = cos_ref[rows, :]
        sinv = sin_ref[rows, :]

        def rope(x):
            return x * cosv + pltpu.roll(x, DK // 2, 1) * sinv

        for h in range(RET_HEADS):
            cols = slice(h * DK, (h + 1) * DK)
            q = rope(q_ref[rows, cols])
            k = rope(k_ref[rows, cols]) * (DK ** -0.5)
            v = v_ref[rows, cols].astype(BF16)
            r_prev = r_ref[h]
            inner = _dot_nt(q.astype(BF16), k.astype(BF16)) * decay_ref[h]
            y = _dot(inner.astype(BF16), v) + _dot((q * xi_ref[h]).astype(BF16), r_prev.astype(BF16))
            y = y * lax.rsqrt(jnp.mean(y * y, axis=-1, keepdims=True) + EPS)
            g = g_ref[rows, cols]
            o_ref[rows, cols] = (g * jax.nn.sigmoid(g) * y).astype(BF16)
            r_ref[h] = gch_ref[h, 0:1, :] * r_prev + _dot_tn((k * zeta_ref[h]).astype(BF16), v)


def _retention(zn, cosf, sinf, decay, xi, zeta, gch, *, b, s, ts):
    nst = s // ts
    col = lambda blk: (lambda i, j: (i * nst + j, blk))
    tab = lambda i, j: (j, 0)
    whole = lambda i, j: (0, 0, 0)
    return pl.pallas_call(
        functools.partial(_ret_kernel, nchunk=ts // RET_CHUNK),
        grid=(b, nst),
        in_specs=[
            pl.BlockSpec((ts, RET_W), col(0)),
            pl.BlockSpec((ts, RET_W), col(1)),
            pl.BlockSpec((ts, RET_W), col(2)),
            pl.BlockSpec((ts, RET_W), col(3)),
            pl.BlockSpec((ts, DK), tab),
            pl.BlockSpec((ts, DK), tab),
            pl.BlockSpec((RET_HEADS, RET_CHUNK, RET_CHUNK), whole),
            pl.BlockSpec((RET_HEADS, RET_CHUNK, DK), whole),
            pl.BlockSpec((RET_HEADS, RET_CHUNK, DK), whole),
            pl.BlockSpec((RET_HEADS, 8, DK), whole),
        ],
        out_specs=pl.BlockSpec((ts, RET_W), lambda i, j: (i * nst + j, 0)),
        out_shape=jax.ShapeDtypeStruct((b * s, RET_W), BF16),
        scratch_shapes=[pltpu.VMEM((RET_HEADS, DK, DK), F32)],
        compiler_params=_params(2),
        name="retention",
    )(zn, zn, zn, zn, cosf, sinf, decay, xi, zeta, gch)


def _mix_kernel(x_ref, ya_ref, yb_ref, ma_ref, mb_ref, wa_ref, wb_ref, wo_ref, o_ref):
    ua = _dot(ya_ref[...], wa_ref[...])
    ub = _dot(yb_ref[...], wb_ref[...])
    mix = jax.nn.sigmoid(ma_ref[...]) * ua + jax.nn.sigmoid(mb_ref[...]) * ub
    o_ref[...] = x_ref[...] + _dot(mix.astype(BF16), wo_ref[...])


def _mix(x2, ya, yb, zn, wa, wb, wo, *, tm):
    n = x2.shape[0]
    full = lambda i: (0, 0)
    ma_blk = 4 * RET_W // D_MODEL
    return pl.pallas_call(
        _mix_kernel,
        grid=(n // tm,),
        in_specs=[
            pl.BlockSpec((tm, D_MODEL), lambda i: (i, 0)),
            pl.BlockSpec((tm, NSA_Q_W), lambda i: (i, 0)),
            pl.BlockSpec((tm, RET_W), lambda i: (i, 0)),
            pl.BlockSpec((tm, D_MODEL), lambda i: (i, ma_blk)),
            pl.BlockSpec((tm, D_MODEL), lambda i: (i, ma_blk + 1)),
            pl.BlockSpec((NSA_Q_W, D_MODEL), full),
            pl.BlockSpec((RET_W, D_MODEL), full),
            pl.BlockSpec((D_MODEL, D_MODEL), full),
        ],
        out_specs=pl.BlockSpec((tm, D_MODEL), lambda i: (i, 0)),
        out_shape=jax.ShapeDtypeStruct((n, D_MODEL), F32),
        compiler_params=_params(1),
        name="mix",
    )(x2, ya, yb, zn, zn, wa, wb, wo)


def _ffn_kernel(x_ref, p_ref, gm_ref, w1_ref, w2_ref, gp_ref, wg_ref, wp_ref, o_ref, *, ff_chunk):
    x = x_ref[...]
    h = _rms_rows(x, gm_ref[...]).astype(BF16)
    for j in range(D_FF // ff_chunk):
        cols = slice(j * ff_chunk, (j + 1) * ff_chunk)
        u = jnp.square(jnp.maximum(_dot(h, w1_ref[:, cols]), 0.0)).astype(BF16)
        x = x + _dot(u, w2_ref[cols, :])
    gate = jax.nn.sigmoid(_dot(_rms_rows(x, gp_ref[...]).astype(BF16), wg_ref[...]))
    o_ref[...] = x + gate * _dot(p_ref[...].astype(BF16), wp_ref[...])


def _ffn(x2, p2, gm, w1, w2, gp, wg, wp, *, tm):
    n = x2.shape[0]
    full = lambda i: (0, 0)
    once = pl.Buffered(1)
    return pl.pallas_call(
        functools.partial(_ffn_kernel, ff_chunk=1024),
        grid=(n // tm,),
        in_specs=[
            pl.BlockSpec((tm, D_MODEL), lambda i: (i, 0)),
            pl.BlockSpec((tm, PLE_DIM), lambda i: (i, 0)),
            pl.BlockSpec((1, D_MODEL), full),
            pl.BlockSpec((D_MODEL, D_FF), full, pipeline_mode=once),
            pl.BlockSpec((D_FF, D_MODEL), full, pipeline_mode=once),
            pl.BlockSpec((1, D_MODEL), full),
            pl.BlockSpec((D_MODEL, D_MODEL), full, pipeline_mode=once),
            pl.BlockSpec((PLE_DIM, D_MODEL), full, pipeline_mode=once),
        ],
        out_specs=pl.BlockSpec((tm, D_MODEL), lambda i: (i, 0)),
        out_shape=jax.ShapeDtypeStruct((n, D_MODEL), F32),
        compiler_params=_params(1),
        name="ffn",
    )(x2, p2, gm, w1, w2, gp, wg, wp)


def _retention_tables(s):
    half = DK // 2
    pos = jnp.arange(s, dtype=F32)
    inv = ROPE_BASE ** (-jnp.arange(half, dtype=F32) / half)
    ang = pos[:, None] * inv[None, :]
    cos, sin = jnp.cos(ang), jnp.sin(ang)
    cosf = jnp.concatenate([cos, cos], axis=-1)
    sinf = jnp.concatenate([-sin, sin], axis=-1)
    C = RET_CHUNK
    gamma = 1.0 - 2.0 ** (-5.0 - jnp.arange(RET_HEADS, dtype=F32))
    lg = jnp.log(gamma)
    n = jnp.arange(C, dtype=F32)
    diff = n[:, None] - n[None, :]
    decay = jnp.where(diff >= 0, jnp.exp(lg[:, None, None] * jnp.maximum(diff, 0.0)), 0.0)
    xi = jnp.broadcast_to(jnp.exp(lg[:, None] * (n + 1.0))[:, :, None], (RET_HEADS, C, DK))
    zeta = jnp.broadcast_to(jnp.exp(lg[:, None] * (C - 1.0 - n))[:, :, None], (RET_HEADS, C, DK))
    gch = jnp.broadcast_to(jnp.exp(lg * C)[:, None, None], (RET_HEADS, 8, DK))
    return cosf, sinf, decay, xi, zeta, gch


def _layer(x2, p2, w, tabs, *, b, s):
    nc = s // CMP_STRIDE
    zn, qT, cz, ksa, kwa, vsa, vwa, gT = _proj(x2, w["norm_mix"], w["wn"], w["wt"], w["qg"], w["kgc"], tm=256)
    cz = cz.reshape(2 * NG, b, s, DH)
    last = jnp.tile(cz[:, :, s - 1:s, :], (1, 1, 1, CMP_STRIDE))
    pad = jnp.zeros((2 * NG, b, 7, CMP_STRIDE * DH), cz.dtype)
    cz = jnp.concatenate([cz.reshape(2 * NG, b, nc, CMP_STRIDE * DH), last, pad], axis=2)
    kc, vcT = _compress(cz, w["pos_k"], w["pos_v"], w["w1k"], w["w2k"], w["w1v"], w["w2vT"], w["kgr"], nc=nc)
    ya = _nsa(qT, kc, vcT, ksa, vsa, kwa, vwa, gT, b=b, s=s)
    yb = _retention(zn, *tabs, b=b, s=s, ts=min(s, 1024))
    x2 = _mix(x2, ya, yb, zn, w["wa"], w["wb"], w["wo"], tm=512)
    return _ffn(x2, p2, w["norm_mlp"], w["w1"], w["w2"], w["norm_ple"], w["wg"], w["wp"], tm=512)


def kernel(x, p, norm_mix, w_in, nsa_q_norm, nsa_k_norm, cmp_pos_k, cmp_pos_v, cmp_w1_k, cmp_w2_k,
           cmp_w1_v, cmp_w2_v, w_up_nsa, w_up_ret, w_out, norm_mlp, w_ff1, w_ff2, norm_ple, w_ple,
           w_ple_gate):
    b, s, _ = x.shape
    depth = w_in.shape[0]
    tabs = _retention_tables(s)
    x2 = x.reshape(b * s, D_MODEL)
    q_end = NSA_Q_W
    kv = NSA_KV_W
    g_end = NSA_Q_W + 6 * kv + NSA_GATE_W
    for i in range(depth):
        wi = w_in[i]
        wg = wi[:, q_end + 6 * kv:g_end].reshape(D_MODEL, 3, NG, HG).transpose(0, 2, 1, 3)
        wg = jnp.pad(wg.reshape(D_MODEL, NG, 3 * HG), ((0, 0), (0, 0), (0, GATE_ROWS // NG - 3 * HG)))
        wt = jnp.concatenate([
            wi[:, 0:q_end + 2 * kv],
            wi[:, q_end + 2 * kv:q_end + 3 * kv], wi[:, q_end + 4 * kv:q_end + 5 * kv],
            wi[:, q_end + 3 * kv:q_end + 4 * kv], wi[:, q_end + 5 * kv:q_end + 6 * kv],
            wg.reshape(D_MODEL, GATE_ROWS)], axis=1).T.astype(BF16)
        w = dict(
            norm_mix=norm_mix[i][None, :], wn=wi[:, g_end:].astype(BF16), wt=wt,
            qg=nsa_q_norm[i][:, None], kgc=nsa_k_norm[i][:, None],
            kgr=jnp.pad(nsa_k_norm[i][None, :], ((0, 0), (0, KA - DH))),
            pos_k=cmp_pos_k[i].reshape(1, CMP_BLOCK * DH), pos_v=cmp_pos_v[i].reshape(1, CMP_BLOCK * DH),
            w1k=cmp_w1_k[i].astype(BF16), w2k=jnp.pad(cmp_w2_k[i], ((0, 0), (0, KA - DH))).astype(BF16),
            w1v=cmp_w1_v[i].astype(BF16), w2vT=cmp_w2_v[i].T.astype(BF16),
            wa=w_up_nsa[i].astype(BF16), wb=w_up_ret[i].astype(BF16), wo=w_out[i].astype(BF16),
            norm_mlp=norm_mlp[i][None, :], w1=w_ff1[i].astype(BF16), w2=w_ff2[i].astype(BF16),
            norm_ple=norm_ple[i][None, :], wg=w_ple_gate[i].astype(BF16), wp=w_ple[i].astype(BF16),
        )
        x2 = _layer(x2, p[i].reshape(b * s, PLE_DIM), w, tabs, b=b, s=s)
    return x2.reshape(b, s, D_MODEL)
```

```python
import functools
import math

import jax
import jax.numpy as jnp
from jax import lax
from jax.experimental import pallas as pl
from jax.experimental.pallas import tpu as pltpu

D_MODEL = 1024
PLE_DIM = 256
NSA_HEADS = 8
DH = 64
NG = 2
HG = NSA_HEADS // NG
CMP_BLOCK = 32
CMP_STRIDE = 16
CMP_HIDDEN = 4 * DH
SLC_BLOCK = 64
N_SELECT = 16
WINDOW = 512
QB = 128
RET_HEADS = 4
DK = 128
RET_CHUNK = 128
ROPE_BASE = 10000.0
D_FF = 4 * D_MODEL
EPS = 1e-6
MASK_VALUE = -1e30
FORCE_SCORE = 1e4
LOG2E = math.log2(math.e)

NSA_Q_W = NSA_HEADS * DH
NSA_KV_W = NG * DH
NSA_GATE_W = 3 * NSA_HEADS
RET_W = RET_HEADS * DK
NAT_W = 4 * RET_W + 2 * D_MODEL
GATE_ROWS = 32
T_ROWS = NSA_Q_W + 6 * NSA_KV_W + GATE_ROWS

KT = 512
NB = KT // SLC_BLOCK
KA = 128
VROWS = DH + 16
BIAS_ROWS = 16

VMEM_LIMIT = 56 * 1024 * 1024

F32 = jnp.float32
BF16 = jnp.bfloat16


def _dot(a, b):
    return jnp.dot(a, b, preferred_element_type=F32)


def _dot_nt(a, b):
    return lax.dot_general(a, b, (((1,), (1,)), ((), ())), preferred_element_type=F32)


def _dot_tn(a, b):
    return lax.dot_general(a, b, (((0,), (0,)), ((), ())), preferred_element_type=F32)


def _rms_rows(x, g):
    y = x * lax.rsqrt(jnp.mean(x * x, axis=-1, keepdims=True) + EPS)
    return y * g


def _params(n_axes):
    return pltpu.CompilerParams(dimension_semantics=("arbitrary",) * n_axes,
                                vmem_limit_bytes=VMEM_LIMIT)


def _proj_kernel(x_ref, g_ref, wn_ref, wt_ref, qg_ref, kg_ref,
                 zn_ref, qT_ref, cz_ref, ksa_ref, kwa_ref, vsa_ref, vwa_ref, gT_ref):
    h = _rms_rows(x_ref[...], g_ref[...]).astype(BF16)
    zn_ref[...] = _dot(h, wn_ref[...])
    zt = _dot_nt(wt_ref[...], h)
    qg = qg_ref[...]
    kg = kg_ref[...]

    def norm_cols(z, g):
        return z * lax.rsqrt(jnp.mean(z * z, axis=0, keepdims=True) + EPS) * g

    for hh in range(NSA_HEADS):
        rows = slice(hh * DH, (hh + 1) * DH)
        qT_ref[rows, :] = (norm_cols(zt[rows, :], qg) * (DH ** -0.5 * LOG2E)).astype(BF16)
    tm = zt.shape[1]
    zpad = jnp.zeros((KA - DH, tm), F32)
    o = NSA_Q_W
    for j in range(2 * NG):
        rows = slice(o + j * DH, o + (j + 1) * DH)
        cz_ref[j] = jnp.concatenate([zt[rows, :], zpad], axis=0).T[:, 0:DH]
    o += 2 * NSA_KV_W
    lane_row = lax.broadcasted_iota(jnp.int32, (KA - DH, tm), 0)
    pos = pl.program_id(0) * tm + lax.broadcasted_iota(jnp.int32, (KA - DH, tm), 1)
    ind = jnp.where(lane_row == ((pos // SLC_BLOCK) & (NB - 1)), 1.0, 0.0)
    for j, ref in enumerate((ksa_ref, ksa_ref, kwa_ref, kwa_ref)):
        rows = slice(o + j * DH, o + (j + 1) * DH)
        ref[j % NG] = jnp.concatenate([norm_cols(zt[rows, :], kg), ind], axis=0).T.astype(BF16)
    o += 2 * NSA_KV_W
    tail_row = lax.broadcasted_iota(jnp.int32, (VROWS - DH, tm), 0)
    vtail = jnp.where(tail_row == 0, 1.0, 0.0)
    for j, ref in enumerate((vsa_ref, vsa_ref, vwa_ref, vwa_ref)):
        rows = slice(o + j * DH, o + (j + 1) * DH)
        va = jnp.concatenate([zt[rows, :], vtail], axis=0).astype(BF16)
        for u in range(tm // QB):
            ref[j % NG, u] = va[:, u * QB:(u + 1) * QB]
    o += 2 * NSA_KV_W
    gT_ref[...] = jax.nn.sigmoid(zt[o:o + GATE_ROWS, :])


def _proj(x2, g, wn, wt, qg, kg, *, tm):
    n = x2.shape[0]
    full = lambda i: (0, 0)
    colblk = lambda i: (0, i)
    rowblk3 = lambda i: (0, i, 0)
    rowblk4 = lambda i: (0, i, 0, 0)
    return pl.pallas_call(
        _proj_kernel,
        grid=(n // tm,),
        in_specs=[
            pl.BlockSpec((tm, D_MODEL), lambda i: (i, 0)),
            pl.BlockSpec((1, D_MODEL), full),
            pl.BlockSpec((D_MODEL, NAT_W), full),
            pl.BlockSpec((T_ROWS, D_MODEL), full),
            pl.BlockSpec((DH, 1), full),
            pl.BlockSpec((DH, 1), full),
        ],
        out_specs=[
            pl.BlockSpec((tm, NAT_W), lambda i: (i, 0)),
            pl.BlockSpec((NSA_Q_W, tm), colblk),
            pl.BlockSpec((2 * NG, tm, DH), rowblk3),
            pl.BlockSpec((NG, tm, KA), rowblk3),
            pl.BlockSpec((NG, tm, KA), rowblk3),
            pl.BlockSpec((NG, tm // QB, VROWS, QB), rowblk4),
            pl.BlockSpec((NG, tm // QB, VROWS, QB), rowblk4),
            pl.BlockSpec((GATE_ROWS, tm), colblk),
        ],
        out_shape=[
            jax.ShapeDtypeStruct((n, NAT_W), F32),
            jax.ShapeDtypeStruct((NSA_Q_W, n), BF16),
            jax.ShapeDtypeStruct((2 * NG, n, DH), F32),
            jax.ShapeDtypeStruct((NG, n, KA), BF16),
            jax.ShapeDtypeStruct((NG, n, KA), BF16),
            jax.ShapeDtypeStruct((NG, n // QB, VROWS, QB), BF16),
            jax.ShapeDtypeStruct((NG, n // QB, VROWS, QB), BF16),
            jax.ShapeDtypeStruct((GATE_ROWS, n), F32),
        ],
        compiler_params=_params(1),
        name="proj",
    )(x2, g, wn, wt, qg, kg)


def _gelu_tanh(x):
    cdf = 0.5 * (1.0 + jnp.tanh(math.sqrt(2.0 / math.pi) * (x + 0.044715 * (x ** 3))))
    return x * cdf


def _cmp_kernel(zk_ref, zv_ref, pk_ref, pv_ref, w1k_ref, w2k_ref, w1v_ref, w2vT_ref, kg_ref,
                kc_ref, vcT_ref, *, nc):
    half = CMP_STRIDE * DH

    def hidden(z_ref, p_ref, w1_ref):
        a = (z_ref[0, 0, 0:nc, :] + p_ref[:, 0:half]).astype(BF16)
        b = (z_ref[0, 0, 1:nc + 1, :] + p_ref[:, half:2 * half]).astype(BF16)
        return _gelu_tanh(_dot(a, w1_ref[0:half, :]) + _dot(b, w1_ref[half:2 * half, :]))

    hk = hidden(zk_ref, pk_ref, w1k_ref).astype(BF16)
    k = _dot(hk, w2k_ref[...])
    k = k * lax.rsqrt(jnp.sum(k * k, axis=-1, keepdims=True) * (1.0 / DH) + EPS) * kg_ref[...]
    row = lax.broadcasted_iota(jnp.int32, (nc, KA), 0)
    lane = lax.broadcasted_iota(jnp.int32, (nc, KA), 1)
    kc_ref[0, 0] = (k + jnp.where(lane - DH == (row >> 3), 1.0, 0.0)).astype(BF16)
    hv = hidden(zv_ref, pv_ref, w1v_ref).astype(BF16)
    vcT_ref[0, 0] = _dot_nt(w2vT_ref[...], hv).astype(BF16)


def _compress(cz, pk, pv, w1k, w2k, w1v, w2vT, kg, *, nc):
    b = cz.shape[1]
    rows = cz.shape[2]
    half = CMP_STRIDE * DH
    full = lambda i, j: (0, 0)
    return pl.pallas_call(
        functools.partial(_cmp_kernel, nc=nc),
        grid=(b, NG),
        in_specs=[
            pl.BlockSpec((1, 1, rows, half), lambda i, j: (j, i, 0, 0)),
            pl.BlockSpec((1, 1, rows, half), lambda i, j: (NG + j, i, 0, 0)),
            pl.BlockSpec((1, 2 * half), full),
            pl.BlockSpec((1, 2 * half), full),
            pl.BlockSpec((2 * half, CMP_HIDDEN), full),
            pl.BlockSpec((CMP_HIDDEN, KA), full),
            pl.BlockSpec((2 * half, CMP_HIDDEN), full),
            pl.BlockSpec((DH, CMP_HIDDEN), full),
            pl.BlockSpec((1, KA), full),
        ],
        out_specs=[
            pl.BlockSpec((1, 1, nc, KA), lambda i, j: (i, j, 0, 0)),
            pl.BlockSpec((1, 1, DH, nc), lambda i, j: (i, j, 0, 0)),
        ],
        out_shape=[
            jax.ShapeDtypeStruct((b, NG, nc, KA), BF16),
            jax.ShapeDtypeStruct((b, NG, DH, nc), BF16),
        ],
        compiler_params=_params(2),
        name="compress",
    )(cz, cz, pk, pv, w1k, w2k, w1v, w2vT, kg)


def _nsa_kernel(q_ref, qn_ref, kc_ref, vcT_ref, ks_ref, vsT_ref, kw_ref, vwT_ref, gate_ref, out_ref,
                qz_ref, qw_ref, qs0_ref, qs1_ref, qc_ref, sc_ref, ps_ref, score_ref, sel_ref, seln_ref,
                ms_ref, as_ref, sw_ref, aw_ref, sd_ref, oc_ref, ocn_ref, s0_ref, s1_ref, mx0_ref, mx1_ref,
                *, nc, ns, n_sel):
    c = pl.program_id(2)
    ratio = SLC_BLOCK // CMP_STRIDE
    W = HG * QB
    nwt = WINDOW // QB

    def heads_on_lanes(ref):
        return jnp.concatenate([ref[h * DH:(h + 1) * DH, :] for h in range(HG)], axis=1)

    def set_bias(ref, rows):
        b = jnp.concatenate([rows] * HG, axis=1)
        ref[DH:DH + BIAS_ROWS, :] = jnp.concatenate(
            [b, jnp.zeros((BIAS_ROWS - NB, W), F32)], axis=0).astype(BF16)

    def compressed_scores(ct, qt):
        grp = lax.broadcasted_iota(jnp.int32, (KA - DH, W), 0)
        qc_ref[0:DH, :] = qt
        qc_ref[DH:KA, :] = jnp.where(grp <= ct, 0.0, MASK_VALUE).astype(BF16)
        sc_ref[...] = _dot(kc_ref[0, 0], qc_ref[...])

    def select_blocks(ct):
        r0 = pl.multiple_of(jnp.maximum(ct - 1, 0) * 8, 8)
        row = r0 + lax.broadcasted_iota(jnp.int32, (16, W), 0)
        tcol = ct * QB + (lax.broadcasted_iota(jnp.int32, (16, W), 1) & (QB - 1))
        edge = sc_ref[pl.ds(r0, 16), :]
        sc_ref[pl.ds(r0, 16), :] = jnp.where(row * CMP_STRIDE + (CMP_BLOCK - 1) <= tcol, edge, MASK_VALUE)
        s = sc_ref[...]
        e = jnp.exp2(s - jnp.max(s, axis=0, keepdims=True))
        t1 = ct * QB + (lax.broadcasted_iota(jnp.int32, (1, W), 1) & (QB - 1))
        inv = jnp.where(t1 >= CMP_BLOCK - 1, 1.0 / jnp.maximum(jnp.sum(e, axis=0, keepdims=True), 1e-30), 0.0)
        ocn_ref[...] = _dot(vcT_ref[0, 0], e.astype(BF16)) * inv

        psum = e[:, 0:QB] * inv[:, 0:QB]
        for h in range(1, HG):
            psum = psum + e[:, h * QB:(h + 1) * QB] * inv[:, h * QB:(h + 1) * QB]
        ps_ref[0:8, :] = jnp.zeros((8, QB), F32)
        ps_ref[8:8 + nc, :] = psum
        imp = ps_ref[pl.ds(8, ns, stride=ratio), :]
        for r in range(1, ratio):
            imp = imp + ps_ref[pl.ds(8 + r, ns, stride=ratio), :]
        imp = imp + ps_ref[pl.ds(7, ns, stride=ratio), :]
        blk = lax.broadcasted_iota(jnp.int32, (ns, QB), 0)
        tq = ct * QB + lax.broadcasted_iota(jnp.int32, (ns, QB), 1)
        cur = tq // SLC_BLOCK
        score = jnp.where(blk == cur, FORCE_SCORE, imp)
        score = jnp.where(blk == cur - 1, 2 * FORCE_SCORE, score)
        score = jnp.where(blk == 0, 3 * FORCE_SCORE, score)
        score = jnp.where(blk > cur, MASK_VALUE, score)
        score_ref[...] = score

        before_diag = blk < 2 * ct
        work = jnp.where((blk == 0) | (blk == cur) | (blk == cur - 1), -jnp.inf, score)
        thr = None
        for _ in range(n_sel - 3):
            thr = jnp.max(work, axis=0, keepdims=True)
            work = jnp.where(work == thr, -jnp.inf, work)
        fast = score >= thr
        cnt = jnp.sum(jnp.where(fast, 1.0, 0.0), axis=0, keepdims=True)
        n_tied = jnp.sum(jnp.where(cnt != float(n_sel), 1.0, 0.0))
        all_selected = 2 * ct + 2 <= n_sel
        seln_ref[...] = jnp.where(before_diag & (fast | all_selected), 1.0, 0.0)
        return jnp.logical_and(jnp.logical_not(all_selected), n_tied > 0.0)

    def rank_blocks(ct):
        score = score_ref[...]
        blk = lax.broadcasted_iota(jnp.int32, (ns, QB), 0)

        def rank_body(jp, rank):
            other = jnp.broadcast_to(score_ref[pl.ds(jp, 1), :], (ns, QB))
            beats = (other > score) | ((other == score) & (jp < blk))
            return rank + jnp.where(beats, 1.0, 0.0)

        rank = lax.fori_loop(0, jnp.minimum(2 * ct + 2, ns), rank_body, jnp.zeros((ns, QB), F32))
        seln_ref[...] = jnp.where((blk < 2 * ct) & (rank < float(n_sel)), 1.0, 0.0)

    @pl.when(c == 0)
    def _():
        compressed_scores(c, heads_on_lanes(q_ref))
        pl.when(select_blocks(c))(lambda: rank_blocks(c))

    sel_ref[...] = seln_ref[...]
    oc_ref[...] = ocn_ref[...]
    q = heads_on_lanes(q_ref)
    qz_ref[...] = jnp.concatenate([q, jnp.zeros((KA - DH, W), BF16)], axis=0)
    for r in (qw_ref, qs0_ref, qs1_ref):
        r[0:DH, :] = q
        r[DH + BIAS_ROWS:KA, :] = jnp.zeros((KA - DH - BIAS_ROWS, W), BF16)

    def gate_rows(br):
        return jnp.concatenate(
            [jnp.broadcast_to(gate_ref[br * HG + h:br * HG + h + 1, :], (DH, QB)) for h in range(HG)],
            axis=1)

    rr = lax.broadcasted_iota(jnp.int32, (QB, QB), 0)
    cc = lax.broadcasted_iota(jnp.int32, (QB, QB), 1)

    def mask_first_tile(s, emask):
        top = jnp.concatenate(
            [jnp.where(emask, s[0:QB, h * QB:(h + 1) * QB], MASK_VALUE) for h in range(HG)], axis=1)
        return top if s.shape[0] == QB else jnp.concatenate([top, s[QB:, :]], axis=0)

    def normalised(a_ref):
        a = a_ref[...]
        return a[0:DH, :] * (1.0 / jnp.maximum(a[DH:DH + 1, :], 1e-30))

    last_tile = ns // NB - 1

    def scores(qs_ref, s_ref, mx_ref, t4):
        t4 = jnp.minimum(t4, last_tile)
        rows = sel_ref[pl.ds(pl.multiple_of(t4 * NB, NB), NB), :]
        set_bias(qs_ref, jnp.where(rows > 0.5, 0.0, MASK_VALUE))
        s = _dot(ks_ref[0,pl.ds(pl.multiple_of(t4 * KT, KT), KT), :], qs_ref[...])
        s_ref[...] = s
        mx_ref[...] = jnp.max(s, axis=0, keepdims=True)

    def accumulate(s_ref, mx_ref, t4):
        m_old = ms_ref[...]
        m_new = jnp.maximum(m_old, mx_ref[...])
        pt = jnp.exp2(s_ref[...] - m_new).astype(BF16)
        vT = jnp.concatenate([vsT_ref[0,(KT // QB) * t4 + i] for i in range(KT // QB)], axis=1)
        as_ref[...] = jnp.exp2(m_old - m_new) * as_ref[...] + _dot(vT, pt)
        ms_ref[...] = m_new

    nxt = jnp.minimum(c + 1, pl.num_programs(2) - 1)
    compressed_scores(nxt, heads_on_lanes(qn_ref))

    scores(qs0_ref, s0_ref, mx0_ref, 0)

    diag = pl.ds(pl.multiple_of(c * QB, QB), QB)
    qz = qz_ref[...]
    kt0 = jnp.maximum(c - nwt, 0)
    b8 = lax.broadcasted_iota(jnp.int32, (NB, QB), 0)
    tile_of_col = ((b8 - 2 * kt0) & (NB - 1)) >> 1
    set_bias(qw_ref, jnp.where(tile_of_col < jnp.minimum(c, nwt), 0.0, MASK_VALUE))
    kwc = kw_ref[0, pl.ds(pl.multiple_of(kt0 * QB, QB), nwt * QB), :]
    sw_ref[0:nwt * QB, :] = mask_first_tile(_dot(kwc, qw_ref[...]), (rr > cc) | (c < nwt))
    sw_ref[nwt * QB:, :] = mask_first_tile(_dot(kw_ref[0, diag, :], qz), rr <= cc)
    sd_ref[...] = mask_first_tile(_dot(ks_ref[0, diag, :], qz), rr <= cc)

    sw = sw_ref[...]
    pw = jnp.exp2(sw - jnp.max(sw, axis=0, keepdims=True)).astype(BF16)
    vw = jnp.concatenate([vwT_ref[0,kt0 + i] for i in range(nwt)] + [vwT_ref[0,c]], axis=1)
    aw_ref[...] = _dot(vw, pw)
    sd = sd_ref[...]
    md = jnp.max(sd, axis=0, keepdims=True)
    ms_ref[...] = md
    as_ref[...] = _dot(vsT_ref[0,c], jnp.exp2(sd - md).astype(BF16))
    pl.when(select_blocks(nxt))(lambda: rank_blocks(nxt))

    def sel_body(j, carry):
        scores(qs1_ref, s1_ref, mx1_ref, 2 * j + 1)
        accumulate(s0_ref, mx0_ref, 2 * j)
        scores(qs0_ref, s0_ref, mx0_ref, 2 * j + 2)
        accumulate(s1_ref, mx1_ref, 2 * j + 1)
        return carry

    n_tiles = (c * QB + KT - 1) // KT
    trips = jnp.maximum(n_tiles - 1, 0) // 2
    lax.fori_loop(0, trips, sel_body, 0)
    left = n_tiles - 2 * trips

    @pl.when(left == 1)
    def _():
        accumulate(s0_ref, mx0_ref, 2 * trips)

    @pl.when(left == 2)
    def _():
        scores(qs1_ref, s1_ref, mx1_ref, 2 * trips + 1)
        accumulate(s0_ref, mx0_ref, 2 * trips)
        accumulate(s1_ref, mx1_ref, 2 * trips + 1)

    o = gate_rows(0) * oc_ref[...] + gate_rows(1) * normalised(as_ref) + gate_rows(2) * normalised(aw_ref)
    for hp in range(HG // 2):
        pair = jnp.concatenate([o[:, (2 * hp) * QB:(2 * hp + 1) * QB],
                                o[:, (2 * hp + 1) * QB:(2 * hp + 2) * QB]], axis=0)
        out_ref[:, hp * 2 * DH:(hp + 1) * 2 * DH] = pair.T.astype(BF16)


def _nsa(qT, kc, vcT, ks, vsT, kw, vwT, gates, *, b, s):
    assert s % (2 * KT) == 0 and s >= WINDOW + QB and s // CMP_STRIDE // 8 <= KA - DH
    nq = s // QB
    nc = s // CMP_STRIDE
    ns = s // SLC_BLOCK
    n_sel = min(N_SELECT, ns)
    W = HG * QB
    bg = lambda i, j, c: (i, j, 0, 0)
    keys = pl.BlockSpec((1, s, KA), lambda i, j, c: (j, i, 0))
    values = pl.BlockSpec((1, nq, VROWS, QB), lambda i, j, c: (j, i, 0, 0))
    qa = pltpu.VMEM((KA, W), BF16)
    return pl.pallas_call(
        functools.partial(_nsa_kernel, nc=nc, ns=ns, n_sel=n_sel),
        grid=(b, NG, nq),
        in_specs=[
            pl.BlockSpec((HG * DH, QB), lambda i, j, c: (j, i * nq + c)),
            pl.BlockSpec((HG * DH, QB), lambda i, j, c: (j, i * nq + jnp.minimum(c + 1, nq - 1))),
            pl.BlockSpec((1, 1, nc, KA), bg),
            pl.BlockSpec((1, 1, DH, nc), bg),
            keys, values, keys, values,
            pl.BlockSpec((GATE_ROWS // NG, QB), lambda i, j, c: (j, i * nq + c)),
        ],
        out_specs=pl.BlockSpec((QB, HG * DH), lambda i, j, c: (i * nq + c, j)),
        out_shape=jax.ShapeDtypeStruct((b * s, NSA_Q_W), BF16),
        scratch_shapes=[
            qa, qa, qa, qa, qa,
            pltpu.VMEM((nc, W), F32),
            pltpu.VMEM((8 + nc, QB), F32),
            pltpu.VMEM((ns, QB), F32),
            pltpu.VMEM((ns, QB), F32),
            pltpu.VMEM((ns, QB), F32),
            pltpu.VMEM((1, W), F32),
            pltpu.VMEM((VROWS, W), F32),
            pltpu.VMEM((WINDOW + QB, W), F32),
            pltpu.VMEM((VROWS, W), F32),
            pltpu.VMEM((QB, W), F32),
            pltpu.VMEM((DH, W), F32),
            pltpu.VMEM((DH, W), F32),
            pltpu.VMEM((KT, W), F32),
            pltpu.VMEM((KT, W), F32),
            pltpu.VMEM((1, W), F32),
            pltpu.VMEM((1, W), F32),
        ],
        compiler_params=_params(3),
        name="nsa",
    )(qT, qT, kc, vcT, ks, vsT, kw, vwT, gates)


def _ret_kernel(q_ref, k_ref, v_ref, g_ref, cos_ref, sin_ref, decay_ref, xi_ref, zeta_ref, gch_ref,
                o_ref, r_ref, *, nchunk):
    @pl.when(pl.program_id(1) == 0)
    def _():
        r_ref[...] = jnp.zeros((RET_HEADS, DK, DK), F32)

    for ci in range(nchunk):
        rows = slice(ci * RET_CHUNK, (ci + 1) * RET_CHUNK)
        cosv = cos_ref[rows, :]
        sinv = sin_ref[rows, :]

        def rope(x):
            return x * cosv + pltpu.roll(x, DK // 2, 1) * sinv

        for h in range(RET_HEADS):
            cols = slice(h * DK, (h + 1) * DK)
            q = rope(q_ref[rows, cols])
            k = rope(k_ref[rows, cols]) * (DK ** -0.5)
            v = v_ref[rows, cols].astype(BF16)
            r_prev = r_ref[h]
            inner = _dot_nt(q.astype(BF16), k.astype(BF16)) * decay_ref[h]
            y = _dot(inner.astype(BF16), v) + _dot((q * xi_ref[h]).astype(BF16), r_prev.astype(BF16))
            y = y * lax.rsqrt(jnp.mean(y * y, axis=-1, keepdims=True) + EPS)
            g = g_ref[rows, cols]
            o_ref[rows, cols] = (g * jax.nn.sigmoid(g) * y).astype(BF16)
            r_ref[h] = gch_ref[h, 0:1, :] * r_prev + _dot_tn((k * zeta_ref[h]).astype(BF16), v)


def _retention(zn, cosf, sinf, decay, xi, zeta, gch, *, b, s, ts):
    nst = s // ts
    col = lambda blk: (lambda i, j: (i * nst + j, blk))
    tab = lambda i, j: (j, 0)
    whole = lambda i, j: (0, 0, 0)
    return pl.pallas_call(
        functools.partial(_ret_kernel, nchunk=ts // RET_CHUNK),
        grid=(b, nst),
        in_specs=[
            pl.BlockSpec((ts, RET_W), col(0)),
            pl.BlockSpec((ts, RET_W), col(1)),
            pl.BlockSpec((ts, RET_W), col(2)),
            pl.BlockSpec((ts, RET_W), col(3)),
            pl.BlockSpec((ts, DK), tab),
            pl.BlockSpec((ts, DK), tab),
            pl.BlockSpec((RET_HEADS, RET_CHUNK, RET_CHUNK), whole),
            pl.BlockSpec((RET_HEADS, RET_CHUNK, DK), whole),
            pl.BlockSpec((RET_HEADS, RET_CHUNK, DK), whole),
            pl.BlockSpec((RET_HEADS, 8, DK), whole),
        ],
        out_specs=pl.BlockSpec((ts, RET_W), lambda i, j: (i * nst + j, 0)),
        out_shape=jax.ShapeDtypeStruct((b * s, RET_W), BF16),
        scratch_shapes=[pltpu.VMEM((RET_HEADS, DK, DK), F32)],
        compiler_params=_params(2),
        name="retention",
    )(zn, zn, zn, zn, cosf, sinf, decay, xi, zeta, gch)


def _mix_kernel(x_ref, ya_ref, yb_ref, ma_ref, mb_ref, wa_ref, wb_ref, wo_ref, o_ref):
    ua = _dot(ya_ref[...], wa_ref[...])
    ub = _dot(yb_ref[...], wb_ref[...])
    mix = jax.nn.sigmoid(ma_ref[...]) * ua + jax.nn.sigmoid(mb_ref[...]) * ub
    o_ref[...] = x_ref[...] + _dot(mix.astype(BF16), wo_ref[...])


def _mix(x2, ya, yb, zn, wa, wb, wo, *, tm):
    n = x2.shape[0]
    full = lambda i: (0, 0)
    ma_blk = 4 * RET_W // D_MODEL
    return pl.pallas_call(
        _mix_kernel,
        grid=(n // tm,),
        in_specs=[
            pl.BlockSpec((tm, D_MODEL), lambda i: (i, 0)),
            pl.BlockSpec((tm, NSA_Q_W), lambda i: (i, 0)),
            pl.BlockSpec((tm, RET_W), lambda i: (i, 0)),
            pl.BlockSpec((tm, D_MODEL), lambda i: (i, ma_blk)),
            pl.BlockSpec((tm, D_MODEL), lambda i: (i, ma_blk + 1)),
            pl.BlockSpec((NSA_Q_W, D_MODEL), full),
            pl.BlockSpec((RET_W, D_MODEL), full),
            pl.BlockSpec((D_MODEL, D_MODEL), full),
        ],
        out_specs=pl.BlockSpec((tm, D_MODEL), lambda i: (i, 0)),
        out_shape=jax.ShapeDtypeStruct((n, D_MODEL), F32),
        compiler_params=_params(1),
        name="mix",
    )(x2, ya, yb, zn, zn, wa, wb, wo)


def _ffn_kernel(x_ref, p_ref, gm_ref, w1_ref, w2_ref, gp_ref, wg_ref, wp_ref, o_ref, *, ff_chunk):
    x = x_ref[...]
    h = _rms_rows(x, gm_ref[...]).astype(BF16)
    for j in range(D_FF // ff_chunk):
        cols = slice(j * ff_chunk, (j + 1) * ff_chunk)
        u = jnp.square(jnp.maximum(_dot(h, w1_ref[:, cols]), 0.0)).astype(BF16)
        x = x + _dot(u, w2_ref[cols, :])
    gate = jax.nn.sigmoid(_dot(_rms_rows(x, gp_ref[...]).astype(BF16), wg_ref[...]))
    o_ref[...] = x + gate * _dot(p_ref[...].astype(BF16), wp_ref[...])


def _ffn(x2, p2, gm, w1, w2, gp, wg, wp, *, tm):
    n = x2.shape[0]
    full = lambda i: (0, 0)
    once = pl.Buffered(1)
    return pl.pallas_call(
        functools.partial(_ffn_kernel, ff_chunk=1024),
        grid=(n // tm,),
        in_specs=[
            pl.BlockSpec((tm, D_MODEL), lambda i: (i, 0)),
            pl.BlockSpec((tm, PLE_DIM), lambda i: (i, 0)),
            pl.BlockSpec((1, D_MODEL), full),
            pl.BlockSpec((D_MODEL, D_FF), full, pipeline_mode=once),
            pl.BlockSpec((D_FF, D_MODEL), full, pipeline_mode=once),
            pl.BlockSpec((1, D_MODEL), full),
            pl.BlockSpec((D_MODEL, D_MODEL), full, pipeline_mode=once),
            pl.BlockSpec((PLE_DIM, D_MODEL), full, pipeline_mode=once),
        ],
        out_specs=pl.BlockSpec((tm, D_MODEL), lambda i: (i, 0)),
        out_shape=jax.ShapeDtypeStruct((n, D_MODEL), F32),
        compiler_params=_params(1),
        name="ffn",
    )(x2, p2, gm, w1, w2, gp, wg, wp)


def _retention_tables(s):
    half = DK // 2
    pos = jnp.arange(s, dtype=F32)
    inv = ROPE_BASE ** (-jnp.arange(half, dtype=F32) / half)
    ang = pos[:, None] * inv[None, :]
    cos, sin = jnp.cos(ang), jnp.sin(ang)
    cosf = jnp.concatenate([cos, cos], axis=-1)
    sinf = jnp.concatenate([-sin, sin], axis=-1)
    C = RET_CHUNK
    gamma = 1.0 - 2.0 ** (-5.0 - jnp.arange(RET_HEADS, dtype=F32))
    lg = jnp.log(gamma)
    n = jnp.arange(C, dtype=F32)
    diff = n[:, None] - n[None, :]
    decay = jnp.where(diff >= 0, jnp.exp(lg[:, None, None] * jnp.maximum(diff, 0.0)), 0.0)
    xi = jnp.broadcast_to(jnp.exp(lg[:, None] * (n + 1.0))[:, :, None], (RET_HEADS, C, DK))
    zeta = jnp.broadcast_to(jnp.exp(lg[:, None] * (C - 1.0 - n))[:, :, None], (RET_HEADS, C, DK))
    gch = jnp.broadcast_to(jnp.exp(lg * C)[:, None, None], (RET_HEADS, 8, DK))
    return cosf, sinf, decay, xi, zeta, gch


def _layer(x2, p2, w, tabs, *, b, s):
    nc = s // CMP_STRIDE
    zn, qT, cz, ksa, kwa, vsa, vwa, gT = _proj(x2, w["norm_mix"], w["wn"], w["wt"], w["qg"], w["kgc"], tm=256)
    cz = cz.reshape(2 * NG, b, s, DH)
    last = jnp.tile(cz[:, :, s - 1:s, :], (1, 1, 1, CMP_STRIDE))
    pad = jnp.zeros((2 * NG, b, 7, CMP_STRIDE * DH), cz.dtype)
    cz = jnp.concatenate([cz.reshape(2 * NG, b, nc, CMP_STRIDE * DH), last, pad], axis=2)
    kc, vcT = _compress(cz, w["pos_k"], w["pos_v"], w["w1k"], w["w2k"], w["w1v"], w["w2vT"], w["kgr"], nc=nc)
    ya = _nsa(qT, kc, vcT, ksa, vsa, kwa, vwa, gT, b=b, s=s)
    yb = _retention(zn, *tabs, b=b, s=s, ts=min(s, 1024))
    x2 = _mix(x2, ya, yb, zn, w["wa"], w["wb"], w["wo"], tm=512)
    return _ffn(x2, p2, w["norm_mlp"], w["w1"], w["w2"], w["norm_ple"], w["wg"], w["wp"], tm=512)


def kernel(x, p, norm_mix, w_in, nsa_q_norm, nsa_k_norm, cmp_pos_k, cmp_pos_v, cmp_w1_k, cmp_w2_k,
           cmp_w1_v, cmp_w2_v, w_up_nsa, w_up_ret, w_out, norm_mlp, w_ff1, w_ff2, norm_ple, w_ple,
           w_ple_gate):
    b, s, _ = x.shape
    depth = w_in.shape[0]
    tabs = _retention_tables(s)
    x2 = x.reshape(b * s, D_MODEL)
    q_end = NSA_Q_W
    kv = NSA_KV_W
    g_end = NSA_Q_W + 6 * kv + NSA_GATE_W
    for i in range(depth):
        wi = w_in[i]
        wg = wi[:, q_end + 6 * kv:g_end].reshape(D_MODEL, 3, NG, HG).transpose(0, 2, 1, 3)
        wg = jnp.pad(wg.reshape(D_MODEL, NG, 3 * HG), ((0, 0), (0, 0), (0, GATE_ROWS // NG - 3 * HG)))
        wt = jnp.concatenate([
            wi[:, 0:q_end + 2 * kv],
            wi[:, q_end + 2 * kv:q_end + 3 * kv], wi[:, q_end + 4 * kv:q_end + 5 * kv],
            wi[:, q_end + 3 * kv:q_end + 4 * kv], wi[:, q_end + 5 * kv:q_end + 6 * kv],
            wg.reshape(D_MODEL, GATE_ROWS)], axis=1).T.astype(BF16)
        w = dict(
            norm_mix=norm_mix[i][None, :], wn=wi[:, g_end:].astype(BF16), wt=wt,
            qg=nsa_q_norm[i][:, None], kgc=nsa_k_norm[i][:, None],
            kgr=jnp.pad(nsa_k_norm[i][None, :], ((0, 0), (0, KA - DH))),
            pos_k=cmp_pos_k[i].reshape(1, CMP_BLOCK * DH), pos_v=cmp_pos_v[i].reshape(1, CMP_BLOCK * DH),
            w1k=cmp_w1_k[i].astype(BF16), w2k=jnp.pad(cmp_w2_k[i], ((0, 0), (0, KA - DH))).astype(BF16),
            w1v=cmp_w1_v[i].astype(BF16), w2vT=cmp_w2_v[i].T.astype(BF16),
            wa=w_up_nsa[i].astype(BF16), wb=w_up_ret[i].astype(BF16), wo=w_out[i].astype(BF16),
            norm_mlp=norm_mlp[i][None, :], w1=w_ff1[i].astype(BF16), w2=w_ff2[i].astype(BF16),
            norm_ple=norm_ple[i][None, :], wg=w_ple_gate[i].astype(BF16), wp=w_ple[i].astype(BF16),
        )
        x2 = _layer(x2, p[i].reshape(b * s, PLE_DIM), w, tabs, b=b, s=s)
    return x2.reshape(b, s, D_MODEL)
```

```python
import functools
import math

import jax
import jax.numpy as jnp
from jax import lax
from jax.experimental import pallas as pl
from jax.experimental.pallas import tpu as pltpu

D_MODEL = 1024
PLE_DIM = 256
NSA_HEADS = 8
DH = 64
NG = 2
HG = NSA_HEADS // NG
CMP_BLOCK = 32
CMP_STRIDE = 16
CMP_HIDDEN = 4 * DH
SLC_BLOCK = 64
N_SELECT = 16
WINDOW = 512
QB = 128
RET_HEADS = 4
DK = 128
RET_CHUNK = 128
ROPE_BASE = 10000.0
D_FF = 4 * D_MODEL
EPS = 1e-6
MASK_VALUE = -1e30
FORCE_SCORE = 1e4
LOG2E = math.log2(math.e)

NSA_Q_W = NSA_HEADS * DH
NSA_KV_W = NG * DH
NSA_GATE_W = 3 * NSA_HEADS
RET_W = RET_HEADS * DK
NAT_W = 4 * RET_W + 2 * D_MODEL
GATE_ROWS = 32
T_ROWS = NSA_Q_W + 6 * NSA_KV_W + GATE_ROWS

KT = 512
NB = KT // SLC_BLOCK
KA = 128
VROWS = DH + 16
BIAS_ROWS = 16

VMEM_LIMIT = 56 * 1024 * 1024

F32 = jnp.float32
BF16 = jnp.bfloat16


def _dot(a, b):
    return jnp.dot(a, b, preferred_element_type=F32)


def _dot_nt(a, b):
    return lax.dot_general(a, b, (((1,), (1,)), ((), ())), preferred_element_type=F32)


def _dot_tn(a, b):
    return lax.dot_general(a, b, (((0,), (0,)), ((), ())), preferred_element_type=F32)


def _rms_rows(x, g):
    y = x * lax.rsqrt(jnp.mean(x * x, axis=-1, keepdims=True) + EPS)
    return y * g


def _params(n_axes):
    return pltpu.CompilerParams(dimension_semantics=("arbitrary",) * n_axes,
                                vmem_limit_bytes=VMEM_LIMIT)


def _proj_kernel(x_ref, g_ref, wn_ref, wt_ref, qg_ref, kg_ref,
                 zr_ref, zv_ref, zm_ref, qT_ref, cz_ref, ksa_ref, kwa_ref, vsa_ref, vwa_ref, gT_ref):
    h = _rms_rows(x_ref[...], g_ref[...]).astype(BF16)
    zn = _dot(h, wn_ref[...])
    zr_ref[...] = zn[:, 0:3 * RET_W]
    zv_ref[...] = zn[:, 3 * RET_W:4 * RET_W].astype(BF16)
    zm_ref[...] = jax.nn.sigmoid(zn[:, 4 * RET_W:]).astype(BF16)
    zt = _dot_nt(wt_ref[...], h)
    qg = qg_ref[...]
    kg = kg_ref[...]

    def norm_cols(z, g):
        return z * lax.rsqrt(jnp.mean(z * z, axis=0, keepdims=True) + EPS) * g

    for hh in range(NSA_HEADS):
        rows = slice(hh * DH, (hh + 1) * DH)
        qT_ref[rows, :] = (norm_cols(zt[rows, :], qg) * (DH ** -0.5 * LOG2E)).astype(BF16)
    tm = zt.shape[1]
    zpad = jnp.zeros((KA - DH, tm), F32)
    o = NSA_Q_W
    for j in range(2 * NG):
        rows = slice(o + j * DH, o + (j + 1) * DH)
        cz_ref[j] = jnp.concatenate([zt[rows, :], zpad], axis=0).T[:, 0:DH]
    o += 2 * NSA_KV_W
    lane_row = lax.broadcasted_iota(jnp.int32, (KA - DH, tm), 0)
    pos = pl.program_id(0) * tm + lax.broadcasted_iota(jnp.int32, (KA - DH, tm), 1)
    ind = jnp.where(lane_row == ((pos // SLC_BLOCK) & (NB - 1)), 1.0, 0.0)
    for j, ref in enumerate((ksa_ref, ksa_ref, kwa_ref, kwa_ref)):
        rows = slice(o + j * DH, o + (j + 1) * DH)
        ref[j % NG] = jnp.concatenate([norm_cols(zt[rows, :], kg), ind], axis=0).T.astype(BF16)
    o += 2 * NSA_KV_W
    tail_row = lax.broadcasted_iota(jnp.int32, (VROWS - DH, tm), 0)
    vtail = jnp.where(tail_row == 0, 1.0, 0.0)
    for j, ref in enumerate((vsa_ref, vsa_ref, vwa_ref, vwa_ref)):
        rows = slice(o + j * DH, o + (j + 1) * DH)
        va = jnp.concatenate([zt[rows, :], vtail], axis=0).astype(BF16)
        for u in range(tm // QB):
            ref[j % NG, u] = va[:, u * QB:(u + 1) * QB]
    o += 2 * NSA_KV_W
    gT_ref[...] = jax.nn.sigmoid(zt[o:o + GATE_ROWS, :])


def _proj(x2, g, wn, wt, qg, kg, *, tm):
    n = x2.shape[0]
    full = lambda i: (0, 0)
    colblk = lambda i: (0, i)
    rowblk3 = lambda i: (0, i, 0)
    rowblk4 = lambda i: (0, i, 0, 0)
    return pl.pallas_call(
        _proj_kernel,
        grid=(n // tm,),
        in_specs=[
            pl.BlockSpec((tm, D_MODEL), lambda i: (i, 0)),
            pl.BlockSpec((1, D_MODEL), full),
            pl.BlockSpec((D_MODEL, NAT_W), full),
            pl.BlockSpec((T_ROWS, D_MODEL), full),
            pl.BlockSpec((DH, 1), full),
            pl.BlockSpec((DH, 1), full),
        ],
        out_specs=[
            pl.BlockSpec((tm, 3 * RET_W), lambda i: (i, 0)),
            pl.BlockSpec((tm, RET_W), lambda i: (i, 0)),
            pl.BlockSpec((tm, 2 * D_MODEL), lambda i: (i, 0)),
            pl.BlockSpec((NSA_Q_W, tm), colblk),
            pl.BlockSpec((2 * NG, tm, DH), rowblk3),
            pl.BlockSpec((NG, tm, KA), rowblk3),
            pl.BlockSpec((NG, tm, KA), rowblk3),
            pl.BlockSpec((NG, tm // QB, VROWS, QB), rowblk4),
            pl.BlockSpec((NG, tm // QB, VROWS, QB), rowblk4),
            pl.BlockSpec((GATE_ROWS, tm), colblk),
        ],
        out_shape=[
            jax.ShapeDtypeStruct((n, 3 * RET_W), F32),
            jax.ShapeDtypeStruct((n, RET_W), BF16),
            jax.ShapeDtypeStruct((n, 2 * D_MODEL), BF16),
            jax.ShapeDtypeStruct((NSA_Q_W, n), BF16),
            jax.ShapeDtypeStruct((2 * NG, n, DH), F32),
            jax.ShapeDtypeStruct((NG, n, KA), BF16),
            jax.ShapeDtypeStruct((NG, n, KA), BF16),
            jax.ShapeDtypeStruct((NG, n // QB, VROWS, QB), BF16),
            jax.ShapeDtypeStruct((NG, n // QB, VROWS, QB), BF16),
            jax.ShapeDtypeStruct((GATE_ROWS, n), F32),
        ],
        compiler_params=_params(1),
        name="proj",
    )(x2, g, wn, wt, qg, kg)


def _gelu_tanh(x):
    cdf = 0.5 * (1.0 + jnp.tanh(math.sqrt(2.0 / math.pi) * (x + 0.044715 * (x ** 3))))
    return x * cdf


def _cmp_kernel(zk_ref, zv_ref, lk_ref, lv_ref, pk_ref, pv_ref, w1k_ref, w2k_ref, w1v_ref, w2vT_ref, kg_ref,
                kc_ref, vcT_ref, *, nc):
    half = CMP_STRIDE * DH
    last_row = lax.broadcasted_iota(jnp.int32, (nc, CMP_HIDDEN), 0) == nc - 1

    def hidden(z_ref, l_ref, p_ref, w1_ref):
        z = z_ref[0, 0]
        first = _dot((z + p_ref[:, 0:half]).astype(BF16), w1_ref[0:half, :])
        second = _dot((z + p_ref[:, half:2 * half]).astype(BF16), w1_ref[half:2 * half, :])
        clamp = _dot((l_ref[0, 0] + p_ref[:, half:2 * half]).astype(BF16), w1_ref[half:2 * half, :])
        second = jnp.where(last_row, clamp[0:1, :], pltpu.roll(second, nc - 1, 0))
        return _gelu_tanh(first + second)

    hk = hidden(zk_ref, lk_ref, pk_ref, w1k_ref).astype(BF16)
    k = _dot(hk, w2k_ref[...])
    k = k * lax.rsqrt(jnp.sum(k * k, axis=-1, keepdims=True) * (1.0 / DH) + EPS) * kg_ref[...]
    row = lax.broadcasted_iota(jnp.int32, (nc, KA), 0)
    lane = lax.broadcasted_iota(jnp.int32, (nc, KA), 1)
    kc_ref[0, 0] = (k + jnp.where(lane - DH == (row >> 3), 1.0, 0.0)).astype(BF16)
    hv = hidden(zv_ref, lv_ref, pv_ref, w1v_ref).astype(BF16)
    vcT_ref[0, 0] = _dot_nt(w2vT_ref[...], hv).astype(BF16)


def _compress(cz, last, pk, pv, w1k, w2k, w1v, w2vT, kg, *, nc):
    b = cz.shape[1]
    half = CMP_STRIDE * DH
    full = lambda i, j: (0, 0)
    return pl.pallas_call(
        functools.partial(_cmp_kernel, nc=nc),
        grid=(b, NG),
        in_specs=[
            pl.BlockSpec((1, 1, nc, half), lambda i, j: (j, i, 0, 0)),
            pl.BlockSpec((1, 1, nc, half), lambda i, j: (NG + j, i, 0, 0)),
            pl.BlockSpec((1, 1, 8, half), lambda i, j: (j, i, 0, 0)),
            pl.BlockSpec((1, 1, 8, half), lambda i, j: (NG + j, i, 0, 0)),
            pl.BlockSpec((1, 2 * half), full),
            pl.BlockSpec((1, 2 * half), full),
            pl.BlockSpec((2 * half, CMP_HIDDEN), full),
            pl.BlockSpec((CMP_HIDDEN, KA), full),
            pl.BlockSpec((2 * half, CMP_HIDDEN), full),
            pl.BlockSpec((DH, CMP_HIDDEN), full),
            pl.BlockSpec((1, KA), full),
        ],
        out_specs=[
            pl.BlockSpec((1, 1, nc, KA), lambda i, j: (i, j, 0, 0)),
            pl.BlockSpec((1, 1, DH, nc), lambda i, j: (i, j, 0, 0)),
        ],
        out_shape=[
            jax.ShapeDtypeStruct((b, NG, nc, KA), BF16),
            jax.ShapeDtypeStruct((b, NG, DH, nc), BF16),
        ],
        compiler_params=_params(2),
        name="compress",
    )(cz, cz, last, last, pk, pv, w1k, w2k, w1v, w2vT, kg)


def _nsa_kernel(q_ref, qn_ref, kc_ref, vcT_ref, ks_ref, vsT_ref, kw_ref, vwT_ref, gate_ref, out_ref,
                qz_ref, qw_ref, qs0_ref, qs1_ref, qc_ref, sc_ref, ps_ref, score_ref, sel_ref, seln_ref,
                ms_ref, as_ref, sw_ref, aw_ref, sd_ref, oc_ref, ocn_ref, s0_ref, s1_ref, mx0_ref, mx1_ref,
                *, nc, ns, n_sel):
    c = pl.program_id(2)
    ratio = SLC_BLOCK // CMP_STRIDE
    W = HG * QB
    nwt = WINDOW // QB

    def heads_on_lanes(ref):
        return jnp.concatenate([ref[h * DH:(h + 1) * DH, :] for h in range(HG)], axis=1)

    def set_bias(ref, rows):
        b = jnp.concatenate([rows] * HG, axis=1)
        ref[DH:DH + BIAS_ROWS, :] = jnp.concatenate(
            [b, jnp.zeros((BIAS_ROWS - NB, W), F32)], axis=0).astype(BF16)

    def compressed_scores(ct, qt):
        grp = lax.broadcasted_iota(jnp.int32, (KA - DH, W), 0)
        qc_ref[0:DH, :] = qt
        qc_ref[DH:KA, :] = jnp.where(grp <= ct, 0.0, MASK_VALUE).astype(BF16)
        sc_ref[...] = _dot(kc_ref[0, 0], qc_ref[...])

    def select_blocks(ct):
        r0 = pl.multiple_of(jnp.maximum(ct - 1, 0) * 8, 8)
        row = r0 + lax.broadcasted_iota(jnp.int32, (16, W), 0)
        tcol = ct * QB + (lax.broadcasted_iota(jnp.int32, (16, W), 1) & (QB - 1))
        edge = sc_ref[pl.ds(r0, 16), :]
        sc_ref[pl.ds(r0, 16), :] = jnp.where(row * CMP_STRIDE + (CMP_BLOCK - 1) <= tcol, edge, MASK_VALUE)
        s = sc_ref[...]
        e = jnp.exp2(s - jnp.max(s, axis=0, keepdims=True))
        t1 = ct * QB + (lax.broadcasted_iota(jnp.int32, (1, W), 1) & (QB - 1))
        inv = jnp.where(t1 >= CMP_BLOCK - 1, 1.0 / jnp.maximum(jnp.sum(e, axis=0, keepdims=True), 1e-30), 0.0)
        ocn_ref[...] = _dot(vcT_ref[0, 0], e.astype(BF16)) * inv

        psum = e[:, 0:QB] * inv[:, 0:QB]
        for h in range(1, HG):
            psum = psum + e[:, h * QB:(h + 1) * QB] * inv[:, h * QB:(h + 1) * QB]
        ps_ref[0:8, :] = jnp.zeros((8, QB), F32)
        ps_ref[8:8 + nc, :] = psum
        imp = ps_ref[pl.ds(8, ns, stride=ratio), :]
        for r in range(1, ratio):
            imp = imp + ps_ref[pl.ds(8 + r, ns, stride=ratio), :]
        imp = imp + ps_ref[pl.ds(7, ns, stride=ratio), :]
        blk = lax.broadcasted_iota(jnp.int32, (ns, QB), 0)
        tq = ct * QB + lax.broadcasted_iota(jnp.int32, (ns, QB), 1)
        cur = tq // SLC_BLOCK
        score = jnp.where(blk == cur, FORCE_SCORE, imp)
        score = jnp.where(blk == cur - 1, 2 * FORCE_SCORE, score)
        score = jnp.where(blk == 0, 3 * FORCE_SCORE, score)
        score = jnp.where(blk > cur, MASK_VALUE, score)
        score_ref[...] = score

        before_diag = blk < 2 * ct
        work = jnp.where((blk == 0) | (blk == cur) | (blk == cur - 1), -jnp.inf, score)
        thr = None
        for _ in range(n_sel - 3):
            thr = jnp.max(work, axis=0, keepdims=True)
            work = jnp.where(work == thr, -jnp.inf, work)
        fast = score >= thr
        cnt = jnp.sum(jnp.where(fast, 1.0, 0.0), axis=0, keepdims=True)
        n_tied = jnp.sum(jnp.where(cnt != float(n_sel), 1.0, 0.0))
        all_selected = 2 * ct + 2 <= n_sel
        seln_ref[...] = jnp.where(before_diag & (fast | all_selected), 1.0, 0.0)
        return jnp.logical_and(jnp.logical_not(all_selected), n_tied > 0.0)

    def rank_blocks(ct):
        score = score_ref[...]
        blk = lax.broadcasted_iota(jnp.int32, (ns, QB), 0)

        def rank_body(jp, rank):
            other = jnp.broadcast_to(score_ref[pl.ds(jp, 1), :], (ns, QB))
            beats = (other > score) | ((other == score) & (jp < blk))
            return rank + jnp.where(beats, 1.0, 0.0)

        rank = lax.fori_loop(0, jnp.minimum(2 * ct + 2, ns), rank_body, jnp.zeros((ns, QB), F32))
        seln_ref[...] = jnp.where((blk < 2 * ct) & (rank < float(n_sel)), 1.0, 0.0)

    @pl.when(c == 0)
    def _():
        compressed_scores(c, heads_on_lanes(q_ref))
        pl.when(select_blocks(c))(lambda: rank_blocks(c))

    sel_ref[...] = seln_ref[...]
    oc_ref[...] = ocn_ref[...]
    q = heads_on_lanes(q_ref)
    qz_ref[...] = jnp.concatenate([q, jnp.zeros((KA - DH, W), BF16)], axis=0)
    for r in (qw_ref, qs0_ref, qs1_ref):
        r[0:DH, :] = q
        r[DH + BIAS_ROWS:KA, :] = jnp.zeros((KA - DH - BIAS_ROWS, W), BF16)

    def gate_rows(br):
        return jnp.concatenate(
            [jnp.broadcast_to(gate_ref[br * HG + h:br * HG + h + 1, :], (DH, QB)) for h in range(HG)],
            axis=1)

    rr = lax.broadcasted_iota(jnp.int32, (QB, QB), 0)
    cc = lax.broadcasted_iota(jnp.int32, (QB, QB), 1)

    def mask_first_tile(s, emask):
        top = jnp.concatenate(
            [jnp.where(emask, s[0:QB, h * QB:(h + 1) * QB], MASK_VALUE) for h in range(HG)], axis=1)
        return top if s.shape[0] == QB else jnp.concatenate([top, s[QB:, :]], axis=0)

    def normalised(a_ref):
        a = a_ref[...]
        return a[0:DH, :] * (1.0 / jnp.maximum(a[DH:DH + 1, :], 1e-30))

    last_tile = ns // NB - 1

    def scores(qs_ref, s_ref, mx_ref, t4):
        t4 = jnp.minimum(t4, last_tile)
        rows = sel_ref[pl.ds(pl.multiple_of(t4 * NB, NB), NB), :]
        set_bias(qs_ref, jnp.where(rows > 0.5, 0.0, MASK_VALUE))
        s = _dot(ks_ref[0,pl.ds(pl.multiple_of(t4 * KT, KT), KT), :], qs_ref[...])
        s_ref[...] = s
        mx_ref[...] = jnp.max(s, axis=0, keepdims=True)

    def accumulate(s_ref, mx_ref, t4):
        m_old = ms_ref[...]
        m_new = jnp.maximum(m_old, mx_ref[...])
        pt = jnp.exp2(s_ref[...] - m_new).astype(BF16)
        vT = jnp.concatenate([vsT_ref[0,(KT // QB) * t4 + i] for i in range(KT // QB)], axis=1)
        as_ref[...] = jnp.exp2(m_old - m_new) * as_ref[...] + _dot(vT, pt)
        ms_ref[...] = m_new

    nxt = jnp.minimum(c + 1, pl.num_programs(2) - 1)
    compressed_scores(nxt, heads_on_lanes(qn_ref))

    scores(qs0_ref, s0_ref, mx0_ref, 0)

    diag = pl.ds(pl.multiple_of(c * QB, QB), QB)
    qz = qz_ref[...]
    kt0 = jnp.maximum(c - nwt, 0)
    b8 = lax.broadcasted_iota(jnp.int32, (NB, QB), 0)
    tile_of_col = ((b8 - 2 * kt0) & (NB - 1)) >> 1
    set_bias(qw_ref, jnp.where(tile_of_col < jnp.minimum(c, nwt), 0.0, MASK_VALUE))
    kwc = kw_ref[0, pl.ds(pl.multiple_of(kt0 * QB, QB), nwt * QB), :]
    sw_ref[0:nwt * QB, :] = mask_first_tile(_dot(kwc, qw_ref[...]), (rr > cc) | (c < nwt))
    sw_ref[nwt * QB:, :] = mask_first_tile(_dot(kw_ref[0, diag, :], qz), rr <= cc)
    sd_ref[...] = mask_first_tile(_dot(ks_ref[0, diag, :], qz), rr <= cc)

    sw = sw_ref[...]
    pw = jnp.exp2(sw - jnp.max(sw, axis=0, keepdims=True)).astype(BF16)
    vw = jnp.concatenate([vwT_ref[0,kt0 + i] for i in range(nwt)] + [vwT_ref[0,c]], axis=1)
    aw_ref[...] = _dot(vw, pw)
    sd = sd_ref[...]
    md = jnp.max(sd, axis=0, keepdims=True)
    ms_ref[...] = md
    as_ref[...] = _dot(vsT_ref[0,c], jnp.exp2(sd - md).astype(BF16))
    pl.when(select_blocks(nxt))(lambda: rank_blocks(nxt))

    def sel_body(j, carry):
        scores(qs1_ref, s1_ref, mx1_ref, 2 * j + 1)
        accumulate(s0_ref, mx0_ref, 2 * j)
        scores(qs0_ref, s0_ref, mx0_ref, 2 * j + 2)
        accumulate(s1_ref, mx1_ref, 2 * j + 1)
        return carry

    n_tiles = (c * QB + KT - 1) // KT
    trips = jnp.maximum(n_tiles - 1, 0) // 2
    lax.fori_loop(0, trips, sel_body, 0)
    left = n_tiles - 2 * trips

    @pl.when(left == 1)
    def _():
        accumulate(s0_ref, mx0_ref, 2 * trips)

    @pl.when(left == 2)
    def _():
        scores(qs1_ref, s1_ref, mx1_ref, 2 * trips + 1)
        accumulate(s0_ref, mx0_ref, 2 * trips)
        accumulate(s1_ref, mx1_ref, 2 * trips + 1)

    o = gate_rows(0) * oc_ref[...] + gate_rows(1) * normalised(as_ref) + gate_rows(2) * normalised(aw_ref)
    for hp in range(HG // 2):
        pair = jnp.concatenate([o[:, (2 * hp) * QB:(2 * hp + 1) * QB],
                                o[:, (2 * hp + 1) * QB:(2 * hp + 2) * QB]], axis=0)
        out_ref[:, hp * 2 * DH:(hp + 1) * 2 * DH] = pair.T.astype(BF16)


def _nsa(qT, kc, vcT, ks, vsT, kw, vwT, gates, *, b, s):
    assert s % (2 * KT) == 0 and s >= WINDOW + QB and s // CMP_STRIDE // 8 <= KA - DH
    nq = s // QB
    nc = s // CMP_STRIDE
    ns = s // SLC_BLOCK
    n_sel = min(N_SELECT, ns)
    W = HG * QB
    bg = lambda i, j, c: (i, j, 0, 0)
    keys = pl.BlockSpec((1, s, KA), lambda i, j, c: (j, i, 0))
    values = pl.BlockSpec((1, nq, VROWS, QB), lambda i, j, c: (j, i, 0, 0))
    qa = pltpu.VMEM((KA, W), BF16)
    return pl.pallas_call(
        functools.partial(_nsa_kernel, nc=nc, ns=ns, n_sel=n_sel),
        grid=(b, NG, nq),
        in_specs=[
            pl.BlockSpec((HG * DH, QB), lambda i, j, c: (j, i * nq + c)),
            pl.BlockSpec((HG * DH, QB), lambda i, j, c: (j, i * nq + jnp.minimum(c + 1, nq - 1))),
            pl.BlockSpec((1, 1, nc, KA), bg),
            pl.BlockSpec((1, 1, DH, nc), bg),
            keys, values, keys, values,
            pl.BlockSpec((GATE_ROWS // NG, QB), lambda i, j, c: (j, i * nq + c)),
        ],
        out_specs=pl.BlockSpec((QB, HG * DH), lambda i, j, c: (i * nq + c, j)),
        out_shape=jax.ShapeDtypeStruct((b * s, NSA_Q_W), BF16),
        scratch_shapes=[
            qa, qa, qa, qa, qa,
            pltpu.VMEM((nc, W), F32),
            pltpu.VMEM((8 + nc, QB), F32),
            pltpu.VMEM((ns, QB), F32),
            pltpu.VMEM((ns, QB), F32),
            pltpu.VMEM((ns, QB), F32),
            pltpu.VMEM((1, W), F32),
            pltpu.VMEM((VROWS, W), F32),
            pltpu.VMEM((WINDOW + QB, W), F32),
            pltpu.VMEM((VROWS, W), F32),
            pltpu.VMEM((QB, W), F32),
            pltpu.VMEM((DH, W), F32),
            pltpu.VMEM((DH, W), F32),
            pltpu.VMEM((KT, W), F32),
            pltpu.VMEM((KT, W), F32),
            pltpu.VMEM((1, W), F32),
            pltpu.VMEM((1, W), F32),
        ],
        compiler_params=_params(3),
        name="nsa",
    )(qT, qT, kc, vcT, ks, vsT, kw, vwT, gates)


def _ret_kernel(q_ref, k_ref, v_ref, g_ref, cos_ref, sin_ref, decay_ref, xi_ref, zeta_ref, gch_ref,
                o_ref, r_ref, *, nchunk):
    @pl.when(pl.program_id(1) == 0)
    def _():
        r_ref[...] = jnp.zeros((RET_HEADS, DK, DK), F32)

    for ci in range(nchunk):
        rows = slice(ci * RET_CHUNK, (ci + 1) * RET_CHUNK)
        cosv = cos_ref[rows, :]
        sinv = sin_ref[rows, :]

        def rope(x):
            return x * cosv + pltpu.roll(x, DK // 2, 1) * sinv

        for h in range(RET_HEADS):
            cols = slice(h * DK, (h + 1) * DK)
            q = rope(q_ref[rows, cols])
            k = rope(k_ref[rows, cols]) * (DK ** -0.5)
            v = v_ref[rows, cols].astype(BF16)
            r_prev = r_ref[h]
            inner = _dot_nt(q.astype(BF16), k.astype(BF16)) * decay_ref[h]
            y = _dot(inner.astype(BF16), v) + _dot((q * xi_ref[h]).astype(BF16), r_prev.astype(BF16))
            y = y * lax.rsqrt(jnp.mean(y * y, axis=-1, keepdims=True) + EPS)
            g = g_ref[rows, cols]
            o_ref[rows, cols] = (g * jax.nn.sigmoid(g) * y).astype(BF16)
            r_ref[h] = gch_ref[h, 0:1, :] * r_prev + _dot_tn((k * zeta_ref[h]).astype(BF16), v)


def _retention(zr, zv, cosf, sinf, decay, xi, zeta, gch, *, b, s, ts):
    nst = s // ts
    col = lambda blk: (lambda i, j: (i * nst + j, blk))
    tab = lambda i, j: (j, 0)
    whole = lambda i, j: (0, 0, 0)
    return pl.pallas_call(
        functools.partial(_ret_kernel, nchunk=ts // RET_CHUNK),
        grid=(b, nst),
        in_specs=[
            pl.BlockSpec((ts, RET_W), col(0)),
            pl.BlockSpec((ts, RET_W), col(1)),
            pl.BlockSpec((ts, RET_W), col(0)),
            pl.BlockSpec((ts, RET_W), col(2)),
            pl.BlockSpec((ts, DK), tab),
            pl.BlockSpec((ts, DK), tab),
            pl.BlockSpec((RET_HEADS, RET_CHUNK, RET_CHUNK), whole),
            pl.BlockSpec((RET_HEADS, RET_CHUNK, DK), whole),
            pl.BlockSpec((RET_HEADS, RET_CHUNK, DK), whole),
            pl.BlockSpec((RET_HEADS, 8, DK), whole),
        ],
        out_specs=pl.BlockSpec((ts, RET_W), lambda i, j: (i * nst + j, 0)),
        out_shape=jax.ShapeDtypeStruct((b * s, RET_W), BF16),
        scratch_shapes=[pltpu.VMEM((RET_HEADS, DK, DK), F32)],
        compiler_params=_params(2),
        name="retention",
    )(zr, zr, zv, zr, cosf, sinf, decay, xi, zeta, gch)


def _mix_kernel(x_ref, ya_ref, yb_ref, ma_ref, mb_ref, wa_ref, wb_ref, wo_ref, o_ref):
    ua = _dot(ya_ref[...], wa_ref[...])
    ub = _dot(yb_ref[...], wb_ref[...])
    mix = ma_ref[...].astype(F32) * ua + mb_ref[...].astype(F32) * ub
    o_ref[...] = x_ref[...] + _dot(mix.astype(BF16), wo_ref[...])


def _mix(x2, ya, yb, zm, wa, wb, wo, *, tm):
    n = x2.shape[0]
    full = lambda i: (0, 0)
    return pl.pallas_call(
        _mix_kernel,
        grid=(n // tm,),
        in_specs=[
            pl.BlockSpec((tm, D_MODEL), lambda i: (i, 0)),
            pl.BlockSpec((tm, NSA_Q_W), lambda i: (i, 0)),
            pl.BlockSpec((tm, RET_W), lambda i: (i, 0)),
            pl.BlockSpec((tm, D_MODEL), lambda i: (i, 0)),
            pl.BlockSpec((tm, D_MODEL), lambda i: (i, 1)),
            pl.BlockSpec((NSA_Q_W, D_MODEL), full),
            pl.BlockSpec((RET_W, D_MODEL), full),
            pl.BlockSpec((D_MODEL, D_MODEL), full),
        ],
        out_specs=pl.BlockSpec((tm, D_MODEL), lambda i: (i, 0)),
        out_shape=jax.ShapeDtypeStruct((n, D_MODEL), F32),
        compiler_params=_params(1),
        name="mix",
    )(x2, ya, yb, zm, zm, wa, wb, wo)


def _ffn_kernel(x_ref, p_ref, gm_ref, w1_ref, w2_ref, gp_ref, wg_ref, wp_ref, o_ref, *, ff_chunk):
    x = x_ref[...]
    h = _rms_rows(x, gm_ref[...]).astype(BF16)
    for j in range(D_FF // ff_chunk):
        cols = slice(j * ff_chunk, (j + 1) * ff_chunk)
        u = jnp.square(jnp.maximum(_dot(h, w1_ref[:, cols]), 0.0)).astype(BF16)
        x = x + _dot(u, w2_ref[cols, :])
    gate = jax.nn.sigmoid(_dot(_rms_rows(x, gp_ref[...]).astype(BF16), wg_ref[...]))
    o_ref[...] = x + gate * _dot(p_ref[...].astype(BF16), wp_ref[...])


def _ffn(x2, p2, gm, w1, w2, gp, wg, wp, *, tm):
    n = x2.shape[0]
    full = lambda i: (0, 0)
    once = pl.Buffered(1)
    return pl.pallas_call(
        functools.partial(_ffn_kernel, ff_chunk=1024),
        grid=(n // tm,),
        in_specs=[
            pl.BlockSpec((tm, D_MODEL), lambda i: (i, 0)),
            pl.BlockSpec((tm, PLE_DIM), lambda i: (i, 0)),
            pl.BlockSpec((1, D_MODEL), full),
            pl.BlockSpec((D_MODEL, D_FF), full, pipeline_mode=once),
            pl.BlockSpec((D_FF, D_MODEL), full, pipeline_mode=once),
            pl.BlockSpec((1, D_MODEL), full),
            pl.BlockSpec((D_MODEL, D_MODEL), full, pipeline_mode=once),
            pl.BlockSpec((PLE_DIM, D_MODEL), full, pipeline_mode=once),
        ],
        out_specs=pl.BlockSpec((tm, D_MODEL), lambda i: (i, 0)),
        out_shape=jax.ShapeDtypeStruct((n, D_MODEL), F32),
        compiler_params=_params(1),
        name="ffn",
    )(x2, p2, gm, w1, w2, gp, wg, wp)


def _retention_tables(s):
    half = DK // 2
    pos = jnp.arange(s, dtype=F32)
    inv = ROPE_BASE ** (-jnp.arange(half, dtype=F32) / half)
    ang = pos[:, None] * inv[None, :]
    cos, sin = jnp.cos(ang), jnp.sin(ang)
    cosf = jnp.concatenate([cos, cos], axis=-1)
    sinf = jnp.concatenate([-sin, sin], axis=-1)
    C = RET_CHUNK
    gamma = 1.0 - 2.0 ** (-5.0 - jnp.arange(RET_HEADS, dtype=F32))
    lg = jnp.log(gamma)
    n = jnp.arange(C, dtype=F32)
    diff = n[:, None] - n[None, :]
    decay = jnp.where(diff >= 0, jnp.exp(lg[:, None, None] * jnp.maximum(diff, 0.0)), 0.0)
    xi = jnp.broadcast_to(jnp.exp(lg[:, None] * (n + 1.0))[:, :, None], (RET_HEADS, C, DK))
    zeta = jnp.broadcast_to(jnp.exp(lg[:, None] * (C - 1.0 - n))[:, :, None], (RET_HEADS, C, DK))
    gch = jnp.broadcast_to(jnp.exp(lg * C)[:, None, None], (RET_HEADS, 8, DK))
    return cosf, sinf, decay, xi, zeta, gch


def _layer(x2, p2, w, tabs, *, b, s):
    nc = s // CMP_STRIDE
    zr, zv, zm, qT, cz, ksa, kwa, vsa, vwa, gT = _proj(
        x2, w["norm_mix"], w["wn"], w["wt"], w["qg"], w["kgc"], tm=256)
    cz = cz.reshape(2 * NG, b, s, DH)
    last = jnp.tile(cz[:, :, s - 1:s, :], (1, 1, 8, CMP_STRIDE))
    kc, vcT = _compress(cz.reshape(2 * NG, b, nc, CMP_STRIDE * DH), last, w["pos_k"], w["pos_v"],
                        w["w1k"], w["w2k"], w["w1v"], w["w2vT"], w["kgr"], nc=nc)
    ya = _nsa(qT, kc, vcT, ksa, vsa, kwa, vwa, gT, b=b, s=s)
    yb = _retention(zr, zv, *tabs, b=b, s=s, ts=min(s, 1024))
    x2 = _mix(x2, ya, yb, zm, w["wa"], w["wb"], w["wo"], tm=512)
    return _ffn(x2, p2, w["norm_mlp"], w["w1"], w["w2"], w["norm_ple"], w["wg"], w["wp"], tm=512)


def kernel(x, p, norm_mix, w_in, nsa_q_norm, nsa_k_norm, cmp_pos_k, cmp_pos_v, cmp_w1_k, cmp_w2_k,
           cmp_w1_v, cmp_w2_v, w_up_nsa, w_up_ret, w_out, norm_mlp, w_ff1, w_ff2, norm_ple, w_ple,
           w_ple_gate):
    b, s, _ = x.shape
    depth = w_in.shape[0]
    tabs = _retention_tables(s)
    x2 = x.reshape(b * s, D_MODEL)
    q_end = NSA_Q_W
    kv = NSA_KV_W
    g_end = NSA_Q_W + 6 * kv + NSA_GATE_W
    for i in range(depth):
        wi = w_in[i]
        wg = wi[:, q_end + 6 * kv:g_end].reshape(D_MODEL, 3, NG, HG).transpose(0, 2, 1, 3)
        wg = jnp.pad(wg.reshape(D_MODEL, NG, 3 * HG), ((0, 0), (0, 0), (0, GATE_ROWS // NG - 3 * HG)))
        wt = jnp.concatenate([
            wi[:, 0:q_end + 2 * kv],
            wi[:, q_end + 2 * kv:q_end + 3 * kv], wi[:, q_end + 4 * kv:q_end + 5 * kv],
            wi[:, q_end + 3 * kv:q_end + 4 * kv], wi[:, q_end + 5 * kv:q_end + 6 * kv],
            wg.reshape(D_MODEL, GATE_ROWS)], axis=1).T.astype(BF16)
        w = dict(
            norm_mix=norm_mix[i][None, :], wt=wt,
            wn=jnp.concatenate([wi[:, g_end:g_end + 2 * RET_W], wi[:, g_end + 3 * RET_W:g_end + 4 * RET_W],
                                wi[:, g_end + 2 * RET_W:g_end + 3 * RET_W], wi[:, g_end + 4 * RET_W:]],
                               axis=1).astype(BF16),
            qg=nsa_q_norm[i][:, None], kgc=nsa_k_norm[i][:, None],
            kgr=jnp.pad(nsa_k_norm[i][None, :], ((0, 0), (0, KA - DH))),
            pos_k=cmp_pos_k[i].reshape(1, CMP_BLOCK * DH), pos_v=cmp_pos_v[i].reshape(1, CMP_BLOCK * DH),
            w1k=cmp_w1_k[i].astype(BF16), w2k=jnp.pad(cmp_w2_k[i], ((0, 0), (0, KA - DH))).astype(BF16),
            w1v=cmp_w1_v[i].astype(BF16), w2vT=cmp_w2_v[i].T.astype(BF16),
            wa=w_up_nsa[i].astype(BF16), wb=w_up_ret[i].astype(BF16), wo=w_out[i].astype(BF16),
            norm_mlp=norm_mlp[i][None, :], w1=w_ff1[i].astype(BF16), w2=w_ff2[i].astype(BF16),
            norm_ple=norm_ple[i][None, :], wg=w_ple_gate[i].astype(BF16), wp=w_ple[i].astype(BF16),
        )
        x2 = _layer(x2, p[i].reshape(b * s, PLE_DIM), w, tabs, b=b, s=s)
    return x2.reshape(b, s, D_MODEL)
```

```python
import functools
import math

import jax
import jax.numpy as jnp
from jax import lax
from jax.experimental import pallas as pl
from jax.experimental.pallas import tpu as pltpu

D_MODEL = 1024
PLE_DIM = 256
NSA_HEADS = 8
DH = 64
NG = 2
HG = NSA_HEADS // NG
CMP_BLOCK = 32
CMP_STRIDE = 16
CMP_HIDDEN = 4 * DH
SLC_BLOCK = 64
N_SELECT = 16
WINDOW = 512
QB = 128
RET_HEADS = 4
DK = 128
RET_CHUNK = 128
ROPE_BASE = 10000.0
D_FF = 4 * D_MODEL
EPS = 1e-6
MASK_VALUE = -1e30
FORCE_SCORE = 1e4
LOG2E = math.log2(math.e)

NSA_Q_W = NSA_HEADS * DH
NSA_KV_W = NG * DH
NSA_GATE_W = 3 * NSA_HEADS
RET_W = RET_HEADS * DK
NAT_W = 4 * RET_W + 2 * D_MODEL
GATE_ROWS = 32
T_ROWS = NSA_Q_W + 6 * NSA_KV_W + GATE_ROWS

KT = 512
NB = KT // SLC_BLOCK
KA = 128
VROWS = DH + 16
BIAS_ROWS = 16

VMEM_LIMIT = 56 * 1024 * 1024

F32 = jnp.float32
BF16 = jnp.bfloat16


def _dot(a, b):
    return jnp.dot(a, b, preferred_element_type=F32)


def _dot_nt(a, b):
    return lax.dot_general(a, b, (((1,), (1,)), ((), ())), preferred_element_type=F32)


def _dot_tn(a, b):
    return lax.dot_general(a, b, (((0,), (0,)), ((), ())), preferred_element_type=F32)


def _rms_rows(x, g):
    y = x * lax.rsqrt(jnp.mean(x * x, axis=-1, keepdims=True) + EPS)
    return y * g


def _params(n_axes):
    return pltpu.CompilerParams(dimension_semantics=("arbitrary",) * n_axes,
                                vmem_limit_bytes=VMEM_LIMIT)


def _proj_kernel(x_ref, g_ref, wn_ref, wt_ref, qg_ref, kg_ref,
                 zr_ref, zv_ref, zm_ref, qT_ref, cz_ref, ksa_ref, kwa_ref, vsa_ref, vwa_ref, gT_ref):
    h = _rms_rows(x_ref[...], g_ref[...]).astype(BF16)
    zn = _dot(h, wn_ref[...])
    zr_ref[...] = zn[:, 0:3 * RET_W]
    zv_ref[...] = zn[:, 3 * RET_W:4 * RET_W].astype(BF16)
    zm_ref[...] = jax.nn.sigmoid(zn[:, 4 * RET_W:]).astype(BF16)
    zt = _dot_nt(wt_ref[...], h)
    qg = qg_ref[...]
    kg = kg_ref[...]

    def norm_cols(z, g):
        return z * lax.rsqrt(jnp.mean(z * z, axis=0, keepdims=True) + EPS) * g

    for hh in range(NSA_HEADS):
        rows = slice(hh * DH, (hh + 1) * DH)
        qT_ref[rows, :] = (norm_cols(zt[rows, :], qg) * (DH ** -0.5 * LOG2E)).astype(BF16)
    tm = zt.shape[1]
    zpad = jnp.zeros((KA - DH, tm), F32)
    o = NSA_Q_W
    for j in range(2 * NG):
        rows = slice(o + j * DH, o + (j + 1) * DH)
        cz_ref[j] = jnp.concatenate([zt[rows, :], zpad], axis=0).T[:, 0:DH]
    o += 2 * NSA_KV_W
    lane_row = lax.broadcasted_iota(jnp.int32, (KA - DH, tm), 0)
    pos = pl.program_id(0) * tm + lax.broadcasted_iota(jnp.int32, (KA - DH, tm), 1)
    ind = jnp.where(lane_row == ((pos // SLC_BLOCK) & (NB - 1)), 1.0, 0.0)
    for j, ref in enumerate((ksa_ref, ksa_ref, kwa_ref, kwa_ref)):
        rows = slice(o + j * DH, o + (j + 1) * DH)
        ref[j % NG] = jnp.concatenate([norm_cols(zt[rows, :], kg), ind], axis=0).T.astype(BF16)
    o += 2 * NSA_KV_W
    tail_row = lax.broadcasted_iota(jnp.int32, (VROWS - DH, tm), 0)
    vtail = jnp.where(tail_row == 0, 1.0, 0.0)
    for j, ref in enumerate((vsa_ref, vsa_ref, vwa_ref, vwa_ref)):
        rows = slice(o + j * DH, o + (j + 1) * DH)
        va = jnp.concatenate([zt[rows, :], vtail], axis=0).astype(BF16)
        for u in range(tm // QB):
            ref[j % NG, u] = va[:, u * QB:(u + 1) * QB]
    o += 2 * NSA_KV_W
    gT_ref[...] = jax.nn.sigmoid(zt[o:o + GATE_ROWS, :])


def _proj(x2, g, wn, wt, qg, kg, *, tm):
    n = x2.shape[0]
    full = lambda i: (0, 0)
    colblk = lambda i: (0, i)
    rowblk3 = lambda i: (0, i, 0)
    rowblk4 = lambda i: (0, i, 0, 0)
    return pl.pallas_call(
        _proj_kernel,
        grid=(n // tm,),
        in_specs=[
            pl.BlockSpec((tm, D_MODEL), lambda i: (i, 0)),
            pl.BlockSpec((1, D_MODEL), full),
            pl.BlockSpec((D_MODEL, NAT_W), full),
            pl.BlockSpec((T_ROWS, D_MODEL), full),
            pl.BlockSpec((DH, 1), full),
            pl.BlockSpec((DH, 1), full),
        ],
        out_specs=[
            pl.BlockSpec((tm, 3 * RET_W), lambda i: (i, 0)),
            pl.BlockSpec((tm, RET_W), lambda i: (i, 0)),
            pl.BlockSpec((tm, 2 * D_MODEL), lambda i: (i, 0)),
            pl.BlockSpec((NSA_Q_W, tm), colblk),
            pl.BlockSpec((2 * NG, tm, DH), rowblk3),
            pl.BlockSpec((NG, tm, KA), rowblk3),
            pl.BlockSpec((NG, tm, KA), rowblk3),
            pl.BlockSpec((NG, tm // QB, VROWS, QB), rowblk4),
            pl.BlockSpec((NG, tm // QB, VROWS, QB), rowblk4),
            pl.BlockSpec((GATE_ROWS, tm), colblk),
        ],
        out_shape=[
            jax.ShapeDtypeStruct((n, 3 * RET_W), F32),
            jax.ShapeDtypeStruct((n, RET_W), BF16),
            jax.ShapeDtypeStruct((n, 2 * D_MODEL), BF16),
            jax.ShapeDtypeStruct((NSA_Q_W, n), BF16),
            jax.ShapeDtypeStruct((2 * NG, n, DH), F32),
            jax.ShapeDtypeStruct((NG, n, KA), BF16),
            jax.ShapeDtypeStruct((NG, n, KA), BF16),
            jax.ShapeDtypeStruct((NG, n // QB, VROWS, QB), BF16),
            jax.ShapeDtypeStruct((NG, n // QB, VROWS, QB), BF16),
            jax.ShapeDtypeStruct((GATE_ROWS, n), F32),
        ],
        compiler_params=_params(1),
        name="proj",
    )(x2, g, wn, wt, qg, kg)


def _gelu_tanh(x):
    cdf = 0.5 * (1.0 + jnp.tanh(math.sqrt(2.0 / math.pi) * (x + 0.044715 * (x ** 3))))
    return x * cdf


def _cmp_kernel(zk_ref, zv_ref, lk_ref, lv_ref, pk_ref, pv_ref, w1k_ref, w2k_ref, w1v_ref, w2vT_ref, kg_ref,
                kc_ref, vcT_ref, *, nc):
    half = CMP_STRIDE * DH
    last_row = lax.broadcasted_iota(jnp.int32, (nc, CMP_HIDDEN), 0) == nc - 1

    def hidden(z_ref, l_ref, p_ref, w1_ref):
        z = z_ref[0, 0]
        first = _dot((z + p_ref[:, 0:half]).astype(BF16), w1_ref[0:half, :])
        second = _dot((z + p_ref[:, half:2 * half]).astype(BF16), w1_ref[half:2 * half, :])
        clamp = _dot((l_ref[0, 0] + p_ref[:, half:2 * half]).astype(BF16), w1_ref[half:2 * half, :])
        second = jnp.where(last_row, clamp[0:1, :], pltpu.roll(second, nc - 1, 0))
        return _gelu_tanh(first + second)

    hk = hidden(zk_ref, lk_ref, pk_ref, w1k_ref).astype(BF16)
    k = _dot(hk, w2k_ref[...])
    k = k * lax.rsqrt(jnp.sum(k * k, axis=-1, keepdims=True) * (1.0 / DH) + EPS) * kg_ref[...]
    row = lax.broadcasted_iota(jnp.int32, (nc, KA), 0)
    lane = lax.broadcasted_iota(jnp.int32, (nc, KA), 1)
    kc_ref[0, 0] = (k + jnp.where(lane - DH == (row >> 3), 1.0, 0.0)).astype(BF16)
    hv = hidden(zv_ref, lv_ref, pv_ref, w1v_ref).astype(BF16)
    vcT_ref[0, 0] = _dot_nt(w2vT_ref[...], hv).astype(BF16)


def _compress(cz, last, pk, pv, w1k, w2k, w1v, w2vT, kg, *, nc):
    b = cz.shape[1]
    half = CMP_STRIDE * DH
    full = lambda i, j: (0, 0)
    return pl.pallas_call(
        functools.partial(_cmp_kernel, nc=nc),
        grid=(b, NG),
        in_specs=[
            pl.BlockSpec((1, 1, nc, half), lambda i, j: (j, i, 0, 0)),
            pl.BlockSpec((1, 1, nc, half), lambda i, j: (NG + j, i, 0, 0)),
            pl.BlockSpec((1, 1, 8, half), lambda i, j: (j, i, 0, 0)),
            pl.BlockSpec((1, 1, 8, half), lambda i, j: (NG + j, i, 0, 0)),
            pl.BlockSpec((1, 2 * half), full),
            pl.BlockSpec((1, 2 * half), full),
            pl.BlockSpec((2 * half, CMP_HIDDEN), full),
            pl.BlockSpec((CMP_HIDDEN, KA), full),
            pl.BlockSpec((2 * half, CMP_HIDDEN), full),
            pl.BlockSpec((DH, CMP_HIDDEN), full),
            pl.BlockSpec((1, KA), full),
        ],
        out_specs=[
            pl.BlockSpec((1, 1, nc, KA), lambda i, j: (i, j, 0, 0)),
            pl.BlockSpec((1, 1, DH, nc), lambda i, j: (i, j, 0, 0)),
        ],
        out_shape=[
            jax.ShapeDtypeStruct((b, NG, nc, KA), BF16),
            jax.ShapeDtypeStruct((b, NG, DH, nc), BF16),
        ],
        compiler_params=_params(2),
        name="compress",
    )(cz, cz, last, last, pk, pv, w1k, w2k, w1v, w2vT, kg)


def _nsa_kernel(q_ref, qn_ref, kc_ref, vcT_ref, ks_ref, vsT_ref, kw_ref, vwT_ref, gate_ref, out_ref,
                qz_ref, qw_ref, qs0_ref, qs1_ref, qs2_ref, qc_ref, sc_ref, ps_ref, score_ref, sel_ref, seln_ref,
                ms_ref, as_ref, sw_ref, aw_ref, sd_ref, oc_ref, ocn_ref,
                s0_ref, s1_ref, s2_ref, mx0_ref, mx1_ref, mx2_ref,
                *, nc, ns, n_sel):
    c = pl.program_id(2)
    ratio = SLC_BLOCK // CMP_STRIDE
    W = HG * QB
    nwt = WINDOW // QB

    def heads_on_lanes(ref):
        return jnp.concatenate([ref[h * DH:(h + 1) * DH, :] for h in range(HG)], axis=1)

    def set_bias(ref, rows):
        b = jnp.concatenate([rows] * HG, axis=1)
        ref[DH:DH + BIAS_ROWS, :] = jnp.concatenate(
            [b, jnp.zeros((BIAS_ROWS - NB, W), F32)], axis=0).astype(BF16)

    def compressed_scores(ct, qt):
        grp = lax.broadcasted_iota(jnp.int32, (KA - DH, W), 0)
        qc_ref[0:DH, :] = qt
        qc_ref[DH:KA, :] = jnp.where(grp <= ct, 0.0, MASK_VALUE).astype(BF16)
        sc_ref[...] = _dot(kc_ref[0, 0], qc_ref[...])

    def select_blocks(ct):
        r0 = pl.multiple_of(jnp.maximum(ct - 1, 0) * 8, 8)
        row = r0 + lax.broadcasted_iota(jnp.int32, (16, W), 0)
        tcol = ct * QB + (lax.broadcasted_iota(jnp.int32, (16, W), 1) & (QB - 1))
        edge = sc_ref[pl.ds(r0, 16), :]
        sc_ref[pl.ds(r0, 16), :] = jnp.where(row * CMP_STRIDE + (CMP_BLOCK - 1) <= tcol, edge, MASK_VALUE)
        s = sc_ref[...]
        e = jnp.exp2(s - jnp.max(s, axis=0, keepdims=True))
        t1 = ct * QB + (lax.broadcasted_iota(jnp.int32, (1, W), 1) & (QB - 1))
        inv = jnp.where(t1 >= CMP_BLOCK - 1, 1.0 / jnp.maximum(jnp.sum(e, axis=0, keepdims=True), 1e-30), 0.0)
        ocn_ref[...] = _dot(vcT_ref[0, 0], e.astype(BF16)) * inv

        psum = e[:, 0:QB] * inv[:, 0:QB]
        for h in range(1, HG):
            psum = psum + e[:, h * QB:(h + 1) * QB] * inv[:, h * QB:(h + 1) * QB]
        ps_ref[0:8, :] = jnp.zeros((8, QB), F32)
        ps_ref[8:8 + nc, :] = psum
        imp = ps_ref[pl.ds(8, ns, stride=ratio), :]
        for r in range(1, ratio):
            imp = imp + ps_ref[pl.ds(8 + r, ns, stride=ratio), :]
        imp = imp + ps_ref[pl.ds(7, ns, stride=ratio), :]
        blk = lax.broadcasted_iota(jnp.int32, (ns, QB), 0)
        tq = ct * QB + lax.broadcasted_iota(jnp.int32, (ns, QB), 1)
        cur = tq // SLC_BLOCK
        score = jnp.where(blk == cur, FORCE_SCORE, imp)
        score = jnp.where(blk == cur - 1, 2 * FORCE_SCORE, score)
        score = jnp.where(blk == 0, 3 * FORCE_SCORE, score)
        score = jnp.where(blk > cur, MASK_VALUE, score)
        score_ref[...] = score

        before_diag = blk < 2 * ct
        work = jnp.where((blk == 0) | (blk == cur) | (blk == cur - 1), -jnp.inf, score)
        thr = None
        for _ in range(n_sel - 3):
            thr = jnp.max(work, axis=0, keepdims=True)
            work = jnp.where(work == thr, -jnp.inf, work)
        fast = score >= thr
        cnt = jnp.sum(jnp.where(fast, 1.0, 0.0), axis=0, keepdims=True)
        n_tied = jnp.sum(jnp.where(cnt != float(n_sel), 1.0, 0.0))
        all_selected = 2 * ct + 2 <= n_sel
        seln_ref[...] = jnp.where(before_diag & (fast | all_selected), 1.0, 0.0)
        return jnp.logical_and(jnp.logical_not(all_selected), n_tied > 0.0)

    def rank_blocks(ct):
        score = score_ref[...]
        blk = lax.broadcasted_iota(jnp.int32, (ns, QB), 0)

        def rank_body(jp, rank):
            other = jnp.broadcast_to(score_ref[pl.ds(jp, 1), :], (ns, QB))
            beats = (other > score) | ((other == score) & (jp < blk))
            return rank + jnp.where(beats, 1.0, 0.0)

        rank = lax.fori_loop(0, jnp.minimum(2 * ct + 2, ns), rank_body, jnp.zeros((ns, QB), F32))
        seln_ref[...] = jnp.where((blk < 2 * ct) & (rank < float(n_sel)), 1.0, 0.0)

    @pl.when(c == 0)
    def _():
        compressed_scores(c, heads_on_lanes(q_ref))
        pl.when(select_blocks(c))(lambda: rank_blocks(c))

    sel_ref[...] = seln_ref[...]
    oc_ref[...] = ocn_ref[...]
    q = heads_on_lanes(q_ref)
    qz_ref[...] = jnp.concatenate([q, jnp.zeros((KA - DH, W), BF16)], axis=0)
    for r in (qw_ref, qs0_ref, qs1_ref, qs2_ref):
        r[0:DH, :] = q
        r[DH + BIAS_ROWS:KA, :] = jnp.zeros((KA - DH - BIAS_ROWS, W), BF16)

    def gate_rows(br):
        return jnp.concatenate(
            [jnp.broadcast_to(gate_ref[br * HG + h:br * HG + h + 1, :], (DH, QB)) for h in range(HG)],
            axis=1)

    rr = lax.broadcasted_iota(jnp.int32, (QB, QB), 0)
    cc = lax.broadcasted_iota(jnp.int32, (QB, QB), 1)

    def mask_first_tile(s, emask):
        top = jnp.concatenate(
            [jnp.where(emask, s[0:QB, h * QB:(h + 1) * QB], MASK_VALUE) for h in range(HG)], axis=1)
        return top if s.shape[0] == QB else jnp.concatenate([top, s[QB:, :]], axis=0)

    def normalised(a_ref):
        a = a_ref[...]
        return a[0:DH, :] * (1.0 / jnp.maximum(a[DH:DH + 1, :], 1e-30))

    last_tile = ns // NB - 1

    def scores(qs_ref, s_ref, mx_ref, t4):
        t4 = jnp.minimum(t4, last_tile)
        rows = sel_ref[pl.ds(pl.multiple_of(t4 * NB, NB), NB), :]
        set_bias(qs_ref, jnp.where(rows > 0.5, 0.0, MASK_VALUE))
        s = _dot(ks_ref[0,pl.ds(pl.multiple_of(t4 * KT, KT), KT), :], qs_ref[...])
        s_ref[...] = s
        mx_ref[...] = jnp.max(s, axis=0, keepdims=True)

    def accumulate(s_ref, mx_ref, t4):
        m_old = ms_ref[...]
        m_new = jnp.maximum(m_old, mx_ref[...])
        pt = jnp.exp2(s_ref[...] - m_new).astype(BF16)
        vT = jnp.concatenate([vsT_ref[0,(KT // QB) * t4 + i] for i in range(KT // QB)], axis=1)
        as_ref[...] = jnp.exp2(m_old - m_new) * as_ref[...] + _dot(vT, pt)
        ms_ref[...] = m_new

    nxt = jnp.minimum(c + 1, pl.num_programs(2) - 1)
    compressed_scores(nxt, heads_on_lanes(qn_ref))

    scores(qs0_ref, s0_ref, mx0_ref, 0)
    scores(qs1_ref, s1_ref, mx1_ref, 1)

    diag = pl.ds(pl.multiple_of(c * QB, QB), QB)
    qz = qz_ref[...]
    kt0 = jnp.maximum(c - nwt, 0)
    b8 = lax.broadcasted_iota(jnp.int32, (NB, QB), 0)
    tile_of_col = ((b8 - 2 * kt0) & (NB - 1)) >> 1
    set_bias(qw_ref, jnp.where(tile_of_col < jnp.minimum(c, nwt), 0.0, MASK_VALUE))
    kwc = kw_ref[0, pl.ds(pl.multiple_of(kt0 * QB, QB), nwt * QB), :]
    sw_ref[0:nwt * QB, :] = mask_first_tile(_dot(kwc, qw_ref[...]), (rr > cc) | (c < nwt))
    sw_ref[nwt * QB:, :] = mask_first_tile(_dot(kw_ref[0, diag, :], qz), rr <= cc)
    sd_ref[...] = mask_first_tile(_dot(ks_ref[0, diag, :], qz), rr <= cc)

    sw = sw_ref[...]
    pw = jnp.exp2(sw - jnp.max(sw, axis=0, keepdims=True)).astype(BF16)
    vw = jnp.concatenate([vwT_ref[0,kt0 + i] for i in range(nwt)] + [vwT_ref[0,c]], axis=1)
    aw_ref[...] = _dot(vw, pw)
    sd = sd_ref[...]
    md = jnp.max(sd, axis=0, keepdims=True)
    ms_ref[...] = md
    as_ref[...] = _dot(vsT_ref[0,c], jnp.exp2(sd - md).astype(BF16))
    pl.when(select_blocks(nxt))(lambda: rank_blocks(nxt))

    def sel_body(j, carry):
        t = 3 * j
        scores(qs2_ref, s2_ref, mx2_ref, t + 2)
        accumulate(s0_ref, mx0_ref, t)
        scores(qs0_ref, s0_ref, mx0_ref, t + 3)
        accumulate(s1_ref, mx1_ref, t + 1)
        scores(qs1_ref, s1_ref, mx1_ref, t + 4)
        accumulate(s2_ref, mx2_ref, t + 2)
        return carry

    n_tiles = (c * QB + KT - 1) // KT
    trips = jnp.maximum(n_tiles - 1, 0) // 3
    lax.fori_loop(0, trips, sel_body, 0)
    left = n_tiles - 3 * trips
    t_left = 3 * trips

    @pl.when(left == 1)
    def _():
        accumulate(s0_ref, mx0_ref, t_left)

    @pl.when(left == 2)
    def _():
        accumulate(s0_ref, mx0_ref, t_left)
        accumulate(s1_ref, mx1_ref, t_left + 1)

    @pl.when(left == 3)
    def _():
        scores(qs2_ref, s2_ref, mx2_ref, t_left + 2)
        accumulate(s0_ref, mx0_ref, t_left)
        accumulate(s1_ref, mx1_ref, t_left + 1)
        accumulate(s2_ref, mx2_ref, t_left + 2)

    o = gate_rows(0) * oc_ref[...] + gate_rows(1) * normalised(as_ref) + gate_rows(2) * normalised(aw_ref)
    for hp in range(HG // 2):
        pair = jnp.concatenate([o[:, (2 * hp) * QB:(2 * hp + 1) * QB],
                                o[:, (2 * hp + 1) * QB:(2 * hp + 2) * QB]], axis=0)
        out_ref[:, hp * 2 * DH:(hp + 1) * 2 * DH] = pair.T.astype(BF16)


def _nsa(qT, kc, vcT, ks, vsT, kw, vwT, gates, *, b, s):
    assert s % (2 * KT) == 0 and s >= WINDOW + QB and s // CMP_STRIDE // 8 <= KA - DH
    nq = s // QB
    nc = s // CMP_STRIDE
    ns = s // SLC_BLOCK
    n_sel = min(N_SELECT, ns)
    W = HG * QB
    bg = lambda i, j, c: (i, j, 0, 0)
    keys = pl.BlockSpec((1, s, KA), lambda i, j, c: (j, i, 0))
    values = pl.BlockSpec((1, nq, VROWS, QB), lambda i, j, c: (j, i, 0, 0))
    qa = pltpu.VMEM((KA, W), BF16)
    return pl.pallas_call(
        functools.partial(_nsa_kernel, nc=nc, ns=ns, n_sel=n_sel),
        grid=(b, NG, nq),
        in_specs=[
            pl.BlockSpec((HG * DH, QB), lambda i, j, c: (j, i * nq + c)),
            pl.BlockSpec((HG * DH, QB), lambda i, j, c: (j, i * nq + jnp.minimum(c + 1, nq - 1))),
            pl.BlockSpec((1, 1, nc, KA), bg),
            pl.BlockSpec((1, 1, DH, nc), bg),
            keys, values, keys, values,
            pl.BlockSpec((GATE_ROWS // NG, QB), lambda i, j, c: (j, i * nq + c)),
        ],
        out_specs=pl.BlockSpec((QB, HG * DH), lambda i, j, c: (i * nq + c, j)),
        out_shape=jax.ShapeDtypeStruct((b * s, NSA_Q_W), BF16),
        scratch_shapes=[
            qa, qa, qa, qa, qa, qa,
            pltpu.VMEM((nc, W), F32),
            pltpu.VMEM((8 + nc, QB), F32),
            pltpu.VMEM((ns, QB), F32),
            pltpu.VMEM((ns, QB), F32),
            pltpu.VMEM((ns, QB), F32),
            pltpu.VMEM((1, W), F32),
            pltpu.VMEM((VROWS, W), F32),
            pltpu.VMEM((WINDOW + QB, W), F32),
            pltpu.VMEM((VROWS, W), F32),
            pltpu.VMEM((QB, W), F32),
            pltpu.VMEM((DH, W), F32),
            pltpu.VMEM((DH, W), F32),
            pltpu.VMEM((KT, W), F32),
            pltpu.VMEM((KT, W), F32),
            pltpu.VMEM((KT, W), F32),
            pltpu.VMEM((1, W), F32),
            pltpu.VMEM((1, W), F32),
            pltpu.VMEM((1, W), F32),
        ],
        compiler_params=_params(3),
        name="nsa",
    )(qT, qT, kc, vcT, ks, vsT, kw, vwT, gates)


def _ret_kernel(q_ref, k_ref, v_ref, g_ref, cos_ref, sin_ref, decay_ref, xi_ref, zeta_ref, gch_ref,
                o_ref, r_ref, *, nchunk):
    @pl.when(pl.program_id(1) == 0)
    def _():
        r_ref[...] = jnp.zeros((RET_HEADS, DK, DK), F32)

    for ci in range(nchunk):
        rows = slice(ci * RET_CHUNK, (ci + 1) * RET_CHUNK)
        cosv = cos_ref[rows, :]
        sinv = sin_ref[rows, :]

        def rope(x):
            return x * cosv + pltpu.roll(x, DK // 2, 1) * sinv

        for h in range(RET_HEADS):
            cols = slice(h * DK, (h + 1) * DK)
            q = rope(q_ref[rows, cols])
            k = rope(k_ref[rows, cols]) * (DK ** -0.5)
            v = v_ref[rows, cols].astype(BF16)
            r_prev = r_ref[h]
            inner = _dot_nt(q.astype(BF16), k.astype(BF16)) * decay_ref[h]
            y = _dot(inner.astype(BF16), v) + _dot((q * xi_ref[h]).astype(BF16), r_prev.astype(BF16))
            y = y * lax.rsqrt(jnp.mean(y * y, axis=-1, keepdims=True) + EPS)
            g = g_ref[rows, cols]
            o_ref[rows, cols] = (g * jax.nn.sigmoid(g) * y).astype(BF16)
            r_ref[h] = gch_ref[h, 0:1, :] * r_prev + _dot_tn((k * zeta_ref[h]).astype(BF16), v)


def _retention(zr, zv, cosf, sinf, decay, xi, zeta, gch, *, b, s, ts):
    nst = s // ts
    col = lambda blk: (lambda i, j: (i * nst + j, blk))
    tab = lambda i, j: (j, 0)
    whole = lambda i, j: (0, 0, 0)
    return pl.pallas_call(
        functools.partial(_ret_kernel, nchunk=ts // RET_CHUNK),
        grid=(b, nst),
        in_specs=[
            pl.BlockSpec((ts, RET_W), col(0)),
            pl.BlockSpec((ts, RET_W), col(1)),
            pl.BlockSpec((ts, RET_W), col(0)),
            pl.BlockSpec((ts, RET_W), col(2)),
            pl.BlockSpec((ts, DK), tab),
            pl.BlockSpec((ts, DK), tab),
            pl.BlockSpec((RET_HEADS, RET_CHUNK, RET_CHUNK), whole),
            pl.BlockSpec((RET_HEADS, RET_CHUNK, DK), whole),
            pl.BlockSpec((RET_HEADS, RET_CHUNK, DK), whole),
            pl.BlockSpec((RET_HEADS, 8, DK), whole),
        ],
        out_specs=pl.BlockSpec((ts, RET_W), lambda i, j: (i * nst + j, 0)),
        out_shape=jax.ShapeDtypeStruct((b * s, RET_W), BF16),
        scratch_shapes=[pltpu.VMEM((RET_HEADS, DK, DK), F32)],
        compiler_params=_params(2),
        name="retention",
    )(zr, zr, zv, zr, cosf, sinf, decay, xi, zeta, gch)


def _mix_kernel(x_ref, ya_ref, yb_ref, ma_ref, mb_ref, wa_ref, wb_ref, wo_ref, o_ref):
    ua = _dot(ya_ref[...], wa_ref[...])
    ub = _dot(yb_ref[...], wb_ref[...])
    mix = ma_ref[...].astype(F32) * ua + mb_ref[...].astype(F32) * ub
    o_ref[...] = x_ref[...] + _dot(mix.astype(BF16), wo_ref[...])


def _mix(x2, ya, yb, zm, wa, wb, wo, *, tm):
    n = x2.shape[0]
    full = lambda i: (0, 0)
    return pl.pallas_call(
        _mix_kernel,
        grid=(n // tm,),
        in_specs=[
            pl.BlockSpec((tm, D_MODEL), lambda i: (i, 0)),
            pl.BlockSpec((tm, NSA_Q_W), lambda i: (i, 0)),
            pl.BlockSpec((tm, RET_W), lambda i: (i, 0)),
            pl.BlockSpec((tm, D_MODEL), lambda i: (i, 0)),
            pl.BlockSpec((tm, D_MODEL), lambda i: (i, 1)),
            pl.BlockSpec((NSA_Q_W, D_MODEL), full),
            pl.BlockSpec((RET_W, D_MODEL), full),
            pl.BlockSpec((D_MODEL, D_MODEL), full),
        ],
        out_specs=pl.BlockSpec((tm, D_MODEL), lambda i: (i, 0)),
        out_shape=jax.ShapeDtypeStruct((n, D_MODEL), F32),
        compiler_params=_params(1),
        name="mix",
    )(x2, ya, yb, zm, zm, wa, wb, wo)


def _ffn_kernel(x_ref, p_ref, gm_ref, w1_ref, w2_ref, gp_ref, wg_ref, wp_ref, o_ref, *, ff_chunk):
    x = x_ref[...]
    h = _rms_rows(x, gm_ref[...]).astype(BF16)
    for j in range(D_FF // ff_chunk):
        cols = slice(j * ff_chunk, (j + 1) * ff_chunk)
        u = jnp.square(jnp.maximum(_dot(h, w1_ref[:, cols]), 0.0)).astype(BF16)
        x = x + _dot(u, w2_ref[cols, :])
    gate = jax.nn.sigmoid(_dot(_rms_rows(x, gp_ref[...]).astype(BF16), wg_ref[...]))
    o_ref[...] = x + gate * _dot(p_ref[...].astype(BF16), wp_ref[...])


def _ffn(x2, p2, gm, w1, w2, gp, wg, wp, *, tm):
    n = x2.shape[0]
    full = lambda i: (0, 0)
    once = pl.Buffered(1)
    return pl.pallas_call(
        functools.partial(_ffn_kernel, ff_chunk=1024),
        grid=(n // tm,),
        in_specs=[
            pl.BlockSpec((tm, D_MODEL), lambda i: (i, 0)),
            pl.BlockSpec((tm, PLE_DIM), lambda i: (i, 0)),
            pl.BlockSpec((1, D_MODEL), full),
            pl.BlockSpec((D_MODEL, D_FF), full, pipeline_mode=once),
            pl.BlockSpec((D_FF, D_MODEL), full, pipeline_mode=once),
            pl.BlockSpec((1, D_MODEL), full),
            pl.BlockSpec((D_MODEL, D_MODEL), full, pipeline_mode=once),
            pl.BlockSpec((PLE_DIM, D_MODEL), full, pipeline_mode=once),
        ],
        out_specs=pl.BlockSpec((tm, D_MODEL), lambda i: (i, 0)),
        out_shape=jax.ShapeDtypeStruct((n, D_MODEL), F32),
        compiler_params=_params(1),
        name="ffn",
    )(x2, p2, gm, w1, w2, gp, wg, wp)


def _retention_tables(s):
    half = DK // 2
    pos = jnp.arange(s, dtype=F32)
    inv = ROPE_BASE ** (-jnp.arange(half, dtype=F32) / half)
    ang = pos[:, None] * inv[None, :]
    cos, sin = jnp.cos(ang), jnp.sin(ang)
    cosf = jnp.concatenate([cos, cos], axis=-1)
    sinf = jnp.concatenate([-sin, sin], axis=-1)
    C = RET_CHUNK
    gamma = 1.0 - 2.0 ** (-5.0 - jnp.arange(RET_HEADS, dtype=F32))
    lg = jnp.log(gamma)
    n = jnp.arange(C, dtype=F32)
    diff = n[:, None] - n[None, :]
    decay = jnp.where(diff >= 0, jnp.exp(lg[:, None, None] * jnp.maximum(diff, 0.0)), 0.0)
    xi = jnp.broadcast_to(jnp.exp(lg[:, None] * (n + 1.0))[:, :, None], (RET_HEADS, C, DK))
    zeta = jnp.broadcast_to(jnp.exp(lg[:, None] * (C - 1.0 - n))[:, :, None], (RET_HEADS, C, DK))
    gch = jnp.broadcast_to(jnp.exp(lg * C)[:, None, None], (RET_HEADS, 8, DK))
    return cosf, sinf, decay, xi, zeta, gch


def _layer(x2, p2, w, tabs, *, b, s):
    nc = s // CMP_STRIDE
    zr, zv, zm, qT, cz, ksa, kwa, vsa, vwa, gT = _proj(
        x2, w["norm_mix"], w["wn"], w["wt"], w["qg"], w["kgc"], tm=256)
    cz = cz.reshape(2 * NG, b, s, DH)
    last = jnp.tile(cz[:, :, s - 1:s, :], (1, 1, 8, CMP_STRIDE))
    kc, vcT = _compress(cz.reshape(2 * NG, b, nc, CMP_STRIDE * DH), last, w["pos_k"], w["pos_v"],
                        w["w1k"], w["w2k"], w["w1v"], w["w2vT"], w["kgr"], nc=nc)
    ya = _nsa(qT, kc, vcT, ksa, vsa, kwa, vwa, gT, b=b, s=s)
    yb = _retention(zr, zv, *tabs, b=b, s=s, ts=min(s, 1024))
    x2 = _mix(x2, ya, yb, zm, w["wa"], w["wb"], w["wo"], tm=512)
    return _ffn(x2, p2, w["norm_mlp"], w["w1"], w["w2"], w["norm_ple"], w["wg"], w["wp"], tm=512)


def kernel(x, p, norm_mix, w_in, nsa_q_norm, nsa_k_norm, cmp_pos_k, cmp_pos_v, cmp_w1_k, cmp_w2_k,
           cmp_w1_v, cmp_w2_v, w_up_nsa, w_up_ret, w_out, norm_mlp, w_ff1, w_ff2, norm_ple, w_ple,
           w_ple_gate):
    b, s, _ = x.shape
    depth = w_in.shape[0]
    tabs = _retention_tables(s)
    x2 = x.reshape(b * s, D_MODEL)
    q_end = NSA_Q_W
    kv = NSA_KV_W
    g_end = NSA_Q_W + 6 * kv + NSA_GATE_W
    for i in range(depth):
        wi = w_in[i]
        wg = wi[:, q_end + 6 * kv:g_end].reshape(D_MODEL, 3, NG, HG).transpose(0, 2, 1, 3)
        wg = jnp.pad(wg.reshape(D_MODEL, NG, 3 * HG), ((0, 0), (0, 0), (0, GATE_ROWS // NG - 3 * HG)))
        wt = jnp.concatenate([
            wi[:, 0:q_end + 2 * kv],
            wi[:, q_end + 2 * kv:q_end + 3 * kv], wi[:, q_end + 4 * kv:q_end + 5 * kv],
            wi[:, q_end + 3 * kv:q_end + 4 * kv], wi[:, q_end + 5 * kv:q_end + 6 * kv],
            wg.reshape(D_MODEL, GATE_ROWS)], axis=1).T.astype(BF16)
        w = dict(
            norm_mix=norm_mix[i][None, :], wt=wt,
            wn=jnp.concatenate([wi[:, g_end:g_end + 2 * RET_W], wi[:, g_end + 3 * RET_W:g_end + 4 * RET_W],
                                wi[:, g_end + 2 * RET_W:g_end + 3 * RET_W], wi[:, g_end + 4 * RET_W:]],
                               axis=1).astype(BF16),
            qg=nsa_q_norm[i][:, None], kgc=nsa_k_norm[i][:, None],
            kgr=jnp.pad(nsa_k_norm[i][None, :], ((0, 0), (0, KA - DH))),
            pos_k=cmp_pos_k[i].reshape(1, CMP_BLOCK * DH), pos_v=cmp_pos_v[i].reshape(1, CMP_BLOCK * DH),
            w1k=cmp_w1_k[i].astype(BF16), w2k=jnp.pad(cmp_w2_k[i], ((0, 0), (0, KA - DH))).astype(BF16),
            w1v=cmp_w1_v[i].astype(BF16), w2vT=cmp_w2_v[i].T.astype(BF16),
            wa=w_up_nsa[i].astype(BF16), wb=w_up_ret[i].astype(BF16), wo=w_out[i].astype(BF16),
            norm_mlp=norm_mlp[i][None, :], w1=w_ff1[i].astype(BF16), w2=w_ff2[i].astype(BF16),
            norm_ple=norm_ple[i][None, :], wg=w_ple_gate[i].astype(BF16), wp=w_ple[i].astype(BF16),
        )
        x2 = _layer(x2, p[i].reshape(b * s, PLE_DIM), w, tabs, b=b, s=s)
    return x2.reshape(b, s, D_MODEL)
```

```python
import functools
import math

import jax
import jax.numpy as jnp
from jax import lax
from jax.experimental import pallas as pl
from jax.experimental.pallas import tpu as pltpu

D_MODEL = 1024
PLE_DIM = 256
NSA_HEADS = 8
DH = 64
NG = 2
HG = NSA_HEADS // NG
CMP_BLOCK = 32
CMP_STRIDE = 16
CMP_HIDDEN = 4 * DH
SLC_BLOCK = 64
N_SELECT = 16
WINDOW = 512
QB = 128
RET_HEADS = 4
DK = 128
RET_CHUNK = 128
ROPE_BASE = 10000.0
D_FF = 4 * D_MODEL
EPS = 1e-6
MASK_VALUE = -1e30
FORCE_SCORE = 1e4
LOG2E = math.log2(math.e)

NSA_Q_W = NSA_HEADS * DH
NSA_KV_W = NG * DH
NSA_GATE_W = 3 * NSA_HEADS
RET_W = RET_HEADS * DK
NAT_W = 4 * RET_W + 2 * D_MODEL
GATE_ROWS = 32
T_ROWS = NSA_Q_W + 6 * NSA_KV_W + GATE_ROWS

KT = 512
NB = KT // SLC_BLOCK
KA = 128
VROWS = DH + 16
BIAS_ROWS = 16

VMEM_LIMIT = 56 * 1024 * 1024

F32 = jnp.float32
BF16 = jnp.bfloat16


def _dot(a, b):
    return jnp.dot(a, b, preferred_element_type=F32)


def _dot_nt(a, b):
    return lax.dot_general(a, b, (((1,), (1,)), ((), ())), preferred_element_type=F32)


def _dot_tn(a, b):
    return lax.dot_general(a, b, (((0,), (0,)), ((), ())), preferred_element_type=F32)


def _rms_rows(x, g):
    y = x * lax.rsqrt(jnp.mean(x * x, axis=-1, keepdims=True) + EPS)
    return y * g


def _params(n_axes):
    return pltpu.CompilerParams(dimension_semantics=("arbitrary",) * n_axes,
                                vmem_limit_bytes=VMEM_LIMIT)


def _proj_kernel(x_ref, g_ref, wn_ref, wt_ref, qg_ref, kg_ref,
                 zr_ref, zv_ref, zm_ref, qT_ref, cz_ref, ksa_ref, kwa_ref, vsa_ref, vwa_ref, gT_ref):
    h = _rms_rows(x_ref[...], g_ref[...]).astype(BF16)
    zn = _dot(h, wn_ref[...])
    zr_ref[...] = zn[:, 0:3 * RET_W]
    zv_ref[...] = zn[:, 3 * RET_W:4 * RET_W].astype(BF16)
    zm_ref[...] = jax.nn.sigmoid(zn[:, 4 * RET_W:]).astype(BF16)
    zt = _dot_nt(wt_ref[...], h)
    qg = qg_ref[...]
    kg = kg_ref[...]

    def norm_cols(z, g):
        return z * lax.rsqrt(jnp.mean(z * z, axis=0, keepdims=True) + EPS) * g

    for hh in range(NSA_HEADS):
        rows = slice(hh * DH, (hh + 1) * DH)
        qT_ref[rows, :] = (norm_cols(zt[rows, :], qg) * (DH ** -0.5 * LOG2E)).astype(BF16)
    tm = zt.shape[1]
    zpad = jnp.zeros((KA - DH, tm), F32)
    o = NSA_Q_W
    for j in range(2 * NG):
        rows = slice(o + j * DH, o + (j + 1) * DH)
        cz_ref[j] = jnp.concatenate([zt[rows, :], zpad], axis=0).T[:, 0:DH]
    o += 2 * NSA_KV_W
    lane_row = lax.broadcasted_iota(jnp.int32, (KA - DH, tm), 0)
    pos = pl.program_id(0) * tm + lax.broadcasted_iota(jnp.int32, (KA - DH, tm), 1)
    ind = jnp.where(lane_row == ((pos // SLC_BLOCK) & (NB - 1)), 1.0, 0.0)
    for j, ref in enumerate((ksa_ref, ksa_ref, kwa_ref, kwa_ref)):
        rows = slice(o + j * DH, o + (j + 1) * DH)
        ref[j % NG] = jnp.concatenate([norm_cols(zt[rows, :], kg), ind], axis=0).T.astype(BF16)
    o += 2 * NSA_KV_W
    tail_row = lax.broadcasted_iota(jnp.int32, (VROWS - DH, tm), 0)
    vtail = jnp.where(tail_row == 0, 1.0, 0.0)
    for j, ref in enumerate((vsa_ref, vsa_ref, vwa_ref, vwa_ref)):
        rows = slice(o + j * DH, o + (j + 1) * DH)
        va = jnp.concatenate([zt[rows, :], vtail], axis=0).astype(BF16)
        for u in range(tm // QB):
            ref[j % NG, u] = va[:, u * QB:(u + 1) * QB]
    o += 2 * NSA_KV_W
    gT_ref[...] = jax.nn.sigmoid(zt[o:o + GATE_ROWS, :])


def _proj(x2, g, wn, wt, qg, kg, *, tm):
    n = x2.shape[0]
    full = lambda i: (0, 0)
    colblk = lambda i: (0, i)
    rowblk3 = lambda i: (0, i, 0)
    rowblk4 = lambda i: (0, i, 0, 0)
    return pl.pallas_call(
        _proj_kernel,
        grid=(n // tm,),
        in_specs=[
            pl.BlockSpec((tm, D_MODEL), lambda i: (i, 0)),
            pl.BlockSpec((1, D_MODEL), full),
            pl.BlockSpec((D_MODEL, NAT_W), full),
            pl.BlockSpec((T_ROWS, D_MODEL), full),
            pl.BlockSpec((DH, 1), full),
            pl.BlockSpec((DH, 1), full),
        ],
        out_specs=[
            pl.BlockSpec((tm, 3 * RET_W), lambda i: (i, 0)),
            pl.BlockSpec((tm, RET_W), lambda i: (i, 0)),
            pl.BlockSpec((tm, 2 * D_MODEL), lambda i: (i, 0)),
            pl.BlockSpec((NSA_Q_W, tm), colblk),
            pl.BlockSpec((2 * NG, tm, DH), rowblk3),
            pl.BlockSpec((NG, tm, KA), rowblk3),
            pl.BlockSpec((NG, tm, KA), rowblk3),
            pl.BlockSpec((NG, tm // QB, VROWS, QB), rowblk4),
            pl.BlockSpec((NG, tm // QB, VROWS, QB), rowblk4),
            pl.BlockSpec((GATE_ROWS, tm), colblk),
        ],
        out_shape=[
            jax.ShapeDtypeStruct((n, 3 * RET_W), F32),
            jax.ShapeDtypeStruct((n, RET_W), BF16),
            jax.ShapeDtypeStruct((n, 2 * D_MODEL), BF16),
            jax.ShapeDtypeStruct((NSA_Q_W, n), BF16),
            jax.ShapeDtypeStruct((2 * NG, n, DH), F32),
            jax.ShapeDtypeStruct((NG, n, KA), BF16),
            jax.ShapeDtypeStruct((NG, n, KA), BF16),
            jax.ShapeDtypeStruct((NG, n // QB, VROWS, QB), BF16),
            jax.ShapeDtypeStruct((NG, n // QB, VROWS, QB), BF16),
            jax.ShapeDtypeStruct((GATE_ROWS, n), F32),
        ],
        compiler_params=_params(1),
        name="proj",
    )(x2, g, wn, wt, qg, kg)


def _gelu_tanh(x):
    cdf = 0.5 * (1.0 + jnp.tanh(math.sqrt(2.0 / math.pi) * (x + 0.044715 * (x ** 3))))
    return x * cdf


def _cmp_kernel(zk_ref, zv_ref, lk_ref, lv_ref, pk_ref, pv_ref, w1k_ref, w2k_ref, w1v_ref, w2vT_ref, kg_ref,
                kc_ref, vcT_ref, *, nc):
    half = CMP_STRIDE * DH
    last_row = lax.broadcasted_iota(jnp.int32, (nc, CMP_HIDDEN), 0) == nc - 1

    def hidden(z_ref, l_ref, p_ref, w1_ref):
        z = z_ref[0, 0]
        first = _dot((z + p_ref[:, 0:half]).astype(BF16), w1_ref[0:half, :])
        second = _dot((z + p_ref[:, half:2 * half]).astype(BF16), w1_ref[half:2 * half, :])
        clamp = _dot((l_ref[0, 0] + p_ref[:, half:2 * half]).astype(BF16), w1_ref[half:2 * half, :])
        second = jnp.where(last_row, clamp[0:1, :], pltpu.roll(second, nc - 1, 0))
        return _gelu_tanh(first + second)

    hk = hidden(zk_ref, lk_ref, pk_ref, w1k_ref).astype(BF16)
    k = _dot(hk, w2k_ref[...])
    k = k * lax.rsqrt(jnp.sum(k * k, axis=-1, keepdims=True) * (1.0 / DH) + EPS) * kg_ref[...]
    row = lax.broadcasted_iota(jnp.int32, (nc, KA), 0)
    lane = lax.broadcasted_iota(jnp.int32, (nc, KA), 1)
    kc_ref[0, 0] = (k + jnp.where(lane - DH == (row >> 3), 1.0, 0.0)).astype(BF16)
    hv = hidden(zv_ref, lv_ref, pv_ref, w1v_ref).astype(BF16)
    vcT_ref[0, 0] = _dot_nt(w2vT_ref[...], hv).astype(BF16)


def _compress(cz, last, pk, pv, w1k, w2k, w1v, w2vT, kg, *, nc):
    b = cz.shape[1]
    half = CMP_STRIDE * DH
    full = lambda i, j: (0, 0)
    return pl.pallas_call(
        functools.partial(_cmp_kernel, nc=nc),
        grid=(b, NG),
        in_specs=[
            pl.BlockSpec((1, 1, nc, half), lambda i, j: (j, i, 0, 0)),
            pl.BlockSpec((1, 1, nc, half), lambda i, j: (NG + j, i, 0, 0)),
            pl.BlockSpec((1, 1, 8, half), lambda i, j: (j, i, 0, 0)),
            pl.BlockSpec((1, 1, 8, half), lambda i, j: (NG + j, i, 0, 0)),
            pl.BlockSpec((1, 2 * half), full),
            pl.BlockSpec((1, 2 * half), full),
            pl.BlockSpec((2 * half, CMP_HIDDEN), full),
            pl.BlockSpec((CMP_HIDDEN, KA), full),
            pl.BlockSpec((2 * half, CMP_HIDDEN), full),
            pl.BlockSpec((DH, CMP_HIDDEN), full),
            pl.BlockSpec((1, KA), full),
        ],
        out_specs=[
            pl.BlockSpec((1, 1, nc, KA), lambda i, j: (i, j, 0, 0)),
            pl.BlockSpec((1, 1, DH, nc), lambda i, j: (i, j, 0, 0)),
        ],
        out_shape=[
            jax.ShapeDtypeStruct((b, NG, nc, KA), BF16),
            jax.ShapeDtypeStruct((b, NG, DH, nc), BF16),
        ],
        compiler_params=_params(2),
        name="compress",
    )(cz, cz, last, last, pk, pv, w1k, w2k, w1v, w2vT, kg)


def _nsa_kernel(q_ref, qn_ref, kc_ref, vcT_ref, ks_ref, vsT_ref, kw_ref, vwT_ref, gate_ref, out_ref,
                qz_ref, qw_ref, qs0_ref, qs1_ref, qs2_ref, qc_ref, sc_ref, ps_ref, score_ref, sel_ref, seln_ref,
                ms_ref, as_ref, sw_ref, aw_ref, sd_ref, oc_ref, ocn_ref,
                s0_ref, s1_ref, s2_ref, mx0_ref, mx1_ref, mx2_ref,
                *, nc, ns, n_sel):
    c = pl.program_id(2)
    ratio = SLC_BLOCK // CMP_STRIDE
    W = HG * QB
    nwt = WINDOW // QB

    def heads_on_lanes(ref):
        return jnp.concatenate([ref[h * DH:(h + 1) * DH, :] for h in range(HG)], axis=1)

    def set_bias(ref, rows):
        b = jnp.concatenate([rows] * HG, axis=1)
        ref[DH:DH + BIAS_ROWS, :] = jnp.concatenate(
            [b, jnp.zeros((BIAS_ROWS - NB, W), F32)], axis=0).astype(BF16)

    def compressed_scores(ct, qt):
        grp = lax.broadcasted_iota(jnp.int32, (KA - DH, W), 0)
        qc_ref[0:DH, :] = qt
        qc_ref[DH:KA, :] = jnp.where(grp <= ct, 0.0, MASK_VALUE).astype(BF16)
        sc_ref[...] = _dot(kc_ref[0, 0], qc_ref[...])

    def select_blocks(ct):
        r0 = pl.multiple_of(jnp.maximum(ct - 1, 0) * 8, 8)
        row = r0 + lax.broadcasted_iota(jnp.int32, (16, W), 0)
        tcol = ct * QB + (lax.broadcasted_iota(jnp.int32, (16, W), 1) & (QB - 1))
        edge = sc_ref[pl.ds(r0, 16), :]
        sc_ref[pl.ds(r0, 16), :] = jnp.where(row * CMP_STRIDE + (CMP_BLOCK - 1) <= tcol, edge, MASK_VALUE)
        s = sc_ref[...]
        e = jnp.exp2(s - jnp.max(s, axis=0, keepdims=True))
        t1 = ct * QB + (lax.broadcasted_iota(jnp.int32, (1, W), 1) & (QB - 1))
        inv = jnp.where(t1 >= CMP_BLOCK - 1, 1.0 / jnp.maximum(jnp.sum(e, axis=0, keepdims=True), 1e-30), 0.0)
        ocn_ref[...] = _dot(vcT_ref[0, 0], e.astype(BF16)) * inv

        psum = e[:, 0:QB] * inv[:, 0:QB]
        for h in range(1, HG):
            psum = psum + e[:, h * QB:(h + 1) * QB] * inv[:, h * QB:(h + 1) * QB]
        ps_ref[0:8, :] = jnp.zeros((8, QB), F32)
        ps_ref[8:8 + nc, :] = psum
        imp = ps_ref[pl.ds(8, ns, stride=ratio), :]
        for r in range(1, ratio):
            imp = imp + ps_ref[pl.ds(8 + r, ns, stride=ratio), :]
        imp = imp + ps_ref[pl.ds(7, ns, stride=ratio), :]
        blk = lax.broadcasted_iota(jnp.int32, (ns, QB), 0)
        tq = ct * QB + lax.broadcasted_iota(jnp.int32, (ns, QB), 1)
        cur = tq // SLC_BLOCK
        score = jnp.where(blk == cur, FORCE_SCORE, imp)
        score = jnp.where(blk == cur - 1, 2 * FORCE_SCORE, score)
        score = jnp.where(blk == 0, 3 * FORCE_SCORE, score)
        score = jnp.where(blk > cur, MASK_VALUE, score)
        score_ref[...] = score

        before_diag = blk < 2 * ct
        work = jnp.where((blk == 0) | (blk == cur) | (blk == cur - 1), -jnp.inf, score)
        thr = None
        for _ in range(n_sel - 3):
            thr = jnp.max(work, axis=0, keepdims=True)
            work = jnp.where(work == thr, -jnp.inf, work)
        fast = score >= thr
        cnt = jnp.sum(jnp.where(fast, 1.0, 0.0), axis=0, keepdims=True)
        n_tied = jnp.sum(jnp.where(cnt != float(n_sel), 1.0, 0.0))
        all_selected = 2 * ct + 2 <= n_sel
        seln_ref[...] = jnp.where(before_diag & (fast | all_selected), 1.0, 0.0)
        return jnp.logical_and(jnp.logical_not(all_selected), n_tied > 0.0)

    def rank_blocks(ct):
        score = score_ref[...]
        blk = lax.broadcasted_iota(jnp.int32, (ns, QB), 0)

        def rank_body(jp, rank):
            other = jnp.broadcast_to(score_ref[pl.ds(jp, 1), :], (ns, QB))
            beats = (other > score) | ((other == score) & (jp < blk))
            return rank + jnp.where(beats, 1.0, 0.0)

        rank = lax.fori_loop(0, jnp.minimum(2 * ct + 2, ns), rank_body, jnp.zeros((ns, QB), F32))
        seln_ref[...] = jnp.where((blk < 2 * ct) & (rank < float(n_sel)), 1.0, 0.0)

    @pl.when(c == 0)
    def _():
        compressed_scores(c, heads_on_lanes(q_ref))
        pl.when(select_blocks(c))(lambda: rank_blocks(c))

    sel_ref[...] = seln_ref[...]
    oc_ref[...] = ocn_ref[...]
    q = heads_on_lanes(q_ref)
    qz_ref[...] = jnp.concatenate([q, jnp.zeros((KA - DH, W), BF16)], axis=0)
    for r in (qw_ref, qs0_ref, qs1_ref, qs2_ref):
        r[0:DH, :] = q
        r[DH + BIAS_ROWS:KA, :] = jnp.zeros((KA - DH - BIAS_ROWS, W), BF16)

    def gate_rows(br):
        return jnp.concatenate(
            [jnp.broadcast_to(gate_ref[br * HG + h:br * HG + h + 1, :], (DH, QB)) for h in range(HG)],
            axis=1)

    rr = lax.broadcasted_iota(jnp.int32, (QB, QB), 0)
    cc = lax.broadcasted_iota(jnp.int32, (QB, QB), 1)

    def mask_first_tile(s, emask):
        top = jnp.concatenate(
            [jnp.where(emask, s[0:QB, h * QB:(h + 1) * QB], MASK_VALUE) for h in range(HG)], axis=1)
        return top if s.shape[0] == QB else jnp.concatenate([top, s[QB:, :]], axis=0)

    def normalised(a_ref):
        a = a_ref[...]
        return a[0:DH, :] * (1.0 / jnp.maximum(a[DH:DH + 1, :], 1e-30))

    last_tile = ns // NB - 1

    def scores(qs_ref, s_ref, mx_ref, t4):
        t4 = jnp.minimum(t4, last_tile)
        rows = sel_ref[pl.ds(pl.multiple_of(t4 * NB, NB), NB), :]
        set_bias(qs_ref, jnp.where(rows > 0.5, 0.0, MASK_VALUE))
        s = _dot(ks_ref[0,pl.ds(pl.multiple_of(t4 * KT, KT), KT), :], qs_ref[...])
        s_ref[...] = s
        mx_ref[...] = jnp.max(s, axis=0, keepdims=True)

    def accumulate(s_ref, mx_ref, t4):
        m_old = ms_ref[...]
        m_new = jnp.maximum(m_old, mx_ref[...])
        pt = jnp.exp2(s_ref[...] - m_new).astype(BF16)
        vT = jnp.concatenate([vsT_ref[0,(KT // QB) * t4 + i] for i in range(KT // QB)], axis=1)
        as_ref[...] = jnp.exp2(m_old - m_new) * as_ref[...] + _dot(vT, pt)
        ms_ref[...] = m_new

    nxt = jnp.minimum(c + 1, pl.num_programs(2) - 1)
    compressed_scores(nxt, heads_on_lanes(qn_ref))

    scores(qs0_ref, s0_ref, mx0_ref, 0)
    scores(qs1_ref, s1_ref, mx1_ref, 1)

    diag = pl.ds(pl.multiple_of(c * QB, QB), QB)
    qz = qz_ref[...]
    kt0 = jnp.maximum(c - nwt, 0)
    b8 = lax.broadcasted_iota(jnp.int32, (NB, QB), 0)
    tile_of_col = ((b8 - 2 * kt0) & (NB - 1)) >> 1
    set_bias(qw_ref, jnp.where(tile_of_col < jnp.minimum(c, nwt), 0.0, MASK_VALUE))
    kwc = kw_ref[0, pl.ds(pl.multiple_of(kt0 * QB, QB), nwt * QB), :]
    sw_ref[0:nwt * QB, :] = mask_first_tile(_dot(kwc, qw_ref[...]), (rr > cc) | (c < nwt))
    sw_ref[nwt * QB:, :] = mask_first_tile(_dot(kw_ref[0, diag, :], qz), rr <= cc)
    sd_ref[...] = mask_first_tile(_dot(ks_ref[0, diag, :], qz), rr <= cc)

    sw = sw_ref[...]
    pw = jnp.exp2(sw - jnp.max(sw, axis=0, keepdims=True)).astype(BF16)
    vw = jnp.concatenate([vwT_ref[0,kt0 + i] for i in range(nwt)] + [vwT_ref[0,c]], axis=1)
    aw_ref[...] = _dot(vw, pw)
    sd = sd_ref[...]
    md = jnp.max(sd, axis=0, keepdims=True)
    ms_ref[...] = md
    as_ref[...] = _dot(vsT_ref[0,c], jnp.exp2(sd - md).astype(BF16))
    pl.when(select_blocks(nxt))(lambda: rank_blocks(nxt))

    def three_tiles(t):
        scores(qs2_ref, s2_ref, mx2_ref, t + 2)
        accumulate(s0_ref, mx0_ref, t)
        scores(qs0_ref, s0_ref, mx0_ref, t + 3)
        accumulate(s1_ref, mx1_ref, t + 1)
        scores(qs1_ref, s1_ref, mx1_ref, t + 4)
        accumulate(s2_ref, mx2_ref, t + 2)

    def sel_body(j, carry):
        three_tiles(6 * j)
        three_tiles(6 * j + 3)
        return carry

    n_tiles = (c * QB + KT - 1) // KT
    trips = jnp.maximum(n_tiles - 1, 0) // 6
    lax.fori_loop(0, trips, sel_body, 0)
    half_trip = n_tiles - 6 * trips > 3

    @pl.when(half_trip)
    def _():
        three_tiles(6 * trips)

    t_left = 6 * trips + jnp.where(half_trip, 3, 0)
    left = n_tiles - t_left

    @pl.when(left == 1)
    def _():
        accumulate(s0_ref, mx0_ref, t_left)

    @pl.when(left == 2)
    def _():
        accumulate(s0_ref, mx0_ref, t_left)
        accumulate(s1_ref, mx1_ref, t_left + 1)

    @pl.when(left == 3)
    def _():
        scores(qs2_ref, s2_ref, mx2_ref, t_left + 2)
        accumulate(s0_ref, mx0_ref, t_left)
        accumulate(s1_ref, mx1_ref, t_left + 1)
        accumulate(s2_ref, mx2_ref, t_left + 2)

    o = gate_rows(0) * oc_ref[...] + gate_rows(1) * normalised(as_ref) + gate_rows(2) * normalised(aw_ref)
    for hp in range(HG // 2):
        pair = jnp.concatenate([o[:, (2 * hp) * QB:(2 * hp + 1) * QB],
                                o[:, (2 * hp + 1) * QB:(2 * hp + 2) * QB]], axis=0)
        out_ref[:, hp * 2 * DH:(hp + 1) * 2 * DH] = pair.T.astype(BF16)


def _nsa(qT, kc, vcT, ks, vsT, kw, vwT, gates, *, b, s):
    assert s % (2 * KT) == 0 and s >= WINDOW + QB and s // CMP_STRIDE // 8 <= KA - DH
    nq = s // QB
    nc = s // CMP_STRIDE
    ns = s // SLC_BLOCK
    n_sel = min(N_SELECT, ns)
    W = HG * QB
    bg = lambda i, j, c: (i, j, 0, 0)
    keys = pl.BlockSpec((1, s, KA), lambda i, j, c: (j, i, 0))
    values = pl.BlockSpec((1, nq, VROWS, QB), lambda i, j, c: (j, i, 0, 0))
    qa = pltpu.VMEM((KA, W), BF16)
    return pl.pallas_call(
        functools.partial(_nsa_kernel, nc=nc, ns=ns, n_sel=n_sel),
        grid=(b, NG, nq),
        in_specs=[
            pl.BlockSpec((HG * DH, QB), lambda i, j, c: (j, i * nq + c)),
            pl.BlockSpec((HG * DH, QB), lambda i, j, c: (j, i * nq + jnp.minimum(c + 1, nq - 1))),
            pl.BlockSpec((1, 1, nc, KA), bg),
            pl.BlockSpec((1, 1, DH, nc), bg),
            keys, values, keys, values,
            pl.BlockSpec((GATE_ROWS // NG, QB), lambda i, j, c: (j, i * nq + c)),
        ],
        out_specs=pl.BlockSpec((QB, HG * DH), lambda i, j, c: (i * nq + c, j)),
        out_shape=jax.ShapeDtypeStruct((b * s, NSA_Q_W), BF16),
        scratch_shapes=[
            qa, qa, qa, qa, qa, qa,
            pltpu.VMEM((nc, W), F32),
            pltpu.VMEM((8 + nc, QB), F32),
            pltpu.VMEM((ns, QB), F32),
            pltpu.VMEM((ns, QB), F32),
            pltpu.VMEM((ns, QB), F32),
            pltpu.VMEM((1, W), F32),
            pltpu.VMEM((VROWS, W), F32),
            pltpu.VMEM((WINDOW + QB, W), F32),
            pltpu.VMEM((VROWS, W), F32),
            pltpu.VMEM((QB, W), F32),
            pltpu.VMEM((DH, W), F32),
            pltpu.VMEM((DH, W), F32),
            pltpu.VMEM((KT, W), F32),
            pltpu.VMEM((KT, W), F32),
            pltpu.VMEM((KT, W), F32),
            pltpu.VMEM((1, W), F32),
            pltpu.VMEM((1, W), F32),
            pltpu.VMEM((1, W), F32),
        ],
        compiler_params=_params(3),
        name="nsa",
    )(qT, qT, kc, vcT, ks, vsT, kw, vwT, gates)


def _ret_kernel(q_ref, k_ref, v_ref, g_ref, cos_ref, sin_ref, decay_ref, xi_ref, zeta_ref, gch_ref,
                o_ref, r_ref, *, nchunk):
    @pl.when(pl.program_id(1) == 0)
    def _():
        r_ref[...] = jnp.zeros((RET_HEADS, DK, DK), F32)

    for ci in range(nchunk):
        rows = slice(ci * RET_CHUNK, (ci + 1) * RET_CHUNK)
        cosv = cos_ref[rows, :]
        sinv = sin_ref[rows, :]

        def rope(x):
            return x * cosv + pltpu.roll(x, DK // 2, 1) * sinv

        for h in range(RET_HEADS):
            cols = slice(h * DK, (h + 1) * DK)
            q = rope(q_ref[rows, cols])
            k = rope(k_ref[rows, cols]) * (DK ** -0.5)
            v = v_ref[rows, cols].astype(BF16)
            r_prev = r_ref[h]
            inner = _dot_nt(q.astype(BF16), k.astype(BF16)) * decay_ref[h]
            y = _dot(inner.astype(BF16), v) + _dot((q * xi_ref[h]).astype(BF16), r_prev.astype(BF16))
            y = y * lax.rsqrt(jnp.mean(y * y, axis=-1, keepdims=True) + EPS)
            g = g_ref[rows, cols]
            o_ref[rows, cols] = (g * jax.nn.sigmoid(g) * y).astype(BF16)
            r_ref[h] = gch_ref[h, 0:1, :] * r_prev + _dot_tn((k * zeta_ref[h]).astype(BF16), v)


def _retention(zr, zv, cosf, sinf, decay, xi, zeta, gch, *, b, s, ts):
    nst = s // ts
    col = lambda blk: (lambda i, j: (i * nst + j, blk))
    tab = lambda i, j: (j, 0)
    whole = lambda i, j: (0, 0, 0)
    return pl.pallas_call(
        functools.partial(_ret_kernel, nchunk=ts // RET_CHUNK),
        grid=(b, nst),
        in_specs=[
            pl.BlockSpec((ts, RET_W), col(0)),
            pl.BlockSpec((ts, RET_W), col(1)),
            pl.BlockSpec((ts, RET_W), col(0)),
            pl.BlockSpec((ts, RET_W), col(2)),
            pl.BlockSpec((ts, DK), tab),
            pl.BlockSpec((ts, DK), tab),
            pl.BlockSpec((RET_HEADS, RET_CHUNK, RET_CHUNK), whole),
            pl.BlockSpec((RET_HEADS, RET_CHUNK, DK), whole),
            pl.BlockSpec((RET_HEADS, RET_CHUNK, DK), whole),
            pl.BlockSpec((RET_HEADS, 8, DK), whole),
        ],
        out_specs=pl.BlockSpec((ts, RET_W), lambda i, j: (i * nst + j, 0)),
        out_shape=jax.ShapeDtypeStruct((b * s, RET_W), BF16),
        scratch_shapes=[pltpu.VMEM((RET_HEADS, DK, DK), F32)],
        compiler_params=_params(2),
        name="retention",
    )(zr, zr, zv, zr, cosf, sinf, decay, xi, zeta, gch)


def _mix_kernel(x_ref, ya_ref, yb_ref, ma_ref, mb_ref, wa_ref, wb_ref, wo_ref, o_ref):
    ua = _dot(ya_ref[...], wa_ref[...])
    ub = _dot(yb_ref[...], wb_ref[...])
    mix = ma_ref[...].astype(F32) * ua + mb_ref[...].astype(F32) * ub
    o_ref[...] = x_ref[...] + _dot(mix.astype(BF16), wo_ref[...])


def _mix(x2, ya, yb, zm, wa, wb, wo, *, tm):
    n = x2.shape[0]
    full = lambda i: (0, 0)
    return pl.pallas_call(
        _mix_kernel,
        grid=(n // tm,),
        in_specs=[
            pl.BlockSpec((tm, D_MODEL), lambda i: (i, 0)),
            pl.BlockSpec((tm, NSA_Q_W), lambda i: (i, 0)),
            pl.BlockSpec((tm, RET_W), lambda i: (i, 0)),
            pl.BlockSpec((tm, D_MODEL), lambda i: (i, 0)),
            pl.BlockSpec((tm, D_MODEL), lambda i: (i, 1)),
            pl.BlockSpec((NSA_Q_W, D_MODEL), full),
            pl.BlockSpec((RET_W, D_MODEL), full),
            pl.BlockSpec((D_MODEL, D_MODEL), full),
        ],
        out_specs=pl.BlockSpec((tm, D_MODEL), lambda i: (i, 0)),
        out_shape=jax.ShapeDtypeStruct((n, D_MODEL), F32),
        compiler_params=_params(1),
        name="mix",
    )(x2, ya, yb, zm, zm, wa, wb, wo)


def _ffn_kernel(x_ref, p_ref, gm_ref, w1_ref, w2_ref, gp_ref, wg_ref, wp_ref, o_ref, *, ff_chunk):
    x = x_ref[...]
    h = _rms_rows(x, gm_ref[...]).astype(BF16)
    for j in range(D_FF // ff_chunk):
        cols = slice(j * ff_chunk, (j + 1) * ff_chunk)
        u = jnp.square(jnp.maximum(_dot(h, w1_ref[:, cols]), 0.0)).astype(BF16)
        x = x + _dot(u, w2_ref[cols, :])
    gate = jax.nn.sigmoid(_dot(_rms_rows(x, gp_ref[...]).astype(BF16), wg_ref[...]))
    o_ref[...] = x + gate * _dot(p_ref[...].astype(BF16), wp_ref[...])


def _ffn(x2, p2, gm, w1, w2, gp, wg, wp, *, tm):
    n = x2.shape[0]
    full = lambda i: (0, 0)
    once = pl.Buffered(1)
    return pl.pallas_call(
        functools.partial(_ffn_kernel, ff_chunk=1024),
        grid=(n // tm,),
        in_specs=[
            pl.BlockSpec((tm, D_MODEL), lambda i: (i, 0)),
            pl.BlockSpec((tm, PLE_DIM), lambda i: (i, 0)),
            pl.BlockSpec((1, D_MODEL), full),
            pl.BlockSpec((D_MODEL, D_FF), full, pipeline_mode=once),
            pl.BlockSpec((D_FF, D_MODEL), full, pipeline_mode=once),
            pl.BlockSpec((1, D_MODEL), full),
            pl.BlockSpec((D_MODEL, D_MODEL), full, pipeline_mode=once),
            pl.BlockSpec((PLE_DIM, D_MODEL), full, pipeline_mode=once),
        ],
        out_specs=pl.BlockSpec((tm, D_MODEL), lambda i: (i, 0)),
        out_shape=jax.ShapeDtypeStruct((n, D_MODEL), F32),
        compiler_params=_params(1),
        name="ffn",
    )(x2, p2, gm, w1, w2, gp, wg, wp)


def _retention_tables(s):
    half = DK // 2
    pos = jnp.arange(s, dtype=F32)
    inv = ROPE_BASE ** (-jnp.arange(half, dtype=F32) / half)
    ang = pos[:, None] * inv[None, :]
    cos, sin = jnp.cos(ang), jnp.sin(ang)
    cosf = jnp.concatenate([cos, cos], axis=-1)
    sinf = jnp.concatenate([-sin, sin], axis=-1)
    C = RET_CHUNK
    gamma = 1.0 - 2.0 ** (-5.0 - jnp.arange(RET_HEADS, dtype=F32))
    lg = jnp.log(gamma)
    n = jnp.arange(C, dtype=F32)
    diff = n[:, None] - n[None, :]
    decay = jnp.where(diff >= 0, jnp.exp(lg[:, None, None] * jnp.maximum(diff, 0.0)), 0.0)
    xi = jnp.broadcast_to(jnp.exp(lg[:, None] * (n + 1.0))[:, :, None], (RET_HEADS, C, DK))
    zeta = jnp.broadcast_to(jnp.exp(lg[:, None] * (C - 1.0 - n))[:, :, None], (RET_HEADS, C, DK))
    gch = jnp.broadcast_to(jnp.exp(lg * C)[:, None, None], (RET_HEADS, 8, DK))
    return cosf, sinf, decay, xi, zeta, gch


def _layer(x2, p2, w, tabs, *, b, s):
    nc = s // CMP_STRIDE
    zr, zv, zm, qT, cz, ksa, kwa, vsa, vwa, gT = _proj(
        x2, w["norm_mix"], w["wn"], w["wt"], w["qg"], w["kgc"], tm=256)
    cz = cz.reshape(2 * NG, b, s, DH)
    last = jnp.tile(cz[:, :, s - 1:s, :], (1, 1, 8, CMP_STRIDE))
    kc, vcT = _compress(cz.reshape(2 * NG, b, nc, CMP_STRIDE * DH), last, w["pos_k"], w["pos_v"],
                        w["w1k"], w["w2k"], w["w1v"], w["w2vT"], w["kgr"], nc=nc)
    ya = _nsa(qT, kc, vcT, ksa, vsa, kwa, vwa, gT, b=b, s=s)
    yb = _retention(zr, zv, *tabs, b=b, s=s, ts=min(s, 1024))
    x2 = _mix(x2, ya, yb, zm, w["wa"], w["wb"], w["wo"], tm=512)
    return _ffn(x2, p2, w["norm_mlp"], w["w1"], w["w2"], w["norm_ple"], w["wg"], w["wp"], tm=512)


def kernel(x, p, norm_mix, w_in, nsa_q_norm, nsa_k_norm, cmp_pos_k, cmp_pos_v, cmp_w1_k, cmp_w2_k,
           cmp_w1_v, cmp_w2_v, w_up_nsa, w_up_ret, w_out, norm_mlp, w_ff1, w_ff2, norm_ple, w_ple,
           w_ple_gate):
    b, s, _ = x.shape
    depth = w_in.shape[0]
    tabs = _retention_tables(s)
    x2 = x.reshape(b * s, D_MODEL)
    q_end = NSA_Q_W
    kv = NSA_KV_W
    g_end = NSA_Q_W + 6 * kv + NSA_GATE_W
    for i in range(depth):
        wi = w_in[i]
        wg = wi[:, q_end + 6 * kv:g_end].reshape(D_MODEL, 3, NG, HG).transpose(0, 2, 1, 3)
        wg = jnp.pad(wg.reshape(D_MODEL, NG, 3 * HG), ((0, 0), (0, 0), (0, GATE_ROWS // NG - 3 * HG)))
        wt = jnp.concatenate([
            wi[:, 0:q_end + 2 * kv],
            wi[:, q_end + 2 * kv:q_end + 3 * kv], wi[:, q_end + 4 * kv:q_end + 5 * kv],
            wi[:, q_end + 3 * kv:q_end + 4 * kv], wi[:, q_end + 5 * kv:q_end + 6 * kv],
            wg.reshape(D_MODEL, GATE_ROWS)], axis=1).T.astype(BF16)
        w = dict(
            norm_mix=norm_mix[i][None, :], wt=wt,
            wn=jnp.concatenate([wi[:, g_end:g_end + 2 * RET_W], wi[:, g_end + 3 * RET_W:g_end + 4 * RET_W],
                                wi[:, g_end + 2 * RET_W:g_end + 3 * RET_W], wi[:, g_end + 4 * RET_W:]],
                               axis=1).astype(BF16),
            qg=nsa_q_norm[i][:, None], kgc=nsa_k_norm[i][:, None],
            kgr=jnp.pad(nsa_k_norm[i][None, :], ((0, 0), (0, KA - DH))),
            pos_k=cmp_pos_k[i].reshape(1, CMP_BLOCK * DH), pos_v=cmp_pos_v[i].reshape(1, CMP_BLOCK * DH),
            w1k=cmp_w1_k[i].astype(BF16), w2k=jnp.pad(cmp_w2_k[i], ((0, 0), (0, KA - DH))).astype(BF16),
            w1v=cmp_w1_v[i].astype(BF16), w2vT=cmp_w2_v[i].T.astype(BF16),
            wa=w_up_nsa[i].astype(BF16), wb=w_up_ret[i].astype(BF16), wo=w_out[i].astype(BF16),
            norm_mlp=norm_mlp[i][None, :], w1=w_ff1[i].astype(BF16), w2=w_ff2[i].astype(BF16),
            norm_ple=norm_ple[i][None, :], wg=w_ple_gate[i].astype(BF16), wp=w_ple[i].astype(BF16),
        )
        x2 = _layer(x2, p[i].reshape(b * s, PLE_DIM), w, tabs, b=b, s=s)
    return x2.reshape(b, s, D_MODEL)
```

```python
import functools
import math

import jax
import jax.numpy as jnp
from jax import lax
from jax.experimental import pallas as pl
from jax.experimental.pallas import tpu as pltpu

D_MODEL = 1024
PLE_DIM = 256
NSA_HEADS = 8
DH = 64
NG = 2
HG = NSA_HEADS // NG
CMP_BLOCK = 32
CMP_STRIDE = 16
CMP_HIDDEN = 4 * DH
SLC_BLOCK = 64
N_SELECT = 16
WINDOW = 512
QB = 128
RET_HEADS = 4
DK = 128
RET_CHUNK = 128
ROPE_BASE = 10000.0
D_FF = 4 * D_MODEL
EPS = 1e-6
MASK_VALUE = -1e30
FORCE_SCORE = 1e4
LOG2E = math.log2(math.e)

NSA_Q_W = NSA_HEADS * DH
NSA_KV_W = NG * DH
NSA_GATE_W = 3 * NSA_HEADS
RET_W = RET_HEADS * DK
NAT_W = 4 * RET_W + 2 * D_MODEL
GATE_ROWS = 32
T_ROWS = NSA_Q_W + 6 * NSA_KV_W + GATE_ROWS

KT = 512
NB = KT // SLC_BLOCK
KA = 128
VROWS = DH + 16
BIAS_ROWS = 16

PROJ_TM = 256
POST_TM = 512
RET_TS = 1024
FF_CHUNK = 1024

VMEM_LIMIT = 56 * 1024 * 1024

F32 = jnp.float32
BF16 = jnp.bfloat16


def _dot(a, b):
    return jnp.dot(a, b, preferred_element_type=F32)


def _dot_nt(a, b):
    return lax.dot_general(a, b, (((1,), (1,)), ((), ())), preferred_element_type=F32)


def _dot_tn(a, b):
    return lax.dot_general(a, b, (((0,), (0,)), ((), ())), preferred_element_type=F32)


def _rms_rows(x, g):
    y = x * lax.rsqrt(jnp.mean(x * x, axis=-1, keepdims=True) + EPS)
    return y * g


def _params(n_axes):
    return pltpu.CompilerParams(dimension_semantics=("arbitrary",) * n_axes,
                                vmem_limit_bytes=VMEM_LIMIT)


def _proj_kernel(x_ref, g_ref, wn_ref, wt_ref, qg_ref, kg_ref,
                 zr_ref, zv_ref, zm_ref, qT_ref, cz_ref, ksa_ref, kwa_ref, vsa_ref, vwa_ref, gT_ref):
    h = _rms_rows(x_ref[...], g_ref[...]).astype(BF16)
    zn = _dot(h, wn_ref[...])
    zr_ref[...] = zn[:, 0:3 * RET_W]
    zv_ref[...] = zn[:, 3 * RET_W:4 * RET_W].astype(BF16)
    zm_ref[...] = jax.nn.sigmoid(zn[:, 4 * RET_W:]).astype(BF16)
    zt = _dot_nt(wt_ref[...], h)
    qg = qg_ref[...]
    kg = kg_ref[...]

    def norm_cols(z, g):
        return z * lax.rsqrt(jnp.mean(z * z, axis=0, keepdims=True) + EPS) * g

    for hh in range(NSA_HEADS):
        rows = slice(hh * DH, (hh + 1) * DH)
        qT_ref[rows, :] = (norm_cols(zt[rows, :], qg) * (DH ** -0.5 * LOG2E)).astype(BF16)
    tm = zt.shape[1]
    zpad = jnp.zeros((KA - DH, tm), F32)
    o = NSA_Q_W
    for j in range(2 * NG):
        rows = slice(o + j * DH, o + (j + 1) * DH)
        cz_ref[j] = jnp.concatenate([zt[rows, :], zpad], axis=0).T[:, 0:DH]
    o += 2 * NSA_KV_W
    lane_row = lax.broadcasted_iota(jnp.int32, (KA - DH, tm), 0)
    pos = pl.program_id(0) * tm + lax.broadcasted_iota(jnp.int32, (KA - DH, tm), 1)
    ind = jnp.where(lane_row == ((pos // SLC_BLOCK) & (NB - 1)), 1.0, 0.0)
    for j, ref in enumerate((ksa_ref, ksa_ref, kwa_ref, kwa_ref)):
        rows = slice(o + j * DH, o + (j + 1) * DH)
        ref[j % NG] = jnp.concatenate([norm_cols(zt[rows, :], kg), ind], axis=0).T.astype(BF16)
    o += 2 * NSA_KV_W
    tail_row = lax.broadcasted_iota(jnp.int32, (VROWS - DH, tm), 0)
    vtail = jnp.where(tail_row == 0, 1.0, 0.0)
    for j, ref in enumerate((vsa_ref, vsa_ref, vwa_ref, vwa_ref)):
        rows = slice(o + j * DH, o + (j + 1) * DH)
        va = jnp.concatenate([zt[rows, :], vtail], axis=0).astype(BF16)
        for u in range(tm // QB):
            ref[j % NG, u] = va[:, u * QB:(u + 1) * QB]
    o += 2 * NSA_KV_W
    gT_ref[...] = jax.nn.sigmoid(zt[o:o + GATE_ROWS, :])


def _proj(x2, g, wn, wt, qg, kg, *, tm):
    n = x2.shape[0]
    full = lambda i: (0, 0)
    colblk = lambda i: (0, i)
    rowblk3 = lambda i: (0, i, 0)
    rowblk4 = lambda i: (0, i, 0, 0)
    return pl.pallas_call(
        _proj_kernel,
        grid=(n // tm,),
        in_specs=[
            pl.BlockSpec((tm, D_MODEL), lambda i: (i, 0)),
            pl.BlockSpec((1, D_MODEL), full),
            pl.BlockSpec((D_MODEL, NAT_W), full),
            pl.BlockSpec((T_ROWS, D_MODEL), full),
            pl.BlockSpec((DH, 1), full),
            pl.BlockSpec((DH, 1), full),
        ],
        out_specs=[
            pl.BlockSpec((tm, 3 * RET_W), lambda i: (i, 0)),
            pl.BlockSpec((tm, RET_W), lambda i: (i, 0)),
            pl.BlockSpec((tm, 2 * D_MODEL), lambda i: (i, 0)),
            pl.BlockSpec((NSA_Q_W, tm), colblk),
            pl.BlockSpec((2 * NG, tm, DH), rowblk3),
            pl.BlockSpec((NG, tm, KA), rowblk3),
            pl.BlockSpec((NG, tm, KA), rowblk3),
            pl.BlockSpec((NG, tm // QB, VROWS, QB), rowblk4),
            pl.BlockSpec((NG, tm // QB, VROWS, QB), rowblk4),
            pl.BlockSpec((GATE_ROWS, tm), colblk),
        ],
        out_shape=[
            jax.ShapeDtypeStruct((n, 3 * RET_W), F32),
            jax.ShapeDtypeStruct((n, RET_W), BF16),
            jax.ShapeDtypeStruct((n, 2 * D_MODEL), BF16),
            jax.ShapeDtypeStruct((NSA_Q_W, n), BF16),
            jax.ShapeDtypeStruct((2 * NG, n, DH), F32),
            jax.ShapeDtypeStruct((NG, n, KA), BF16),
            jax.ShapeDtypeStruct((NG, n, KA), BF16),
            jax.ShapeDtypeStruct((NG, n // QB, VROWS, QB), BF16),
            jax.ShapeDtypeStruct((NG, n // QB, VROWS, QB), BF16),
            jax.ShapeDtypeStruct((GATE_ROWS, n), F32),
        ],
        compiler_params=_params(1),
        name="proj",
    )(x2, g, wn, wt, qg, kg)


def _gelu_tanh(x):
    cdf = 0.5 * (1.0 + jnp.tanh(math.sqrt(2.0 / math.pi) * (x + 0.044715 * (x ** 3))))
    return x * cdf


def _cmp_kernel(zk_ref, zv_ref, lk_ref, lv_ref, pk_ref, pv_ref, w1k_ref, w2k_ref, w1v_ref, w2vT_ref, kg_ref,
                kc_ref, vcT_ref, *, nc):
    half = CMP_STRIDE * DH
    last_row = lax.broadcasted_iota(jnp.int32, (nc, CMP_HIDDEN), 0) == nc - 1

    def hidden(z_ref, l_ref, p_ref, w1_ref):
        z = z_ref[0, 0]
        first = _dot((z + p_ref[:, 0:half]).astype(BF16), w1_ref[0:half, :])
        second = _dot((z + p_ref[:, half:2 * half]).astype(BF16), w1_ref[half:2 * half, :])
        clamp = _dot((l_ref[0, 0] + p_ref[:, half:2 * half]).astype(BF16), w1_ref[half:2 * half, :])
        second = jnp.where(last_row, clamp[0:1, :], pltpu.roll(second, nc - 1, 0))
        return _gelu_tanh(first + second)

    hk = hidden(zk_ref, lk_ref, pk_ref, w1k_ref).astype(BF16)
    k = _dot(hk, w2k_ref[...])
    k = k * lax.rsqrt(jnp.sum(k * k, axis=-1, keepdims=True) * (1.0 / DH) + EPS) * kg_ref[...]
    row = lax.broadcasted_iota(jnp.int32, (nc, KA), 0)
    lane = lax.broadcasted_iota(jnp.int32, (nc, KA), 1)
    kc_ref[0, 0] = (k + jnp.where(lane - DH == (row >> 3), 1.0, 0.0)).astype(BF16)
    hv = hidden(zv_ref, lv_ref, pv_ref, w1v_ref).astype(BF16)
    vcT_ref[0, 0] = _dot_nt(w2vT_ref[...], hv).astype(BF16)


def _compress(cz, last, pk, pv, w1k, w2k, w1v, w2vT, kg, *, nc):
    b = cz.shape[1]
    half = CMP_STRIDE * DH
    full = lambda i, j: (0, 0)
    return pl.pallas_call(
        functools.partial(_cmp_kernel, nc=nc),
        grid=(b, NG),
        in_specs=[
            pl.BlockSpec((1, 1, nc, half), lambda i, j: (j, i, 0, 0)),
            pl.BlockSpec((1, 1, nc, half), lambda i, j: (NG + j, i, 0, 0)),
            pl.BlockSpec((1, 1, 8, half), lambda i, j: (j, i, 0, 0)),
            pl.BlockSpec((1, 1, 8, half), lambda i, j: (NG + j, i, 0, 0)),
            pl.BlockSpec((1, 2 * half), full),
            pl.BlockSpec((1, 2 * half), full),
            pl.BlockSpec((2 * half, CMP_HIDDEN), full),
            pl.BlockSpec((CMP_HIDDEN, KA), full),
            pl.BlockSpec((2 * half, CMP_HIDDEN), full),
            pl.BlockSpec((DH, CMP_HIDDEN), full),
            pl.BlockSpec((1, KA), full),
        ],
        out_specs=[
            pl.BlockSpec((1, 1, nc, KA), lambda i, j: (i, j, 0, 0)),
            pl.BlockSpec((1, 1, DH, nc), lambda i, j: (i, j, 0, 0)),
        ],
        out_shape=[
            jax.ShapeDtypeStruct((b, NG, nc, KA), BF16),
            jax.ShapeDtypeStruct((b, NG, DH, nc), BF16),
        ],
        compiler_params=_params(2),
        name="compress",
    )(cz, cz, last, last, pk, pv, w1k, w2k, w1v, w2vT, kg)


def _nsa_kernel(q_ref, qn_ref, kc_ref, vcT_ref, ks_ref, vsT_ref, kw_ref, vwT_ref, gate_ref, out_ref,
                qz_ref, qw_ref, qs0_ref, qs1_ref, qs2_ref, qc_ref, sc_ref, ps_ref, score_ref, sel_ref, seln_ref,
                ms_ref, as_ref, sw_ref, aw_ref, sd_ref, oc_ref, ocn_ref,
                s0_ref, s1_ref, s2_ref, mx0_ref, mx1_ref, mx2_ref,
                *, nc, ns, n_sel):
    c = pl.program_id(2)
    ratio = SLC_BLOCK // CMP_STRIDE
    W = HG * QB
    nwt = WINDOW // QB

    def heads_on_lanes(ref):
        return jnp.concatenate([ref[h * DH:(h + 1) * DH, :] for h in range(HG)], axis=1)

    def set_bias(ref, rows):
        b = jnp.concatenate([rows] * HG, axis=1)
        ref[DH:DH + BIAS_ROWS, :] = jnp.concatenate(
            [b, jnp.zeros((BIAS_ROWS - NB, W), F32)], axis=0).astype(BF16)

    def compressed_scores(ct, qt):
        grp = lax.broadcasted_iota(jnp.int32, (KA - DH, W), 0)
        qc_ref[0:DH, :] = qt
        qc_ref[DH:KA, :] = jnp.where(grp <= ct, 0.0, MASK_VALUE).astype(BF16)
        sc_ref[...] = _dot(kc_ref[0, 0], qc_ref[...])

    def select_blocks(ct):
        r0 = pl.multiple_of(jnp.maximum(ct - 1, 0) * 8, 8)
        row = r0 + lax.broadcasted_iota(jnp.int32, (16, W), 0)
        tcol = ct * QB + (lax.broadcasted_iota(jnp.int32, (16, W), 1) & (QB - 1))
        edge = sc_ref[pl.ds(r0, 16), :]
        sc_ref[pl.ds(r0, 16), :] = jnp.where(row * CMP_STRIDE + (CMP_BLOCK - 1) <= tcol, edge, MASK_VALUE)
        s = sc_ref[...]
        e = jnp.exp2(s - jnp.max(s, axis=0, keepdims=True))
        t1 = ct * QB + (lax.broadcasted_iota(jnp.int32, (1, W), 1) & (QB - 1))
        inv = jnp.where(t1 >= CMP_BLOCK - 1, 1.0 / jnp.maximum(jnp.sum(e, axis=0, keepdims=True), 1e-30), 0.0)
        ocn_ref[...] = _dot(vcT_ref[0, 0], e.astype(BF16)) * inv

        psum = e[:, 0:QB] * inv[:, 0:QB]
        for h in range(1, HG):
            psum = psum + e[:, h * QB:(h + 1) * QB] * inv[:, h * QB:(h + 1) * QB]
        ps_ref[0:8, :] = jnp.zeros((8, QB), F32)
        ps_ref[8:8 + nc, :] = psum
        imp = ps_ref[pl.ds(8, ns, stride=ratio), :]
        for r in range(1, ratio):
            imp = imp + ps_ref[pl.ds(8 + r, ns, stride=ratio), :]
        imp = imp + ps_ref[pl.ds(7, ns, stride=ratio), :]
        blk = lax.broadcasted_iota(jnp.int32, (ns, QB), 0)
        tq = ct * QB + lax.broadcasted_iota(jnp.int32, (ns, QB), 1)
        cur = tq // SLC_BLOCK
        score = jnp.where(blk == cur, FORCE_SCORE, imp)
        score = jnp.where(blk == cur - 1, 2 * FORCE_SCORE, score)
        score = jnp.where(blk == 0, 3 * FORCE_SCORE, score)
        score = jnp.where(blk > cur, MASK_VALUE, score)
        score_ref[...] = score

        before_diag = blk < 2 * ct
        work = jnp.where((blk == 0) | (blk == cur) | (blk == cur - 1), -jnp.inf, score)
        thr = None
        for _ in range(n_sel - 3):
            thr = jnp.max(work, axis=0, keepdims=True)
            work = jnp.where(work == thr, -jnp.inf, work)
        fast = score >= thr
        cnt = jnp.sum(jnp.where(fast, 1.0, 0.0), axis=0, keepdims=True)
        n_tied = jnp.sum(jnp.where(cnt != float(n_sel), 1.0, 0.0))
        all_selected = 2 * ct + 2 <= n_sel
        seln_ref[...] = jnp.where(before_diag & (fast | all_selected), 1.0, 0.0)
        return jnp.logical_and(jnp.logical_not(all_selected), n_tied > 0.0)

    def rank_blocks(ct):
        score = score_ref[...]
        blk = lax.broadcasted_iota(jnp.int32, (ns, QB), 0)

        def rank_body(jp, rank):
            other = jnp.broadcast_to(score_ref[pl.ds(jp, 1), :], (ns, QB))
            beats = (other > score) | ((other == score) & (jp < blk))
            return rank + jnp.where(beats, 1.0, 0.0)

        rank = lax.fori_loop(0, jnp.minimum(2 * ct + 2, ns), rank_body, jnp.zeros((ns, QB), F32))
        seln_ref[...] = jnp.where((blk < 2 * ct) & (rank < float(n_sel)), 1.0, 0.0)

    @pl.when(c == 0)
    def _():
        compressed_scores(c, heads_on_lanes(q_ref))
        pl.when(select_blocks(c))(lambda: rank_blocks(c))

    sel_ref[...] = seln_ref[...]
    oc_ref[...] = ocn_ref[...]
    q = heads_on_lanes(q_ref)
    qz_ref[...] = jnp.concatenate([q, jnp.zeros((KA - DH, W), BF16)], axis=0)
    for r in (qw_ref, qs0_ref, qs1_ref, qs2_ref):
        r[0:DH, :] = q
        r[DH + BIAS_ROWS:KA, :] = jnp.zeros((KA - DH - BIAS_ROWS, W), BF16)

    def gate_rows(br):
        return jnp.concatenate(
            [jnp.broadcast_to(gate_ref[br * HG + h:br * HG + h + 1, :], (DH, QB)) for h in range(HG)],
            axis=1)

    rr = lax.broadcasted_iota(jnp.int32, (QB, QB), 0)
    cc = lax.broadcasted_iota(jnp.int32, (QB, QB), 1)

    def mask_first_tile(s, emask):
        top = jnp.concatenate(
            [jnp.where(emask, s[0:QB, h * QB:(h + 1) * QB], MASK_VALUE) for h in range(HG)], axis=1)
        return top if s.shape[0] == QB else jnp.concatenate([top, s[QB:, :]], axis=0)

    def normalised(a_ref):
        a = a_ref[...]
        return a[0:DH, :] * (1.0 / jnp.maximum(a[DH:DH + 1, :], 1e-30))

    last_tile = ns // NB - 1

    def scores(qs_ref, s_ref, mx_ref, t4):
        t4 = jnp.minimum(t4, last_tile)
        rows = sel_ref[pl.ds(pl.multiple_of(t4 * NB, NB), NB), :]
        set_bias(qs_ref, jnp.where(rows > 0.5, 0.0, MASK_VALUE))
        s = _dot(ks_ref[0,pl.ds(pl.multiple_of(t4 * KT, KT), KT), :], qs_ref[...])
        s_ref[...] = s
        mx_ref[...] = jnp.max(s, axis=0, keepdims=True)

    def accumulate(s_ref, mx_ref, t4):
        m_old = ms_ref[...]
        m_new = jnp.maximum(m_old, mx_ref[...])
        pt = jnp.exp2(s_ref[...] - m_new).astype(BF16)
        vT = jnp.concatenate([vsT_ref[0,(KT // QB) * t4 + i] for i in range(KT // QB)], axis=1)
        as_ref[...] = jnp.exp2(m_old - m_new) * as_ref[...] + _dot(vT, pt)
        ms_ref[...] = m_new

    nxt = jnp.minimum(c + 1, pl.num_programs(2) - 1)
    compressed_scores(nxt, heads_on_lanes(qn_ref))

    scores(qs0_ref, s0_ref, mx0_ref, 0)
    scores(qs1_ref, s1_ref, mx1_ref, 1)

    diag = pl.ds(pl.multiple_of(c * QB, QB), QB)
    qz = qz_ref[...]
    kt0 = jnp.maximum(c - nwt, 0)
    b8 = lax.broadcasted_iota(jnp.int32, (NB, QB), 0)
    tile_of_col = ((b8 - 2 * kt0) & (NB - 1)) >> 1
    set_bias(qw_ref, jnp.where(tile_of_col < jnp.minimum(c, nwt), 0.0, MASK_VALUE))
    kwc = kw_ref[0, pl.ds(pl.multiple_of(kt0 * QB, QB), nwt * QB), :]
    sw_ref[0:nwt * QB, :] = mask_first_tile(_dot(kwc, qw_ref[...]), (rr > cc) | (c < nwt))
    sw_ref[nwt * QB:, :] = mask_first_tile(_dot(kw_ref[0, diag, :], qz), rr <= cc)
    sd_ref[...] = mask_first_tile(_dot(ks_ref[0, diag, :], qz), rr <= cc)

    sw = sw_ref[...]
    pw = jnp.exp2(sw - jnp.max(sw, axis=0, keepdims=True)).astype(BF16)
    vw = jnp.concatenate([vwT_ref[0,kt0 + i] for i in range(nwt)] + [vwT_ref[0,c]], axis=1)
    aw_ref[...] = _dot(vw, pw)
    sd = sd_ref[...]
    md = jnp.max(sd, axis=0, keepdims=True)
    ms_ref[...] = md
    as_ref[...] = _dot(vsT_ref[0,c], jnp.exp2(sd - md).astype(BF16))
    pl.when(select_blocks(nxt))(lambda: rank_blocks(nxt))

    def three_tiles(t):
        scores(qs2_ref, s2_ref, mx2_ref, t + 2)
        accumulate(s0_ref, mx0_ref, t)
        scores(qs0_ref, s0_ref, mx0_ref, t + 3)
        accumulate(s1_ref, mx1_ref, t + 1)
        scores(qs1_ref, s1_ref, mx1_ref, t + 4)
        accumulate(s2_ref, mx2_ref, t + 2)

    def sel_body(j, carry):
        three_tiles(6 * j)
        three_tiles(6 * j + 3)
        return carry

    n_tiles = (c * QB + KT - 1) // KT
    trips = jnp.maximum(n_tiles - 1, 0) // 6
    lax.fori_loop(0, trips, sel_body, 0)
    half_trip = n_tiles - 6 * trips > 3

    @pl.when(half_trip)
    def _():
        three_tiles(6 * trips)

    t_left = 6 * trips + jnp.where(half_trip, 3, 0)
    left = n_tiles - t_left

    @pl.when(left == 1)
    def _():
        accumulate(s0_ref, mx0_ref, t_left)

    @pl.when(left == 2)
    def _():
        accumulate(s0_ref, mx0_ref, t_left)
        accumulate(s1_ref, mx1_ref, t_left + 1)

    @pl.when(left == 3)
    def _():
        scores(qs2_ref, s2_ref, mx2_ref, t_left + 2)
        accumulate(s0_ref, mx0_ref, t_left)
        accumulate(s1_ref, mx1_ref, t_left + 1)
        accumulate(s2_ref, mx2_ref, t_left + 2)

    o = gate_rows(0) * oc_ref[...] + gate_rows(1) * normalised(as_ref) + gate_rows(2) * normalised(aw_ref)
    for hp in range(HG // 2):
        pair = jnp.concatenate([o[:, (2 * hp) * QB:(2 * hp + 1) * QB],
                                o[:, (2 * hp + 1) * QB:(2 * hp + 2) * QB]], axis=0)
        out_ref[:, hp * 2 * DH:(hp + 1) * 2 * DH] = pair.T.astype(BF16)


def _nsa(qT, kc, vcT, ks, vsT, kw, vwT, gates, *, b, s):
    assert s % (2 * KT) == 0 and s >= WINDOW + QB and s // CMP_STRIDE // 8 <= KA - DH
    nq = s // QB
    nc = s // CMP_STRIDE
    ns = s // SLC_BLOCK
    n_sel = min(N_SELECT, ns)
    W = HG * QB
    bg = lambda i, j, c: (i, j, 0, 0)
    keys = pl.BlockSpec((1, s, KA), lambda i, j, c: (j, i, 0))
    values = pl.BlockSpec((1, nq, VROWS, QB), lambda i, j, c: (j, i, 0, 0))
    qa = pltpu.VMEM((KA, W), BF16)
    return pl.pallas_call(
        functools.partial(_nsa_kernel, nc=nc, ns=ns, n_sel=n_sel),
        grid=(b, NG, nq),
        in_specs=[
            pl.BlockSpec((HG * DH, QB), lambda i, j, c: (j, i * nq + c)),
            pl.BlockSpec((HG * DH, QB), lambda i, j, c: (j, i * nq + jnp.minimum(c + 1, nq - 1))),
            pl.BlockSpec((1, 1, nc, KA), bg),
            pl.BlockSpec((1, 1, DH, nc), bg),
            keys, values, keys, values,
            pl.BlockSpec((GATE_ROWS // NG, QB), lambda i, j, c: (j, i * nq + c)),
        ],
        out_specs=pl.BlockSpec((QB, HG * DH), lambda i, j, c: (i * nq + c, j)),
        out_shape=jax.ShapeDtypeStruct((b * s, NSA_Q_W), BF16),
        scratch_shapes=[
            qa, qa, qa, qa, qa, qa,
            pltpu.VMEM((nc, W), F32),
            pltpu.VMEM((8 + nc, QB), F32),
            pltpu.VMEM((ns, QB), F32),
            pltpu.VMEM((ns, QB), F32),
            pltpu.VMEM((ns, QB), F32),
            pltpu.VMEM((1, W), F32),
            pltpu.VMEM((VROWS, W), F32),
            pltpu.VMEM((WINDOW + QB, W), F32),
            pltpu.VMEM((VROWS, W), F32),
            pltpu.VMEM((QB, W), F32),
            pltpu.VMEM((DH, W), F32),
            pltpu.VMEM((DH, W), F32),
            pltpu.VMEM((KT, W), F32),
            pltpu.VMEM((KT, W), F32),
            pltpu.VMEM((KT, W), F32),
            pltpu.VMEM((1, W), F32),
            pltpu.VMEM((1, W), F32),
            pltpu.VMEM((1, W), F32),
        ],
        compiler_params=_params(3),
        name="nsa",
    )(qT, qT, kc, vcT, ks, vsT, kw, vwT, gates)


def _ret_kernel(q_ref, k_ref, v_ref, g_ref, cos_ref, sin_ref, decay_ref, xi_ref, zeta_ref, gch_ref,
                o_ref, r_ref, *, nchunk):
    @pl.when(pl.program_id(1) == 0)
    def _():
        r_ref[...] = jnp.zeros((RET_HEADS, DK, DK), F32)

    for ci in range(nchunk):
        rows = slice(ci * RET_CHUNK, (ci + 1) * RET_CHUNK)
        cosv = cos_ref[rows, :]
        sinv = sin_ref[rows, :]

        def rope(x):
            return x * cosv + pltpu.roll(x, DK // 2, 1) * sinv

        for h in range(RET_HEADS):
            cols = slice(h * DK, (h + 1) * DK)
            q = rope(q_ref[rows, cols])
            k = rope(k_ref[rows, cols]) * (DK ** -0.5)
            v = v_ref[rows, cols].astype(BF16)
            r_prev = r_ref[h]
            inner = _dot_nt(q.astype(BF16), k.astype(BF16)) * decay_ref[h]
            y = _dot(inner.astype(BF16), v) + _dot((q * xi_ref[h]).astype(BF16), r_prev.astype(BF16))
            y = y * lax.rsqrt(jnp.mean(y * y, axis=-1, keepdims=True) + EPS)
            g = g_ref[rows, cols]
            o_ref[rows, cols] = (g * jax.nn.sigmoid(g) * y).astype(BF16)
            r_ref[h] = gch_ref[h, 0:1, :] * r_prev + _dot_tn((k * zeta_ref[h]).astype(BF16), v)


def _retention(zr, zv, cosf, sinf, decay, xi, zeta, gch, *, b, s, ts):
    nst = s // ts
    col = lambda blk: (lambda i, j: (i * nst + j, blk))
    tab = lambda i, j: (j, 0)
    whole = lambda i, j: (0, 0, 0)
    return pl.pallas_call(
        functools.partial(_ret_kernel, nchunk=ts // RET_CHUNK),
        grid=(b, nst),
        in_specs=[
            pl.BlockSpec((ts, RET_W), col(0)),
            pl.BlockSpec((ts, RET_W), col(1)),
            pl.BlockSpec((ts, RET_W), col(0)),
            pl.BlockSpec((ts, RET_W), col(2)),
            pl.BlockSpec((ts, DK), tab),
            pl.BlockSpec((ts, DK), tab),
            pl.BlockSpec((RET_HEADS, RET_CHUNK, RET_CHUNK), whole),
            pl.BlockSpec((RET_HEADS, RET_CHUNK, DK), whole),
            pl.BlockSpec((RET_HEADS, RET_CHUNK, DK), whole),
            pl.BlockSpec((RET_HEADS, 8, DK), whole),
        ],
        out_specs=pl.BlockSpec((ts, RET_W), lambda i, j: (i * nst + j, 0)),
        out_shape=jax.ShapeDtypeStruct((b * s, RET_W), BF16),
        scratch_shapes=[pltpu.VMEM((RET_HEADS, DK, DK), F32)],
        compiler_params=_params(2),
        name="retention",
    )(zr, zr, zv, zr, cosf, sinf, decay, xi, zeta, gch)


def _post_kernel(x_ref, ya_ref, yb_ref, ma_ref, mb_ref, p_ref, wa_ref, wb_ref, wo_ref,
                 gm_ref, w1_ref, w2_ref, gp_ref, wg_ref, wp_ref, o_ref, *, ff_chunk):
    ua = _dot(ya_ref[...], wa_ref[...])
    ub = _dot(yb_ref[...], wb_ref[...])
    mix = ma_ref[...].astype(F32) * ua + mb_ref[...].astype(F32) * ub
    x = x_ref[...] + _dot(mix.astype(BF16), wo_ref[...])
    h = _rms_rows(x, gm_ref[...]).astype(BF16)
    for j in range(D_FF // ff_chunk):
        cols = slice(j * ff_chunk, (j + 1) * ff_chunk)
        u = jnp.square(jnp.maximum(_dot(h, w1_ref[:, cols]), 0.0)).astype(BF16)
        x = x + _dot(u, w2_ref[cols, :])
    gate = jax.nn.sigmoid(_dot(_rms_rows(x, gp_ref[...]).astype(BF16), wg_ref[...]))
    o_ref[...] = x + gate * _dot(p_ref[...].astype(BF16), wp_ref[...])


def _post(x2, ya, yb, zm, p2, wa, wb, wo, gm, w1, w2, gp, wg, wp, *, tm):
    n = x2.shape[0]
    full = lambda i: (0, 0)
    once = pl.Buffered(1)
    return pl.pallas_call(
        functools.partial(_post_kernel, ff_chunk=FF_CHUNK),
        grid=(n // tm,),
        in_specs=[
            pl.BlockSpec((tm, D_MODEL), lambda i: (i, 0)),
            pl.BlockSpec((tm, NSA_Q_W), lambda i: (i, 0)),
            pl.BlockSpec((tm, RET_W), lambda i: (i, 0)),
            pl.BlockSpec((tm, D_MODEL), lambda i: (i, 0)),
            pl.BlockSpec((tm, D_MODEL), lambda i: (i, 1)),
            pl.BlockSpec((tm, PLE_DIM), lambda i: (i, 0)),
            pl.BlockSpec((NSA_Q_W, D_MODEL), full, pipeline_mode=once),
            pl.BlockSpec((RET_W, D_MODEL), full, pipeline_mode=once),
            pl.BlockSpec((D_MODEL, D_MODEL), full, pipeline_mode=once),
            pl.BlockSpec((1, D_MODEL), full),
            pl.BlockSpec((D_MODEL, D_FF), full, pipeline_mode=once),
            pl.BlockSpec((D_FF, D_MODEL), full, pipeline_mode=once),
            pl.BlockSpec((1, D_MODEL), full),
            pl.BlockSpec((D_MODEL, D_MODEL), full, pipeline_mode=once),
            pl.BlockSpec((PLE_DIM, D_MODEL), full, pipeline_mode=once),
        ],
        out_specs=pl.BlockSpec((tm, D_MODEL), lambda i: (i, 0)),
        out_shape=jax.ShapeDtypeStruct((n, D_MODEL), F32),
        compiler_params=_params(1),
        name="post",
    )(x2, ya, yb, zm, zm, p2, wa, wb, wo, gm, w1, w2, gp, wg, wp)


def _retention_tables(s):
    half = DK // 2
    pos = jnp.arange(s, dtype=F32)
    inv = ROPE_BASE ** (-jnp.arange(half, dtype=F32) / half)
    ang = pos[:, None] * inv[None, :]
    cos, sin = jnp.cos(ang), jnp.sin(ang)
    cosf = jnp.concatenate([cos, cos], axis=-1)
    sinf = jnp.concatenate([-sin, sin], axis=-1)
    C = RET_CHUNK
    gamma = 1.0 - 2.0 ** (-5.0 - jnp.arange(RET_HEADS, dtype=F32))
    lg = jnp.log(gamma)
    n = jnp.arange(C, dtype=F32)
    diff = n[:, None] - n[None, :]
    decay = jnp.where(diff >= 0, jnp.exp(lg[:, None, None] * jnp.maximum(diff, 0.0)), 0.0)
    xi = jnp.broadcast_to(jnp.exp(lg[:, None] * (n + 1.0))[:, :, None], (RET_HEADS, C, DK))
    zeta = jnp.broadcast_to(jnp.exp(lg[:, None] * (C - 1.0 - n))[:, :, None], (RET_HEADS, C, DK))
    gch = jnp.broadcast_to(jnp.exp(lg * C)[:, None, None], (RET_HEADS, 8, DK))
    return cosf, sinf, decay, xi, zeta, gch


def _layer(x2, p2, w, tabs, *, b, s):
    nc = s // CMP_STRIDE
    zr, zv, zm, qT, cz, ksa, kwa, vsa, vwa, gT = _proj(
        x2, w["norm_mix"], w["wn"], w["wt"], w["qg"], w["kgc"], tm=PROJ_TM)
    cz = cz.reshape(2 * NG, b, s, DH)
    last = jnp.tile(cz[:, :, s - 1:s, :], (1, 1, 8, CMP_STRIDE))
    kc, vcT = _compress(cz.reshape(2 * NG, b, nc, CMP_STRIDE * DH), last, w["pos_k"], w["pos_v"],
                        w["w1k"], w["w2k"], w["w1v"], w["w2vT"], w["kgr"], nc=nc)
    ya = _nsa(qT, kc, vcT, ksa, vsa, kwa, vwa, gT, b=b, s=s)
    yb = _retention(zr, zv, *tabs, b=b, s=s, ts=min(s, RET_TS))
    return _post(x2, ya, yb, zm, p2, w["wa"], w["wb"], w["wo"], w["norm_mlp"], w["w1"], w["w2"],
                 w["norm_ple"], w["wg"], w["wp"], tm=POST_TM)


def kernel(x, p, norm_mix, w_in, nsa_q_norm, nsa_k_norm, cmp_pos_k, cmp_pos_v, cmp_w1_k, cmp_w2_k,
           cmp_w1_v, cmp_w2_v, w_up_nsa, w_up_ret, w_out, norm_mlp, w_ff1, w_ff2, norm_ple, w_ple,
           w_ple_gate):
    b, s, _ = x.shape
    depth = w_in.shape[0]
    tabs = _retention_tables(s)
    x2 = x.reshape(b * s, D_MODEL)
    q_end = NSA_Q_W
    kv = NSA_KV_W
    g_end = NSA_Q_W + 6 * kv + NSA_GATE_W
    for i in range(depth):
        wi = w_in[i]
        wg = wi[:, q_end + 6 * kv:g_end].reshape(D_MODEL, 3, NG, HG).transpose(0, 2, 1, 3)
        wg = jnp.pad(wg.reshape(D_MODEL, NG, 3 * HG), ((0, 0), (0, 0), (0, GATE_ROWS // NG - 3 * HG)))
        wt = jnp.concatenate([
            wi[:, 0:q_end + 2 * kv],
            wi[:, q_end + 2 * kv:q_end + 3 * kv], wi[:, q_end + 4 * kv:q_end + 5 * kv],
            wi[:, q_end + 3 * kv:q_end + 4 * kv], wi[:, q_end + 5 * kv:q_end + 6 * kv],
            wg.reshape(D_MODEL, GATE_ROWS)], axis=1).T.astype(BF16)
        w = dict(
            norm_mix=norm_mix[i][None, :], wt=wt,
            wn=jnp.concatenate([wi[:, g_end:g_end + 2 * RET_W], wi[:, g_end + 3 * RET_W:g_end + 4 * RET_W],
                                wi[:, g_end + 2 * RET_W:g_end + 3 * RET_W], wi[:, g_end + 4 * RET_W:]],
                               axis=1).astype(BF16),
            qg=nsa_q_norm[i][:, None], kgc=nsa_k_norm[i][:, None],
            kgr=jnp.pad(nsa_k_norm[i][None, :], ((0, 0), (0, KA - DH))),
            pos_k=cmp_pos_k[i].reshape(1, CMP_BLOCK * DH), pos_v=cmp_pos_v[i].reshape(1, CMP_BLOCK * DH),
            w1k=cmp_w1_k[i].astype(BF16), w2k=jnp.pad(cmp_w2_k[i], ((0, 0), (0, KA - DH))).astype(BF16),
            w1v=cmp_w1_v[i].astype(BF16), w2vT=cmp_w2_v[i].T.astype(BF16),
            wa=w_up_nsa[i].astype(BF16), wb=w_up_ret[i].astype(BF16), wo=w_out[i].astype(BF16),
            norm_mlp=norm_mlp[i][None, :], w1=w_ff1[i].astype(BF16), w2=w_ff2[i].astype(BF16),
            norm_ple=norm_ple[i][None, :], wg=w_ple_gate[i].astype(BF16), wp=w_ple[i].astype(BF16),
        )
        x2 = _layer(x2, p[i].reshape(b * s, PLE_DIM), w, tabs, b=b, s=s)
    return x2.reshape(b, s, D_MODEL)
```

```python
import functools
import math

import jax
import jax.numpy as jnp
from jax import lax
from jax.experimental import pallas as pl
from jax.experimental.pallas import tpu as pltpu

D_MODEL = 1024
PLE_DIM = 256
NSA_HEADS = 8
DH = 64
NG = 2
HG = NSA_HEADS // NG
CMP_BLOCK = 32
CMP_STRIDE = 16
CMP_HIDDEN = 4 * DH
SLC_BLOCK = 64
N_SELECT = 16
WINDOW = 512
QB = 128
RET_HEADS = 4
DK = 128
RET_CHUNK = 128
ROPE_BASE = 10000.0
D_FF = 4 * D_MODEL
EPS = 1e-6
MASK_VALUE = -1e30
FORCE_SCORE = 1e4
LOG2E = math.log2(math.e)

NSA_Q_W = NSA_HEADS * DH
NSA_KV_W = NG * DH
NSA_GATE_W = 3 * NSA_HEADS
RET_W = RET_HEADS * DK
NAT_W = 4 * RET_W + 2 * D_MODEL
GATE_ROWS = 32
T_ROWS = NSA_Q_W + 6 * NSA_KV_W + GATE_ROWS

KT = 512
NB = KT // SLC_BLOCK
KA = 128
VROWS = DH + 16
BIAS_ROWS = 16

PROJ_TM = 256
POST_TM = 512
RET_TS = 1024
FF_CHUNK = 1024

VMEM_LIMIT = 56 * 1024 * 1024

F32 = jnp.float32
BF16 = jnp.bfloat16


def _dot(a, b):
    return jnp.dot(a, b, preferred_element_type=F32)


def _dot_nt(a, b):
    return lax.dot_general(a, b, (((1,), (1,)), ((), ())), preferred_element_type=F32)


def _dot_tn(a, b):
    return lax.dot_general(a, b, (((0,), (0,)), ((), ())), preferred_element_type=F32)


def _rms_rows(x, g):
    y = x * lax.rsqrt(jnp.mean(x * x, axis=-1, keepdims=True) + EPS)
    return y * g


def _params(n_axes):
    return pltpu.CompilerParams(dimension_semantics=("arbitrary",) * n_axes,
                                vmem_limit_bytes=VMEM_LIMIT)


def _proj_kernel(x_ref, g_ref, wn_ref, wt_ref, qg_ref, kg_ref,
                 zr_ref, zv_ref, zm_ref, qT_ref, cz_ref, ksa_ref, kwa_ref, vsa_ref, vwa_ref, gT_ref):
    h = _rms_rows(x_ref[...], g_ref[...]).astype(BF16)
    zn = _dot(h, wn_ref[...])
    zr_ref[...] = zn[:, 0:3 * RET_W]
    zv_ref[...] = zn[:, 3 * RET_W:4 * RET_W].astype(BF16)
    zm_ref[...] = jax.nn.sigmoid(zn[:, 4 * RET_W:]).astype(BF16)
    zt = _dot_nt(wt_ref[...], h)
    qg = qg_ref[...]
    kg = kg_ref[...]

    def norm_cols(z, g):
        return z * lax.rsqrt(jnp.mean(z * z, axis=0, keepdims=True) + EPS) * g

    for hh in range(NSA_HEADS):
        rows = slice(hh * DH, (hh + 1) * DH)
        qT_ref[rows, :] = (norm_cols(zt[rows, :], qg) * (DH ** -0.5 * LOG2E)).astype(BF16)
    tm = zt.shape[1]
    zpad = jnp.zeros((KA - DH, tm), F32)
    o = NSA_Q_W
    for j in range(2 * NG):
        rows = slice(o + j * DH, o + (j + 1) * DH)
        cz_ref[j] = jnp.concatenate([zt[rows, :], zpad], axis=0).T[:, 0:DH]
    o += 2 * NSA_KV_W
    lane_row = lax.broadcasted_iota(jnp.int32, (KA - DH, tm), 0)
    pos = pl.program_id(0) * tm + lax.broadcasted_iota(jnp.int32, (KA - DH, tm), 1)
    ind = jnp.where(lane_row == ((pos // SLC_BLOCK) & (NB - 1)), 1.0, 0.0)
    for j, ref in enumerate((ksa_ref, ksa_ref, kwa_ref, kwa_ref)):
        rows = slice(o + j * DH, o + (j + 1) * DH)
        ref[j % NG] = jnp.concatenate([norm_cols(zt[rows, :], kg), ind], axis=0).T.astype(BF16)
    o += 2 * NSA_KV_W
    tail_row = lax.broadcasted_iota(jnp.int32, (VROWS - DH, tm), 0)
    vtail = jnp.where(tail_row == 0, 1.0, 0.0)
    for j, ref in enumerate((vsa_ref, vsa_ref, vwa_ref, vwa_ref)):
        rows = slice(o + j * DH, o + (j + 1) * DH)
        va = jnp.concatenate([zt[rows, :], vtail], axis=0).astype(BF16)
        for u in range(tm // QB):
            ref[j % NG, u] = va[:, u * QB:(u + 1) * QB]
    o += 2 * NSA_KV_W
    gT_ref[...] = jax.nn.sigmoid(zt[o:o + GATE_ROWS, :])


def _proj(x2, g, wn, wt, qg, kg, *, tm):
    n = x2.shape[0]
    full = lambda i: (0, 0)
    colblk = lambda i: (0, i)
    rowblk3 = lambda i: (0, i, 0)
    rowblk4 = lambda i: (0, i, 0, 0)
    return pl.pallas_call(
        _proj_kernel,
        grid=(n // tm,),
        in_specs=[
            pl.BlockSpec((tm, D_MODEL), lambda i: (i, 0)),
            pl.BlockSpec((1, D_MODEL), full),
            pl.BlockSpec((D_MODEL, NAT_W), full),
            pl.BlockSpec((T_ROWS, D_MODEL), full),
            pl.BlockSpec((DH, 1), full),
            pl.BlockSpec((DH, 1), full),
        ],
        out_specs=[
            pl.BlockSpec((tm, 3 * RET_W), lambda i: (i, 0)),
            pl.BlockSpec((tm, RET_W), lambda i: (i, 0)),
            pl.BlockSpec((tm, 2 * D_MODEL), lambda i: (i, 0)),
            pl.BlockSpec((NSA_Q_W, tm), colblk),
            pl.BlockSpec((2 * NG, tm, DH), rowblk3),
            pl.BlockSpec((NG, tm, KA), rowblk3),
            pl.BlockSpec((NG, tm, KA), rowblk3),
            pl.BlockSpec((NG, tm // QB, VROWS, QB), rowblk4),
            pl.BlockSpec((NG, tm // QB, VROWS, QB), rowblk4),
            pl.BlockSpec((GATE_ROWS, tm), colblk),
        ],
        out_shape=[
            jax.ShapeDtypeStruct((n, 3 * RET_W), F32),
            jax.ShapeDtypeStruct((n, RET_W), BF16),
            jax.ShapeDtypeStruct((n, 2 * D_MODEL), BF16),
            jax.ShapeDtypeStruct((NSA_Q_W, n), BF16),
            jax.ShapeDtypeStruct((2 * NG, n, DH), F32),
            jax.ShapeDtypeStruct((NG, n, KA), BF16),
            jax.ShapeDtypeStruct((NG, n, KA), BF16),
            jax.ShapeDtypeStruct((NG, n // QB, VROWS, QB), BF16),
            jax.ShapeDtypeStruct((NG, n // QB, VROWS, QB), BF16),
            jax.ShapeDtypeStruct((GATE_ROWS, n), F32),
        ],
        compiler_params=_params(1),
        name="proj",
    )(x2, g, wn, wt, qg, kg)


def _gelu_tanh(x):
    cdf = 0.5 * (1.0 + jnp.tanh(math.sqrt(2.0 / math.pi) * (x + 0.044715 * (x ** 3))))
    return x * cdf


def _cmp_kernel(zk_ref, zv_ref, lk_ref, lv_ref, pk_ref, pv_ref, w1k_ref, w2k_ref, w1v_ref, w2vT_ref, kg_ref,
                kc_ref, vcT_ref, *, nc):
    half = CMP_STRIDE * DH
    last_row = lax.broadcasted_iota(jnp.int32, (nc, CMP_HIDDEN), 0) == nc - 1

    def hidden(z_ref, l_ref, p_ref, w1_ref):
        z = z_ref[0, 0]
        first = _dot((z + p_ref[:, 0:half]).astype(BF16), w1_ref[0:half, :])
        second = _dot((z + p_ref[:, half:2 * half]).astype(BF16), w1_ref[half:2 * half, :])
        clamp = _dot((l_ref[0, 0] + p_ref[:, half:2 * half]).astype(BF16), w1_ref[half:2 * half, :])
        second = jnp.where(last_row, clamp[0:1, :], pltpu.roll(second, nc - 1, 0))
        return _gelu_tanh(first + second)

    hk = hidden(zk_ref, lk_ref, pk_ref, w1k_ref).astype(BF16)
    k = _dot(hk, w2k_ref[...])
    k = k * lax.rsqrt(jnp.sum(k * k, axis=-1, keepdims=True) * (1.0 / DH) + EPS) * kg_ref[...]
    row = lax.broadcasted_iota(jnp.int32, (nc, KA), 0)
    lane = lax.broadcasted_iota(jnp.int32, (nc, KA), 1)
    kc_ref[0, 0] = (k + jnp.where(lane - DH == (row >> 3), 1.0, 0.0)).astype(BF16)
    hv = hidden(zv_ref, lv_ref, pv_ref, w1v_ref).astype(BF16)
    vcT_ref[0, 0] = _dot_nt(w2vT_ref[...], hv).astype(BF16)


def _compress(cz, last, pk, pv, w1k, w2k, w1v, w2vT, kg, *, nc):
    b = cz.shape[1]
    half = CMP_STRIDE * DH
    full = lambda i, j: (0, 0)
    return pl.pallas_call(
        functools.partial(_cmp_kernel, nc=nc),
        grid=(b, NG),
        in_specs=[
            pl.BlockSpec((1, 1, nc, half), lambda i, j: (j, i, 0, 0)),
            pl.BlockSpec((1, 1, nc, half), lambda i, j: (NG + j, i, 0, 0)),
            pl.BlockSpec((1, 1, 8, half), lambda i, j: (j, i, 0, 0)),
            pl.BlockSpec((1, 1, 8, half), lambda i, j: (NG + j, i, 0, 0)),
            pl.BlockSpec((1, 2 * half), full),
            pl.BlockSpec((1, 2 * half), full),
            pl.BlockSpec((2 * half, CMP_HIDDEN), full),
            pl.BlockSpec((CMP_HIDDEN, KA), full),
            pl.BlockSpec((2 * half, CMP_HIDDEN), full),
            pl.BlockSpec((DH, CMP_HIDDEN), full),
            pl.BlockSpec((1, KA), full),
        ],
        out_specs=[
            pl.BlockSpec((1, 1, nc, KA), lambda i, j: (i, j, 0, 0)),
            pl.BlockSpec((1, 1, DH, nc), lambda i, j: (i, j, 0, 0)),
        ],
        out_shape=[
            jax.ShapeDtypeStruct((b, NG, nc, KA), BF16),
            jax.ShapeDtypeStruct((b, NG, DH, nc), BF16),
        ],
        compiler_params=_params(2),
        name="compress",
    )(cz, cz, last, last, pk, pv, w1k, w2k, w1v, w2vT, kg)


def _nsa_kernel(q_ref, qn_ref, kc_ref, vcT_ref, ks_ref, vsT_ref, kw_ref, vwT_ref, gate_ref, out_ref,
                qz_ref, qw_ref, qs0_ref, qs1_ref, qs2_ref, qc_ref, sc_ref, ps_ref, score_ref, sel_ref, seln_ref,
                ms_ref, as_ref, sw_ref, aw_ref, sd_ref, oc_ref, ocn_ref,
                s0_ref, s1_ref, s2_ref, mx0_ref, mx1_ref, mx2_ref,
                *, nc, ns, n_sel):
    c = pl.program_id(2)
    ratio = SLC_BLOCK // CMP_STRIDE
    W = HG * QB
    nwt = WINDOW // QB

    def heads_on_lanes(ref):
        return jnp.concatenate([ref[h * DH:(h + 1) * DH, :] for h in range(HG)], axis=1)

    def set_bias(ref, rows):
        b = jnp.concatenate([rows] * HG, axis=1)
        ref[DH:DH + BIAS_ROWS, :] = jnp.concatenate(
            [b, jnp.zeros((BIAS_ROWS - NB, W), F32)], axis=0).astype(BF16)

    def compressed_scores(ct, qt, rows):
        grp = lax.broadcasted_iota(jnp.int32, (KA - DH, W), 0)
        qc_ref[0:DH, :] = qt
        qc_ref[DH:KA, :] = jnp.where(grp <= ct, 0.0, MASK_VALUE).astype(BF16)
        sc_ref[0:rows, :] = _dot(kc_ref[0, 0, 0:rows, :], qc_ref[...])

    def select_blocks(ct, rows):
        r0 = pl.multiple_of(jnp.maximum(ct - 1, 0) * 8, 8)
        row = r0 + lax.broadcasted_iota(jnp.int32, (16, W), 0)
        tcol = ct * QB + (lax.broadcasted_iota(jnp.int32, (16, W), 1) & (QB - 1))
        edge = sc_ref[pl.ds(r0, 16), :]
        sc_ref[pl.ds(r0, 16), :] = jnp.where(row * CMP_STRIDE + (CMP_BLOCK - 1) <= tcol, edge, MASK_VALUE)
        s = sc_ref[0:rows, :]
        e = jnp.exp2(s - jnp.max(s, axis=0, keepdims=True))
        t1 = ct * QB + (lax.broadcasted_iota(jnp.int32, (1, W), 1) & (QB - 1))
        inv = jnp.where(t1 >= CMP_BLOCK - 1, 1.0 / jnp.maximum(jnp.sum(e, axis=0, keepdims=True), 1e-30), 0.0)
        ocn_ref[...] = _dot(vcT_ref[0, 0, :, 0:rows], e.astype(BF16)) * inv

        psum = e[:, 0:QB] * inv[:, 0:QB]
        for h in range(1, HG):
            psum = psum + e[:, h * QB:(h + 1) * QB] * inv[:, h * QB:(h + 1) * QB]
        ps_ref[0:8, :] = jnp.zeros((8, QB), F32)
        ps_ref[8:8 + rows, :] = psum
        if rows < nc:
            ps_ref[8 + rows:8 + nc, :] = jnp.zeros((nc - rows, QB), F32)
        imp = ps_ref[pl.ds(8, ns, stride=ratio), :]
        for r in range(1, ratio):
            imp = imp + ps_ref[pl.ds(8 + r, ns, stride=ratio), :]
        imp = imp + ps_ref[pl.ds(7, ns, stride=ratio), :]
        blk = lax.broadcasted_iota(jnp.int32, (ns, QB), 0)
        tq = ct * QB + lax.broadcasted_iota(jnp.int32, (ns, QB), 1)
        cur = tq // SLC_BLOCK
        score = jnp.where(blk == cur, FORCE_SCORE, imp)
        score = jnp.where(blk == cur - 1, 2 * FORCE_SCORE, score)
        score = jnp.where(blk == 0, 3 * FORCE_SCORE, score)
        score = jnp.where(blk > cur, MASK_VALUE, score)
        score_ref[...] = score

        before_diag = blk < 2 * ct
        work = jnp.where((blk == 0) | (blk == cur) | (blk == cur - 1), -jnp.inf, score)
        thr = None
        for _ in range(n_sel - 3):
            thr = jnp.max(work, axis=0, keepdims=True)
            work = jnp.where(work == thr, -jnp.inf, work)
        fast = score >= thr
        cnt = jnp.sum(jnp.where(fast, 1.0, 0.0), axis=0, keepdims=True)
        n_tied = jnp.sum(jnp.where(cnt != float(n_sel), 1.0, 0.0))
        all_selected = 2 * ct + 2 <= n_sel
        seln_ref[...] = jnp.where(before_diag & (fast | all_selected), 1.0, 0.0)
        return jnp.logical_and(jnp.logical_not(all_selected), n_tied > 0.0)

    def rank_blocks(ct):
        score = score_ref[...]
        blk = lax.broadcasted_iota(jnp.int32, (ns, QB), 0)

        def rank_body(jp, rank):
            other = jnp.broadcast_to(score_ref[pl.ds(jp, 1), :], (ns, QB))
            beats = (other > score) | ((other == score) & (jp < blk))
            return rank + jnp.where(beats, 1.0, 0.0)

        rank = lax.fori_loop(0, jnp.minimum(2 * ct + 2, ns), rank_body, jnp.zeros((ns, QB), F32))
        seln_ref[...] = jnp.where((blk < 2 * ct) & (rank < float(n_sel)), 1.0, 0.0)

    @pl.when(c == 0)
    def _():
        compressed_scores(c, heads_on_lanes(q_ref), QB)
        pl.when(select_blocks(c, QB))(lambda: rank_blocks(c))

    sel_ref[...] = seln_ref[...]
    oc_ref[...] = ocn_ref[...]
    q = heads_on_lanes(q_ref)
    qz_ref[...] = jnp.concatenate([q, jnp.zeros((KA - DH, W), BF16)], axis=0)
    for r in (qw_ref, qs0_ref, qs1_ref, qs2_ref):
        r[0:DH, :] = q
        r[DH + BIAS_ROWS:KA, :] = jnp.zeros((KA - DH - BIAS_ROWS, W), BF16)

    def gate_rows(br):
        return jnp.concatenate(
            [jnp.broadcast_to(gate_ref[br * HG + h:br * HG + h + 1, :], (DH, QB)) for h in range(HG)],
            axis=1)

    rr = lax.broadcasted_iota(jnp.int32, (QB, QB), 0)
    cc = lax.broadcasted_iota(jnp.int32, (QB, QB), 1)

    def mask_first_tile(s, emask):
        top = jnp.concatenate(
            [jnp.where(emask, s[0:QB, h * QB:(h + 1) * QB], MASK_VALUE) for h in range(HG)], axis=1)
        return top if s.shape[0] == QB else jnp.concatenate([top, s[QB:, :]], axis=0)

    def normalised(a_ref):
        a = a_ref[...]
        return a[0:DH, :] * (1.0 / jnp.maximum(a[DH:DH + 1, :], 1e-30))

    last_tile = ns // NB - 1

    def scores(qs_ref, s_ref, mx_ref, t4):
        t4 = jnp.minimum(t4, last_tile)
        rows = sel_ref[pl.ds(pl.multiple_of(t4 * NB, NB), NB), :]
        set_bias(qs_ref, jnp.where(rows > 0.5, 0.0, MASK_VALUE))
        s = _dot(ks_ref[0,pl.ds(pl.multiple_of(t4 * KT, KT), KT), :], qs_ref[...])
        s_ref[...] = s
        mx_ref[...] = jnp.max(s, axis=0, keepdims=True)

    def accumulate(s_ref, mx_ref, t4):
        m_old = ms_ref[...]
        m_new = jnp.maximum(m_old, mx_ref[...])
        pt = jnp.exp2(s_ref[...] - m_new).astype(BF16)
        vT = jnp.concatenate([vsT_ref[0,(KT // QB) * t4 + i] for i in range(KT // QB)], axis=1)
        as_ref[...] = jnp.exp2(m_old - m_new) * as_ref[...] + _dot(vT, pt)
        ms_ref[...] = m_new

    nxt = jnp.minimum(c + 1, pl.num_programs(2) - 1)

    def straight_line_part(rows):
        compressed_scores(nxt, heads_on_lanes(qn_ref), rows)

        scores(qs0_ref, s0_ref, mx0_ref, 0)
        scores(qs1_ref, s1_ref, mx1_ref, 1)

        diag = pl.ds(pl.multiple_of(c * QB, QB), QB)
        qz = qz_ref[...]
        kt0 = jnp.maximum(c - nwt, 0)
        b8 = lax.broadcasted_iota(jnp.int32, (NB, QB), 0)
        tile_of_col = ((b8 - 2 * kt0) & (NB - 1)) >> 1
        set_bias(qw_ref, jnp.where(tile_of_col < jnp.minimum(c, nwt), 0.0, MASK_VALUE))
        kwc = kw_ref[0, pl.ds(pl.multiple_of(kt0 * QB, QB), nwt * QB), :]
        sw_ref[0:nwt * QB, :] = mask_first_tile(_dot(kwc, qw_ref[...]), (rr > cc) | (c < nwt))
        sw_ref[nwt * QB:, :] = mask_first_tile(_dot(kw_ref[0, diag, :], qz), rr <= cc)
        sd_ref[...] = mask_first_tile(_dot(ks_ref[0, diag, :], qz), rr <= cc)

        sw = sw_ref[...]
        pw = jnp.exp2(sw - jnp.max(sw, axis=0, keepdims=True)).astype(BF16)
        vw = jnp.concatenate([vwT_ref[0, kt0 + i] for i in range(nwt)] + [vwT_ref[0, c]], axis=1)
        aw_ref[...] = _dot(vw, pw)
        sd = sd_ref[...]
        md = jnp.max(sd, axis=0, keepdims=True)
        ms_ref[...] = md
        as_ref[...] = _dot(vsT_ref[0, c], jnp.exp2(sd - md).astype(BF16))
        pl.when(select_blocks(nxt, rows))(lambda: rank_blocks(nxt))

    groups_needed = (nxt * (QB // CMP_STRIDE)) // QB + 1
    for k in range(1, nc // QB + 1):
        pl.when(groups_needed == k)(functools.partial(straight_line_part, k * QB))

    def three_tiles(t):
        scores(qs2_ref, s2_ref, mx2_ref, t + 2)
        accumulate(s0_ref, mx0_ref, t)
        scores(qs0_ref, s0_ref, mx0_ref, t + 3)
        accumulate(s1_ref, mx1_ref, t + 1)
        scores(qs1_ref, s1_ref, mx1_ref, t + 4)
        accumulate(s2_ref, mx2_ref, t + 2)

    def sel_body(j, carry):
        three_tiles(6 * j)
        three_tiles(6 * j + 3)
        return carry

    n_tiles = (c * QB + KT - 1) // KT
    trips = jnp.maximum(n_tiles - 1, 0) // 6
    lax.fori_loop(0, trips, sel_body, 0)
    half_trip = n_tiles - 6 * trips > 3

    @pl.when(half_trip)
    def _():
        three_tiles(6 * trips)

    t_left = 6 * trips + jnp.where(half_trip, 3, 0)
    left = n_tiles - t_left

    @pl.when(left == 1)
    def _():
        accumulate(s0_ref, mx0_ref, t_left)

    @pl.when(left == 2)
    def _():
        accumulate(s0_ref, mx0_ref, t_left)
        accumulate(s1_ref, mx1_ref, t_left + 1)

    @pl.when(left == 3)
    def _():
        scores(qs2_ref, s2_ref, mx2_ref, t_left + 2)
        accumulate(s0_ref, mx0_ref, t_left)
        accumulate(s1_ref, mx1_ref, t_left + 1)
        accumulate(s2_ref, mx2_ref, t_left + 2)

    o = gate_rows(0) * oc_ref[...] + gate_rows(1) * normalised(as_ref) + gate_rows(2) * normalised(aw_ref)
    for hp in range(HG // 2):
        pair = jnp.concatenate([o[:, (2 * hp) * QB:(2 * hp + 1) * QB],
                                o[:, (2 * hp + 1) * QB:(2 * hp + 2) * QB]], axis=0)
        out_ref[:, hp * 2 * DH:(hp + 1) * 2 * DH] = pair.T.astype(BF16)


def _nsa(qT, kc, vcT, ks, vsT, kw, vwT, gates, *, b, s):
    assert s % (2 * KT) == 0 and s >= WINDOW + QB and s // CMP_STRIDE // 8 <= KA - DH
    nq = s // QB
    nc = s // CMP_STRIDE
    ns = s // SLC_BLOCK
    n_sel = min(N_SELECT, ns)
    W = HG * QB
    bg = lambda i, j, c: (i, j, 0, 0)
    keys = pl.BlockSpec((1, s, KA), lambda i, j, c: (j, i, 0))
    values = pl.BlockSpec((1, nq, VROWS, QB), lambda i, j, c: (j, i, 0, 0))
    qa = pltpu.VMEM((KA, W), BF16)
    return pl.pallas_call(
        functools.partial(_nsa_kernel, nc=nc, ns=ns, n_sel=n_sel),
        grid=(b, NG, nq),
        in_specs=[
            pl.BlockSpec((HG * DH, QB), lambda i, j, c: (j, i * nq + c)),
            pl.BlockSpec((HG * DH, QB), lambda i, j, c: (j, i * nq + jnp.minimum(c + 1, nq - 1))),
            pl.BlockSpec((1, 1, nc, KA), bg),
            pl.BlockSpec((1, 1, DH, nc), bg),
            keys, values, keys, values,
            pl.BlockSpec((GATE_ROWS // NG, QB), lambda i, j, c: (j, i * nq + c)),
        ],
        out_specs=pl.BlockSpec((QB, HG * DH), lambda i, j, c: (i * nq + c, j)),
        out_shape=jax.ShapeDtypeStruct((b * s, NSA_Q_W), BF16),
        scratch_shapes=[
            qa, qa, qa, qa, qa, qa,
            pltpu.VMEM((nc, W), F32),
            pltpu.VMEM((8 + nc, QB), F32),
            pltpu.VMEM((ns, QB), F32),
            pltpu.VMEM((ns, QB), F32),
            pltpu.VMEM((ns, QB), F32),
            pltpu.VMEM((1, W), F32),
            pltpu.VMEM((VROWS, W), F32),
            pltpu.VMEM((WINDOW + QB, W), F32),
            pltpu.VMEM((VROWS, W), F32),
            pltpu.VMEM((QB, W), F32),
            pltpu.VMEM((DH, W), F32),
            pltpu.VMEM((DH, W), F32),
            pltpu.VMEM((KT, W), F32),
            pltpu.VMEM((KT, W), F32),
            pltpu.VMEM((KT, W), F32),
            pltpu.VMEM((1, W), F32),
            pltpu.VMEM((1, W), F32),
            pltpu.VMEM((1, W), F32),
        ],
        compiler_params=_params(3),
        name="nsa",
    )(qT, qT, kc, vcT, ks, vsT, kw, vwT, gates)


def _ret_kernel(q_ref, k_ref, v_ref, g_ref, cos_ref, sin_ref, decay_ref, xi_ref, zeta_ref, gch_ref,
                o_ref, r_ref, *, nchunk):
    @pl.when(pl.program_id(1) == 0)
    def _():
        r_ref[...] = jnp.zeros((RET_HEADS, DK, DK), F32)

    for ci in range(nchunk):
        rows = slice(ci * RET_CHUNK, (ci + 1) * RET_CHUNK)
        cosv = cos_ref[rows, :]
        sinv = sin_ref[rows, :]

        def rope(x):
            return x * cosv + pltpu.roll(x, DK // 2, 1) * sinv

        for h in range(RET_HEADS):
            cols = slice(h * DK, (h + 1) * DK)
            q = rope(q_ref[rows, cols])
            k = rope(k_ref[rows, cols]) * (DK ** -0.5)
            v = v_ref[rows, cols].astype(BF16)
            r_prev = r_ref[h]
            inner = _dot_nt(q.astype(BF16), k.astype(BF16)) * decay_ref[h]
            y = _dot(inner.astype(BF16), v) + _dot((q * xi_ref[h]).astype(BF16), r_prev.astype(BF16))
            y = y * lax.rsqrt(jnp.mean(y * y, axis=-1, keepdims=True) + EPS)
            g = g_ref[rows, cols]
            o_ref[rows, cols] = (g * jax.nn.sigmoid(g) * y).astype(BF16)
            r_ref[h] = gch_ref[h, 0:1, :] * r_prev + _dot_tn((k * zeta_ref[h]).astype(BF16), v)


def _retention(zr, zv, cosf, sinf, decay, xi, zeta, gch, *, b, s, ts):
    nst = s // ts
    col = lambda blk: (lambda i, j: (i * nst + j, blk))
    tab = lambda i, j: (j, 0)
    whole = lambda i, j: (0, 0, 0)
    return pl.pallas_call(
        functools.partial(_ret_kernel, nchunk=ts // RET_CHUNK),
        grid=(b, nst),
        in_specs=[
            pl.BlockSpec((ts, RET_W), col(0)),
            pl.BlockSpec((ts, RET_W), col(1)),
            pl.BlockSpec((ts, RET_W), col(0)),
            pl.BlockSpec((ts, RET_W), col(2)),
            pl.BlockSpec((ts, DK), tab),
            pl.BlockSpec((ts, DK), tab),
            pl.BlockSpec((RET_HEADS, RET_CHUNK, RET_CHUNK), whole),
            pl.BlockSpec((RET_HEADS, RET_CHUNK, DK), whole),
            pl.BlockSpec((RET_HEADS, RET_CHUNK, DK), whole),
            pl.BlockSpec((RET_HEADS, 8, DK), whole),
        ],
        out_specs=pl.BlockSpec((ts, RET_W), lambda i, j: (i * nst + j, 0)),
        out_shape=jax.ShapeDtypeStruct((b * s, RET_W), BF16),
        scratch_shapes=[pltpu.VMEM((RET_HEADS, DK, DK), F32)],
        compiler_params=_params(2),
        name="retention",
    )(zr, zr, zv, zr, cosf, sinf, decay, xi, zeta, gch)


def _post_kernel(x_ref, ya_ref, yb_ref, ma_ref, mb_ref, p_ref, wa_ref, wb_ref, wo_ref,
                 gm_ref, w1_ref, w2_ref, gp_ref, wg_ref, wp_ref, o_ref, *, ff_chunk):
    ua = _dot(ya_ref[...], wa_ref[...])
    ub = _dot(yb_ref[...], wb_ref[...])
    mix = ma_ref[...].astype(F32) * ua + mb_ref[...].astype(F32) * ub
    x = x_ref[...] + _dot(mix.astype(BF16), wo_ref[...])
    h = _rms_rows(x, gm_ref[...]).astype(BF16)
    for j in range(D_FF // ff_chunk):
        cols = slice(j * ff_chunk, (j + 1) * ff_chunk)
        u = jnp.square(jnp.maximum(_dot(h, w1_ref[:, cols]), 0.0)).astype(BF16)
        x = x + _dot(u, w2_ref[cols, :])
    gate = jax.nn.sigmoid(_dot(_rms_rows(x, gp_ref[...]).astype(BF16), wg_ref[...]))
    o_ref[...] = x + gate * _dot(p_ref[...].astype(BF16), wp_ref[...])


def _post(x2, ya, yb, zm, p2, wa, wb, wo, gm, w1, w2, gp, wg, wp, *, tm):
    n = x2.shape[0]
    full = lambda i: (0, 0)
    once = pl.Buffered(1)
    return pl.pallas_call(
        functools.partial(_post_kernel, ff_chunk=FF_CHUNK),
        grid=(n // tm,),
        in_specs=[
            pl.BlockSpec((tm, D_MODEL), lambda i: (i, 0)),
            pl.BlockSpec((tm, NSA_Q_W), lambda i: (i, 0)),
            pl.BlockSpec((tm, RET_W), lambda i: (i, 0)),
            pl.BlockSpec((tm, D_MODEL), lambda i: (i, 0)),
            pl.BlockSpec((tm, D_MODEL), lambda i: (i, 1)),
            pl.BlockSpec((tm, PLE_DIM), lambda i: (i, 0)),
            pl.BlockSpec((NSA_Q_W, D_MODEL), full, pipeline_mode=once),
            pl.BlockSpec((RET_W, D_MODEL), full, pipeline_mode=once),
            pl.BlockSpec((D_MODEL, D_MODEL), full, pipeline_mode=once),
            pl.BlockSpec((1, D_MODEL), full),
            pl.BlockSpec((D_MODEL, D_FF), full, pipeline_mode=once),
            pl.BlockSpec((D_FF, D_MODEL), full, pipeline_mode=once),
            pl.BlockSpec((1, D_MODEL), full),
            pl.BlockSpec((D_MODEL, D_MODEL), full, pipeline_mode=once),
            pl.BlockSpec((PLE_DIM, D_MODEL), full, pipeline_mode=once),
        ],
        out_specs=pl.BlockSpec((tm, D_MODEL), lambda i: (i, 0)),
        out_shape=jax.ShapeDtypeStruct((n, D_MODEL), F32),
        compiler_params=_params(1),
        name="post",
    )(x2, ya, yb, zm, zm, p2, wa, wb, wo, gm, w1, w2, gp, wg, wp)


def _retention_tables(s):
    half = DK // 2
    pos = jnp.arange(s, dtype=F32)
    inv = ROPE_BASE ** (-jnp.arange(half, dtype=F32) / half)
    ang = pos[:, None] * inv[None, :]
    cos, sin = jnp.cos(ang), jnp.sin(ang)
    cosf = jnp.concatenate([cos, cos], axis=-1)
    sinf = jnp.concatenate([-sin, sin], axis=-1)
    C = RET_CHUNK
    gamma = 1.0 - 2.0 ** (-5.0 - jnp.arange(RET_HEADS, dtype=F32))
    lg = jnp.log(gamma)
    n = jnp.arange(C, dtype=F32)
    diff = n[:, None] - n[None, :]
    decay = jnp.where(diff >= 0, jnp.exp(lg[:, None, None] * jnp.maximum(diff, 0.0)), 0.0)
    xi = jnp.broadcast_to(jnp.exp(lg[:, None] * (n + 1.0))[:, :, None], (RET_HEADS, C, DK))
    zeta = jnp.broadcast_to(jnp.exp(lg[:, None] * (C - 1.0 - n))[:, :, None], (RET_HEADS, C, DK))
    gch = jnp.broadcast_to(jnp.exp(lg * C)[:, None, None], (RET_HEADS, 8, DK))
    return cosf, sinf, decay, xi, zeta, gch


def _layer(x2, p2, w, tabs, *, b, s):
    nc = s // CMP_STRIDE
    zr, zv, zm, qT, cz, ksa, kwa, vsa, vwa, gT = _proj(
        x2, w["norm_mix"], w["wn"], w["wt"], w["qg"], w["kgc"], tm=PROJ_TM)
    cz = cz.reshape(2 * NG, b, s, DH)
    last = jnp.tile(cz[:, :, s - 1:s, :], (1, 1, 8, CMP_STRIDE))
    kc, vcT = _compress(cz.reshape(2 * NG, b, nc, CMP_STRIDE * DH), last, w["pos_k"], w["pos_v"],
                        w["w1k"], w["w2k"], w["w1v"], w["w2vT"], w["kgr"], nc=nc)
    ya = _nsa(qT, kc, vcT, ksa, vsa, kwa, vwa, gT, b=b, s=s)
    yb = _retention(zr, zv, *tabs, b=b, s=s, ts=min(s, RET_TS))
    return _post(x2, ya, yb, zm, p2, w["wa"], w["wb"], w["wo"], w["norm_mlp"], w["w1"], w["w2"],
                 w["norm_ple"], w["wg"], w["wp"], tm=POST_TM)


def kernel(x, p, norm_mix, w_in, nsa_q_norm, nsa_k_norm, cmp_pos_k, cmp_pos_v, cmp_w1_k, cmp_w2_k,
           cmp_w1_v, cmp_w2_v, w_up_nsa, w_up_ret, w_out, norm_mlp, w_ff1, w_ff2, norm_ple, w_ple,
           w_ple_gate):
    b, s, _ = x.shape
    depth = w_in.shape[0]
    tabs = _retention_tables(s)
    x2 = x.reshape(b * s, D_MODEL)
    q_end = NSA_Q_W
    kv = NSA_KV_W
    g_end = NSA_Q_W + 6 * kv + NSA_GATE_W
    for i in range(depth):
        wi = w_in[i]
        wg = wi[:, q_end + 6 * kv:g_end].reshape(D_MODEL, 3, NG, HG).transpose(0, 2, 1, 3)
        wg = jnp.pad(wg.reshape(D_MODEL, NG, 3 * HG), ((0, 0), (0, 0), (0, GATE_ROWS // NG - 3 * HG)))
        wt = jnp.concatenate([
            wi[:, 0:q_end + 2 * kv],
            wi[:, q_end + 2 * kv:q_end + 3 * kv], wi[:, q_end + 4 * kv:q_end + 5 * kv],
            wi[:, q_end + 3 * kv:q_end + 4 * kv], wi[:, q_end + 5 * kv:q_end + 6 * kv],
            wg.reshape(D_MODEL, GATE_ROWS)], axis=1).T.astype(BF16)
        w = dict(
            norm_mix=norm_mix[i][None, :], wt=wt,
            wn=jnp.concatenate([wi[:, g_end:g_end + 2 * RET_W], wi[:, g_end + 3 * RET_W:g_end + 4 * RET_W],
                                wi[:, g_end + 2 * RET_W:g_end + 3 * RET_W], wi[:, g_end + 4 * RET_W:]],
                               axis=1).astype(BF16),
            qg=nsa_q_norm[i][:, None], kgc=nsa_k_norm[i][:, None],
            kgr=jnp.pad(nsa_k_norm[i][None, :], ((0, 0), (0, KA - DH))),
            pos_k=cmp_pos_k[i].reshape(1, CMP_BLOCK * DH), pos_v=cmp_pos_v[i].reshape(1, CMP_BLOCK * DH),
            w1k=cmp_w1_k[i].astype(BF16), w2k=jnp.pad(cmp_w2_k[i], ((0, 0), (0, KA - DH))).astype(BF16),
            w1v=cmp_w1_v[i].astype(BF16), w2vT=cmp_w2_v[i].T.astype(BF16),
            wa=w_up_nsa[i].astype(BF16), wb=w_up_ret[i].astype(BF16), wo=w_out[i].astype(BF16),
            norm_mlp=norm_mlp[i][None, :], w1=w_ff1[i].astype(BF16), w2=w_ff2[i].astype(BF16),
            norm_ple=norm_ple[i][None, :], wg=w_ple_gate[i].astype(BF16), wp=w_ple[i].astype(BF16),
        )
        x2 = _layer(x2, p[i].reshape(b * s, PLE_DIM), w, tabs, b=b, s=s)
    return x2.reshape(b, s, D_MODEL)
```

```python
import functools
import math

import jax
import jax.numpy as jnp
from jax import lax
from jax.experimental import pallas as pl
from jax.experimental.pallas import tpu as pltpu

D_MODEL = 1024
PLE_DIM = 256
NSA_HEADS = 8
DH = 64
NG = 2
HG = NSA_HEADS // NG
CMP_BLOCK = 32
CMP_STRIDE = 16
CMP_HIDDEN = 4 * DH
SLC_BLOCK = 64
N_SELECT = 16
WINDOW = 512
QB = 128
RET_HEADS = 4
DK = 128
RET_CHUNK = 128
ROPE_BASE = 10000.0
D_FF = 4 * D_MODEL
EPS = 1e-6
MASK_VALUE = -1e30
FORCE_SCORE = 1e4
LOG2E = math.log2(math.e)

NSA_Q_W = NSA_HEADS * DH
NSA_KV_W = NG * DH
NSA_GATE_W = 3 * NSA_HEADS
RET_W = RET_HEADS * DK
NAT_W = 4 * RET_W + 2 * D_MODEL
GATE_ROWS = 32
T_ROWS = NSA_Q_W + 6 * NSA_KV_W + GATE_ROWS

KT = 512
NB = KT // SLC_BLOCK
KA = 128
VROWS = DH + 16
BIAS_ROWS = 16

PROJ_TM = 256
POST_TM = 512
RET_TS = 1024
FF_CHUNK = 1024

VMEM_LIMIT = 56 * 1024 * 1024

F32 = jnp.float32
BF16 = jnp.bfloat16


def _dot(a, b):
    return jnp.dot(a, b, preferred_element_type=F32)


def _dot_nt(a, b):
    return lax.dot_general(a, b, (((1,), (1,)), ((), ())), preferred_element_type=F32)


def _dot_tn(a, b):
    return lax.dot_general(a, b, (((0,), (0,)), ((), ())), preferred_element_type=F32)


def _rms_rows(x, g):
    y = x * lax.rsqrt(jnp.mean(x * x, axis=-1, keepdims=True) + EPS)
    return y * g


def _params(n_axes):
    return pltpu.CompilerParams(dimension_semantics=("arbitrary",) * n_axes,
                                vmem_limit_bytes=VMEM_LIMIT)


def _proj_kernel(x_ref, g_ref, wn_ref, wt_ref, qg_ref, kg_ref,
                 zr_ref, zv_ref, zm_ref, qT_ref, cz_ref, ksa_ref, kwa_ref, vsa_ref, vwa_ref, gT_ref):
    h = _rms_rows(x_ref[...], g_ref[...]).astype(BF16)
    zn = _dot(h, wn_ref[...])
    zr_ref[...] = zn[:, 0:3 * RET_W]
    zv_ref[...] = zn[:, 3 * RET_W:4 * RET_W].astype(BF16)
    zm_ref[...] = jax.nn.sigmoid(zn[:, 4 * RET_W:]).astype(BF16)
    zt = _dot_nt(wt_ref[...], h)
    qg = qg_ref[...]
    kg = kg_ref[...]

    def norm_cols(z, g):
        return z * lax.rsqrt(jnp.mean(z * z, axis=0, keepdims=True) + EPS) * g

    for hh in range(NSA_HEADS):
        rows = slice(hh * DH, (hh + 1) * DH)
        qT_ref[rows, :] = (norm_cols(zt[rows, :], qg) * (DH ** -0.5 * LOG2E)).astype(BF16)
    tm = zt.shape[1]
    zpad = jnp.zeros((KA - DH, tm), F32)
    o = NSA_Q_W
    for j in range(2 * NG):
        rows = slice(o + j * DH, o + (j + 1) * DH)
        cz_ref[j] = jnp.concatenate([zt[rows, :], zpad], axis=0).T[:, 0:DH]
    o += 2 * NSA_KV_W
    lane_row = lax.broadcasted_iota(jnp.int32, (KA - DH, tm), 0)
    pos = pl.program_id(0) * tm + lax.broadcasted_iota(jnp.int32, (KA - DH, tm), 1)
    ind = jnp.where(lane_row == ((pos // SLC_BLOCK) & (NB - 1)), 1.0, 0.0)
    for j, ref in enumerate((ksa_ref, ksa_ref, kwa_ref, kwa_ref)):
        rows = slice(o + j * DH, o + (j + 1) * DH)
        ref[j % NG] = jnp.concatenate([norm_cols(zt[rows, :], kg), ind], axis=0).T.astype(BF16)
    o += 2 * NSA_KV_W
    tail_row = lax.broadcasted_iota(jnp.int32, (VROWS - DH, tm), 0)
    vtail = jnp.where(tail_row == 0, 1.0, 0.0)
    for j, ref in enumerate((vsa_ref, vsa_ref, vwa_ref, vwa_ref)):
        rows = slice(o + j * DH, o + (j + 1) * DH)
        va = jnp.concatenate([zt[rows, :], vtail], axis=0).astype(BF16)
        for u in range(tm // QB):
            ref[j % NG, u] = va[:, u * QB:(u + 1) * QB]
    o += 2 * NSA_KV_W
    gT_ref[...] = jax.nn.sigmoid(zt[o:o + GATE_ROWS, :])


def _proj(x2, g, wn, wt, qg, kg, *, tm):
    n = x2.shape[0]
    full = lambda i: (0, 0)
    colblk = lambda i: (0, i)
    rowblk3 = lambda i: (0, i, 0)
    rowblk4 = lambda i: (0, i, 0, 0)
    return pl.pallas_call(
        _proj_kernel,
        grid=(n // tm,),
        in_specs=[
            pl.BlockSpec((tm, D_MODEL), lambda i: (i, 0)),
            pl.BlockSpec((1, D_MODEL), full),
            pl.BlockSpec((D_MODEL, NAT_W), full),
            pl.BlockSpec((T_ROWS, D_MODEL), full),
            pl.BlockSpec((DH, 1), full),
            pl.BlockSpec((DH, 1), full),
        ],
        out_specs=[
            pl.BlockSpec((tm, 3 * RET_W), lambda i: (i, 0)),
            pl.BlockSpec((tm, RET_W), lambda i: (i, 0)),
            pl.BlockSpec((tm, 2 * D_MODEL), lambda i: (i, 0)),
            pl.BlockSpec((NSA_Q_W, tm), colblk),
            pl.BlockSpec((2 * NG, tm, DH), rowblk3),
            pl.BlockSpec((NG, tm, KA), rowblk3),
            pl.BlockSpec((NG, tm, KA), rowblk3),
            pl.BlockSpec((NG, tm // QB, VROWS, QB), rowblk4),
            pl.BlockSpec((NG, tm // QB, VROWS, QB), rowblk4),
            pl.BlockSpec((GATE_ROWS, tm), colblk),
        ],
        out_shape=[
            jax.ShapeDtypeStruct((n, 3 * RET_W), F32),
            jax.ShapeDtypeStruct((n, RET_W), BF16),
            jax.ShapeDtypeStruct((n, 2 * D_MODEL), BF16),
            jax.ShapeDtypeStruct((NSA_Q_W, n), BF16),
            jax.ShapeDtypeStruct((2 * NG, n, DH), F32),
            jax.ShapeDtypeStruct((NG, n, KA), BF16),
            jax.ShapeDtypeStruct((NG, n, KA), BF16),
            jax.ShapeDtypeStruct((NG, n // QB, VROWS, QB), BF16),
            jax.ShapeDtypeStruct((NG, n // QB, VROWS, QB), BF16),
            jax.ShapeDtypeStruct((GATE_ROWS, n), F32),
        ],
        compiler_params=_params(1),
        name="proj",
    )(x2, g, wn, wt, qg, kg)


def _gelu_tanh(x):
    cdf = 0.5 * (1.0 + jnp.tanh(math.sqrt(2.0 / math.pi) * (x + 0.044715 * (x ** 3))))
    return x * cdf


def _cmp_kernel(zk_ref, zv_ref, lk_ref, lv_ref, pk_ref, pv_ref, w1k_ref, w2k_ref, w1v_ref, w2vT_ref, kg_ref,
                kc_ref, vcT_ref, *, nc):
    half = CMP_STRIDE * DH
    last_row = lax.broadcasted_iota(jnp.int32, (nc, CMP_HIDDEN), 0) == nc - 1

    def hidden(z_ref, l_ref, p_ref, w1_ref):
        z = z_ref[0, 0]
        first = _dot((z + p_ref[:, 0:half]).astype(BF16), w1_ref[0:half, :])
        second = _dot((z + p_ref[:, half:2 * half]).astype(BF16), w1_ref[half:2 * half, :])
        clamp = _dot((l_ref[0, 0] + p_ref[:, half:2 * half]).astype(BF16), w1_ref[half:2 * half, :])
        second = jnp.where(last_row, clamp[0:1, :], pltpu.roll(second, nc - 1, 0))
        return _gelu_tanh(first + second)

    hk = hidden(zk_ref, lk_ref, pk_ref, w1k_ref).astype(BF16)
    k = _dot(hk, w2k_ref[...])
    k = k * lax.rsqrt(jnp.sum(k * k, axis=-1, keepdims=True) * (1.0 / DH) + EPS) * kg_ref[...]
    row = lax.broadcasted_iota(jnp.int32, (nc, KA), 0)
    lane = lax.broadcasted_iota(jnp.int32, (nc, KA), 1)
    kc_ref[0, 0] = (k + jnp.where(lane - DH == (row >> 3), 1.0, 0.0)).astype(BF16)
    hv = hidden(zv_ref, lv_ref, pv_ref, w1v_ref).astype(BF16)
    vcT_ref[0, 0] = _dot_nt(w2vT_ref[...], hv).astype(BF16)


def _compress(cz, last, pk, pv, w1k, w2k, w1v, w2vT, kg, *, nc):
    b = cz.shape[1]
    half = CMP_STRIDE * DH
    full = lambda i, j: (0, 0)
    return pl.pallas_call(
        functools.partial(_cmp_kernel, nc=nc),
        grid=(b, NG),
        in_specs=[
            pl.BlockSpec((1, 1, nc, half), lambda i, j: (j, i, 0, 0)),
            pl.BlockSpec((1, 1, nc, half), lambda i, j: (NG + j, i, 0, 0)),
            pl.BlockSpec((1, 1, 8, half), lambda i, j: (j, i, 0, 0)),
            pl.BlockSpec((1, 1, 8, half), lambda i, j: (NG + j, i, 0, 0)),
            pl.BlockSpec((1, 2 * half), full),
            pl.BlockSpec((1, 2 * half), full),
            pl.BlockSpec((2 * half, CMP_HIDDEN), full),
            pl.BlockSpec((CMP_HIDDEN, KA), full),
            pl.BlockSpec((2 * half, CMP_HIDDEN), full),
            pl.BlockSpec((DH, CMP_HIDDEN), full),
            pl.BlockSpec((1, KA), full),
        ],
        out_specs=[
            pl.BlockSpec((1, 1, nc, KA), lambda i, j: (i, j, 0, 0)),
            pl.BlockSpec((1, 1, DH, nc), lambda i, j: (i, j, 0, 0)),
        ],
        out_shape=[
            jax.ShapeDtypeStruct((b, NG, nc, KA), BF16),
            jax.ShapeDtypeStruct((b, NG, DH, nc), BF16),
        ],
        compiler_params=_params(2),
        name="compress",
    )(cz, cz, last, last, pk, pv, w1k, w2k, w1v, w2vT, kg)


def _nsa_kernel(q_ref, qn_ref, kc_ref, vcT_ref, ks_ref, vsT_ref, kw_ref, vwT_ref, gate_ref, out_ref,
                qz_ref, qw_ref, qs0_ref, qs1_ref, qs2_ref, qc_ref, sc_ref, ps_ref, score_ref, sel_ref, seln_ref,
                ms_ref, as_ref, sw_ref, aw_ref, sd_ref, oc_ref, ocn_ref,
                s0_ref, s1_ref, s2_ref, mx0_ref, mx1_ref, mx2_ref,
                *, nc, ns, n_sel):
    c = pl.program_id(2)
    ratio = SLC_BLOCK // CMP_STRIDE
    W = HG * QB
    nwt = WINDOW // QB

    def heads_on_lanes(ref):
        return jnp.concatenate([ref[h * DH:(h + 1) * DH, :] for h in range(HG)], axis=1)

    def set_bias(ref, rows):
        b = jnp.concatenate([rows] * HG, axis=1)
        ref[DH:DH + BIAS_ROWS, :] = jnp.concatenate(
            [b, jnp.zeros((BIAS_ROWS - NB, W), F32)], axis=0).astype(BF16)

    def compressed_scores(ct, qt, rows):
        grp = lax.broadcasted_iota(jnp.int32, (KA - DH, W), 0)
        qc_ref[0:DH, :] = qt
        qc_ref[DH:KA, :] = jnp.where(grp <= ct, 0.0, MASK_VALUE).astype(BF16)
        sc_ref[0:rows, :] = _dot(kc_ref[0, 0, 0:rows, :], qc_ref[...])

    def select_blocks(ct, rows):
        r0 = pl.multiple_of(jnp.maximum(ct - 1, 0) * 8, 8)
        row = r0 + lax.broadcasted_iota(jnp.int32, (16, W), 0)
        tcol = ct * QB + (lax.broadcasted_iota(jnp.int32, (16, W), 1) & (QB - 1))
        edge = sc_ref[pl.ds(r0, 16), :]
        sc_ref[pl.ds(r0, 16), :] = jnp.where(row * CMP_STRIDE + (CMP_BLOCK - 1) <= tcol, edge, MASK_VALUE)
        s = sc_ref[0:rows, :]
        e = jnp.exp2(s - jnp.max(s, axis=0, keepdims=True))
        t1 = ct * QB + (lax.broadcasted_iota(jnp.int32, (1, W), 1) & (QB - 1))
        inv = jnp.where(t1 >= CMP_BLOCK - 1, 1.0 / jnp.maximum(jnp.sum(e, axis=0, keepdims=True), 1e-30), 0.0)
        ocn_ref[...] = _dot(vcT_ref[0, 0, :, 0:rows], e.astype(BF16)) * inv

        psum = e[:, 0:QB] * inv[:, 0:QB]
        for h in range(1, HG):
            psum = psum + e[:, h * QB:(h + 1) * QB] * inv[:, h * QB:(h + 1) * QB]
        ps_ref[0:8, :] = jnp.zeros((8, QB), F32)
        ps_ref[8:8 + rows, :] = psum
        nsv = rows // ratio
        imp = ps_ref[pl.ds(8, nsv, stride=ratio), :]
        for r in range(1, ratio):
            imp = imp + ps_ref[pl.ds(8 + r, nsv, stride=ratio), :]
        imp = imp + ps_ref[pl.ds(7, nsv, stride=ratio), :]
        blk = lax.broadcasted_iota(jnp.int32, (nsv, QB), 0)
        tq = ct * QB + lax.broadcasted_iota(jnp.int32, (nsv, QB), 1)
        cur = tq // SLC_BLOCK
        score = jnp.where(blk == cur, FORCE_SCORE, imp)
        score = jnp.where(blk == cur - 1, 2 * FORCE_SCORE, score)
        score = jnp.where(blk == 0, 3 * FORCE_SCORE, score)
        score = jnp.where(blk > cur, MASK_VALUE, score)
        score_ref[0:nsv, :] = score

        before_diag = blk < 2 * ct
        work = jnp.where((blk == 0) | (blk == cur) | (blk == cur - 1), -jnp.inf, score)
        thr = None
        for _ in range(n_sel - 3):
            thr = jnp.max(work, axis=0, keepdims=True)
            work = jnp.where(work == thr, -jnp.inf, work)
        fast = score >= thr
        cnt = jnp.sum(jnp.where(fast, 1.0, 0.0), axis=0, keepdims=True)
        n_tied = jnp.sum(jnp.where(cnt != float(n_sel), 1.0, 0.0))
        all_selected = 2 * ct + 2 <= n_sel
        seln_ref[0:nsv, :] = jnp.where(before_diag & (fast | all_selected), 1.0, 0.0)
        if nsv < ns:
            seln_ref[nsv:ns, :] = jnp.zeros((ns - nsv, QB), F32)
        return jnp.logical_and(jnp.logical_not(all_selected), n_tied > 0.0)

    def rank_blocks(ct, rows):
        nsv = rows // ratio
        score = score_ref[0:nsv, :]
        blk = lax.broadcasted_iota(jnp.int32, (nsv, QB), 0)

        def rank_body(jp, rank):
            other = jnp.broadcast_to(score_ref[pl.ds(jp, 1), :], (nsv, QB))
            beats = (other > score) | ((other == score) & (jp < blk))
            return rank + jnp.where(beats, 1.0, 0.0)

        rank = lax.fori_loop(0, jnp.minimum(2 * ct + 2, nsv), rank_body, jnp.zeros((nsv, QB), F32))
        seln_ref[0:nsv, :] = jnp.where((blk < 2 * ct) & (rank < float(n_sel)), 1.0, 0.0)

    @pl.when(c == 0)
    def _():
        compressed_scores(c, heads_on_lanes(q_ref), QB)
        pl.when(select_blocks(c, QB))(lambda: rank_blocks(c, QB))

    sel_ref[...] = seln_ref[...]
    oc_ref[...] = ocn_ref[...]
    q = heads_on_lanes(q_ref)
    qz_ref[...] = jnp.concatenate([q, jnp.zeros((KA - DH, W), BF16)], axis=0)
    for r in (qw_ref, qs0_ref, qs1_ref, qs2_ref):
        r[0:DH, :] = q
        r[DH + BIAS_ROWS:KA, :] = jnp.zeros((KA - DH - BIAS_ROWS, W), BF16)

    def gate_rows(br):
        return jnp.concatenate(
            [jnp.broadcast_to(gate_ref[br * HG + h:br * HG + h + 1, :], (DH, QB)) for h in range(HG)],
            axis=1)

    rr = lax.broadcasted_iota(jnp.int32, (QB, QB), 0)
    cc = lax.broadcasted_iota(jnp.int32, (QB, QB), 1)

    def mask_first_tile(s, emask):
        top = jnp.concatenate(
            [jnp.where(emask, s[0:QB, h * QB:(h + 1) * QB], MASK_VALUE) for h in range(HG)], axis=1)
        return top if s.shape[0] == QB else jnp.concatenate([top, s[QB:, :]], axis=0)

    def normalised(a_ref):
        a = a_ref[...]
        return a[0:DH, :] * (1.0 / jnp.maximum(a[DH:DH + 1, :], 1e-30))

    last_tile = ns // NB - 1

    def scores(qs_ref, s_ref, mx_ref, t4):
        t4 = jnp.minimum(t4, last_tile)
        rows = sel_ref[pl.ds(pl.multiple_of(t4 * NB, NB), NB), :]
        set_bias(qs_ref, jnp.where(rows > 0.5, 0.0, MASK_VALUE))
        s = _dot(ks_ref[0,pl.ds(pl.multiple_of(t4 * KT, KT), KT), :], qs_ref[...])
        s_ref[...] = s
        mx_ref[...] = jnp.max(s, axis=0, keepdims=True)

    def accumulate(s_ref, mx_ref, t4):
        m_old = ms_ref[...]
        m_new = jnp.maximum(m_old, mx_ref[...])
        pt = jnp.exp2(s_ref[...] - m_new).astype(BF16)
        vT = jnp.concatenate([vsT_ref[0,(KT // QB) * t4 + i] for i in range(KT // QB)], axis=1)
        as_ref[...] = jnp.exp2(m_old - m_new) * as_ref[...] + _dot(vT, pt)
        ms_ref[...] = m_new

    nxt = jnp.minimum(c + 1, pl.num_programs(2) - 1)

    def straight_line_part(rows):
        compressed_scores(nxt, heads_on_lanes(qn_ref), rows)

        scores(qs0_ref, s0_ref, mx0_ref, 0)
        scores(qs1_ref, s1_ref, mx1_ref, 1)

        diag = pl.ds(pl.multiple_of(c * QB, QB), QB)
        qz = qz_ref[...]
        kt0 = jnp.maximum(c - nwt, 0)
        b8 = lax.broadcasted_iota(jnp.int32, (NB, QB), 0)
        tile_of_col = ((b8 - 2 * kt0) & (NB - 1)) >> 1
        set_bias(qw_ref, jnp.where(tile_of_col < jnp.minimum(c, nwt), 0.0, MASK_VALUE))
        kwc = kw_ref[0, pl.ds(pl.multiple_of(kt0 * QB, QB), nwt * QB), :]
        sw_ref[0:nwt * QB, :] = mask_first_tile(_dot(kwc, qw_ref[...]), (rr > cc) | (c < nwt))
        sw_ref[nwt * QB:, :] = mask_first_tile(_dot(kw_ref[0, diag, :], qz), rr <= cc)
        sd_ref[...] = mask_first_tile(_dot(ks_ref[0, diag, :], qz), rr <= cc)

        sw = sw_ref[...]
        pw = jnp.exp2(sw - jnp.max(sw, axis=0, keepdims=True)).astype(BF16)
        vw = jnp.concatenate([vwT_ref[0, kt0 + i] for i in range(nwt)] + [vwT_ref[0, c]], axis=1)
        aw_ref[...] = _dot(vw, pw)
        sd = sd_ref[...]
        md = jnp.max(sd, axis=0, keepdims=True)
        ms_ref[...] = md
        as_ref[...] = _dot(vsT_ref[0, c], jnp.exp2(sd - md).astype(BF16))
        pl.when(select_blocks(nxt, rows))(lambda: rank_blocks(nxt, rows))

    groups_needed = (nxt * (QB // CMP_STRIDE)) // QB + 1
    for k in range(1, nc // QB + 1):
        pl.when(groups_needed == k)(functools.partial(straight_line_part, k * QB))

    def three_tiles(t):
        scores(qs2_ref, s2_ref, mx2_ref, t + 2)
        accumulate(s0_ref, mx0_ref, t)
        scores(qs0_ref, s0_ref, mx0_ref, t + 3)
        accumulate(s1_ref, mx1_ref, t + 1)
        scores(qs1_ref, s1_ref, mx1_ref, t + 4)
        accumulate(s2_ref, mx2_ref, t + 2)

    def sel_body(j, carry):
        three_tiles(6 * j)
        three_tiles(6 * j + 3)
        return carry

    n_tiles = (c * QB + KT - 1) // KT
    trips = jnp.maximum(n_tiles - 1, 0) // 6
    lax.fori_loop(0, trips, sel_body, 0)
    half_trip = n_tiles - 6 * trips > 3

    @pl.when(half_trip)
    def _():
        three_tiles(6 * trips)

    t_left = 6 * trips + jnp.where(half_trip, 3, 0)
    left = n_tiles - t_left

    @pl.when(left == 1)
    def _():
        accumulate(s0_ref, mx0_ref, t_left)

    @pl.when(left == 2)
    def _():
        accumulate(s0_ref, mx0_ref, t_left)
        accumulate(s1_ref, mx1_ref, t_left + 1)

    @pl.when(left == 3)
    def _():
        scores(qs2_ref, s2_ref, mx2_ref, t_left + 2)
        accumulate(s0_ref, mx0_ref, t_left)
        accumulate(s1_ref, mx1_ref, t_left + 1)
        accumulate(s2_ref, mx2_ref, t_left + 2)

    o = gate_rows(0) * oc_ref[...] + gate_rows(1) * normalised(as_ref) + gate_rows(2) * normalised(aw_ref)
    for hp in range(HG // 2):
        pair = jnp.concatenate([o[:, (2 * hp) * QB:(2 * hp + 1) * QB],
                                o[:, (2 * hp + 1) * QB:(2 * hp + 2) * QB]], axis=0)
        out_ref[:, hp * 2 * DH:(hp + 1) * 2 * DH] = pair.T.astype(BF16)


def _nsa(qT, kc, vcT, ks, vsT, kw, vwT, gates, *, b, s):
    assert s % (2 * KT) == 0 and s >= WINDOW + QB and s // CMP_STRIDE // 8 <= KA - DH
    nq = s // QB
    nc = s // CMP_STRIDE
    ns = s // SLC_BLOCK
    n_sel = min(N_SELECT, ns)
    W = HG * QB
    bg = lambda i, j, c: (i, j, 0, 0)
    keys = pl.BlockSpec((1, s, KA), lambda i, j, c: (j, i, 0))
    values = pl.BlockSpec((1, nq, VROWS, QB), lambda i, j, c: (j, i, 0, 0))
    qa = pltpu.VMEM((KA, W), BF16)
    return pl.pallas_call(
        functools.partial(_nsa_kernel, nc=nc, ns=ns, n_sel=n_sel),
        grid=(b, NG, nq),
        in_specs=[
            pl.BlockSpec((HG * DH, QB), lambda i, j, c: (j, i * nq + c)),
            pl.BlockSpec((HG * DH, QB), lambda i, j, c: (j, i * nq + jnp.minimum(c + 1, nq - 1))),
            pl.BlockSpec((1, 1, nc, KA), bg),
            pl.BlockSpec((1, 1, DH, nc), bg),
            keys, values, keys, values,
            pl.BlockSpec((GATE_ROWS // NG, QB), lambda i, j, c: (j, i * nq + c)),
        ],
        out_specs=pl.BlockSpec((QB, HG * DH), lambda i, j, c: (i * nq + c, j)),
        out_shape=jax.ShapeDtypeStruct((b * s, NSA_Q_W), BF16),
        scratch_shapes=[
            qa, qa, qa, qa, qa, qa,
            pltpu.VMEM((nc, W), F32),
            pltpu.VMEM((8 + nc, QB), F32),
            pltpu.VMEM((ns, QB), F32),
            pltpu.VMEM((ns, QB), F32),
            pltpu.VMEM((ns, QB), F32),
            pltpu.VMEM((1, W), F32),
            pltpu.VMEM((VROWS, W), F32),
            pltpu.VMEM((WINDOW + QB, W), F32),
            pltpu.VMEM((VROWS, W), F32),
            pltpu.VMEM((QB, W), F32),
            pltpu.VMEM((DH, W), F32),
            pltpu.VMEM((DH, W), F32),
            pltpu.VMEM((KT, W), F32),
            pltpu.VMEM((KT, W), F32),
            pltpu.VMEM((KT, W), F32),
            pltpu.VMEM((1, W), F32),
            pltpu.VMEM((1, W), F32),
            pltpu.VMEM((1, W), F32),
        ],
        compiler_params=_params(3),
        name="nsa",
    )(qT, qT, kc, vcT, ks, vsT, kw, vwT, gates)


def _ret_kernel(q_ref, k_ref, v_ref, g_ref, cos_ref, sin_ref, decay_ref, xi_ref, zeta_ref, gch_ref,
                o_ref, r_ref, *, nchunk):
    @pl.when(pl.program_id(1) == 0)
    def _():
        r_ref[...] = jnp.zeros((RET_HEADS, DK, DK), F32)

    for ci in range(nchunk):
        rows = slice(ci * RET_CHUNK, (ci + 1) * RET_CHUNK)
        cosv = cos_ref[rows, :]
        sinv = sin_ref[rows, :]

        def rope(x):
            return x * cosv + pltpu.roll(x, DK // 2, 1) * sinv

        for h in range(RET_HEADS):
            cols = slice(h * DK, (h + 1) * DK)
            q = rope(q_ref[rows, cols])
            k = rope(k_ref[rows, cols]) * (DK ** -0.5)
            v = v_ref[rows, cols].astype(BF16)
            r_prev = r_ref[h]
            inner = _dot_nt(q.astype(BF16), k.astype(BF16)) * decay_ref[h]
            y = _dot(inner.astype(BF16), v) + _dot((q * xi_ref[h]).astype(BF16), r_prev.astype(BF16))
            y = y * lax.rsqrt(jnp.mean(y * y, axis=-1, keepdims=True) + EPS)
            g = g_ref[rows, cols]
            o_ref[rows, cols] = (g * jax.nn.sigmoid(g) * y).astype(BF16)
            r_ref[h] = gch_ref[h, 0:1, :] * r_prev + _dot_tn((k * zeta_ref[h]).astype(BF16), v)


def _retention(zr, zv, cosf, sinf, decay, xi, zeta, gch, *, b, s, ts):
    nst = s // ts
    col = lambda blk: (lambda i, j: (i * nst + j, blk))
    tab = lambda i, j: (j, 0)
    whole = lambda i, j: (0, 0, 0)
    return pl.pallas_call(
        functools.partial(_ret_kernel, nchunk=ts // RET_CHUNK),
        grid=(b, nst),
        in_specs=[
            pl.BlockSpec((ts, RET_W), col(0)),
            pl.BlockSpec((ts, RET_W), col(1)),
            pl.BlockSpec((ts, RET_W), col(0)),
            pl.BlockSpec((ts, RET_W), col(2)),
            pl.BlockSpec((ts, DK), tab),
            pl.BlockSpec((ts, DK), tab),
            pl.BlockSpec((RET_HEADS, RET_CHUNK, RET_CHUNK), whole),
            pl.BlockSpec((RET_HEADS, RET_CHUNK, DK), whole),
            pl.BlockSpec((RET_HEADS, RET_CHUNK, DK), whole),
            pl.BlockSpec((RET_HEADS, 8, DK), whole),
        ],
        out_specs=pl.BlockSpec((ts, RET_W), lambda i, j: (i * nst + j, 0)),
        out_shape=jax.ShapeDtypeStruct((b * s, RET_W), BF16),
        scratch_shapes=[pltpu.VMEM((RET_HEADS, DK, DK), F32)],
        compiler_params=_params(2),
        name="retention",
    )(zr, zr, zv, zr, cosf, sinf, decay, xi, zeta, gch)


def _post_kernel(x_ref, ya_ref, yb_ref, ma_ref, mb_ref, p_ref, wa_ref, wb_ref, wo_ref,
                 gm_ref, w1_ref, w2_ref, gp_ref, wg_ref, wp_ref, o_ref, *, ff_chunk):
    ua = _dot(ya_ref[...], wa_ref[...])
    ub = _dot(yb_ref[...], wb_ref[...])
    mix = ma_ref[...].astype(F32) * ua + mb_ref[...].astype(F32) * ub
    x = x_ref[...] + _dot(mix.astype(BF16), wo_ref[...])
    h = _rms_rows(x, gm_ref[...]).astype(BF16)
    for j in range(D_FF // ff_chunk):
        cols = slice(j * ff_chunk, (j + 1) * ff_chunk)
        u = jnp.square(jnp.maximum(_dot(h, w1_ref[:, cols]), 0.0)).astype(BF16)
        x = x + _dot(u, w2_ref[cols, :])
    gate = jax.nn.sigmoid(_dot(_rms_rows(x, gp_ref[...]).astype(BF16), wg_ref[...]))
    o_ref[...] = x + gate * _dot(p_ref[...].astype(BF16), wp_ref[...])


def _post(x2, ya, yb, zm, p2, wa, wb, wo, gm, w1, w2, gp, wg, wp, *, tm):
    n = x2.shape[0]
    full = lambda i: (0, 0)
    once = pl.Buffered(1)
    return pl.pallas_call(
        functools.partial(_post_kernel, ff_chunk=FF_CHUNK),
        grid=(n // tm,),
        in_specs=[
            pl.BlockSpec((tm, D_MODEL), lambda i: (i, 0)),
            pl.BlockSpec((tm, NSA_Q_W), lambda i: (i, 0)),
            pl.BlockSpec((tm, RET_W), lambda i: (i, 0)),
            pl.BlockSpec((tm, D_MODEL), lambda i: (i, 0)),
            pl.BlockSpec((tm, D_MODEL), lambda i: (i, 1)),
            pl.BlockSpec((tm, PLE_DIM), lambda i: (i, 0)),
            pl.BlockSpec((NSA_Q_W, D_MODEL), full, pipeline_mode=once),
            pl.BlockSpec((RET_W, D_MODEL), full, pipeline_mode=once),
            pl.BlockSpec((D_MODEL, D_MODEL), full, pipeline_mode=once),
            pl.BlockSpec((1, D_MODEL), full),
            pl.BlockSpec((D_MODEL, D_FF), full, pipeline_mode=once),
            pl.BlockSpec((D_FF, D_MODEL), full, pipeline_mode=once),
            pl.BlockSpec((1, D_MODEL), full),
            pl.BlockSpec((D_MODEL, D_MODEL), full, pipeline_mode=once),
            pl.BlockSpec((PLE_DIM, D_MODEL), full, pipeline_mode=once),
        ],
        out_specs=pl.BlockSpec((tm, D_MODEL), lambda i: (i, 0)),
        out_shape=jax.ShapeDtypeStruct((n, D_MODEL), F32),
        compiler_params=_params(1),
        name="post",
    )(x2, ya, yb, zm, zm, p2, wa, wb, wo, gm, w1, w2, gp, wg, wp)


def _retention_tables(s):
    half = DK // 2
    pos = jnp.arange(s, dtype=F32)
    inv = ROPE_BASE ** (-jnp.arange(half, dtype=F32) / half)
    ang = pos[:, None] * inv[None, :]
    cos, sin = jnp.cos(ang), jnp.sin(ang)
    cosf = jnp.concatenate([cos, cos], axis=-1)
    sinf = jnp.concatenate([-sin, sin], axis=-1)
    C = RET_CHUNK
    gamma = 1.0 - 2.0 ** (-5.0 - jnp.arange(RET_HEADS, dtype=F32))
    lg = jnp.log(gamma)
    n = jnp.arange(C, dtype=F32)
    diff = n[:, None] - n[None, :]
    decay = jnp.where(diff >= 0, jnp.exp(lg[:, None, None] * jnp.maximum(diff, 0.0)), 0.0)
    xi = jnp.broadcast_to(jnp.exp(lg[:, None] * (n + 1.0))[:, :, None], (RET_HEADS, C, DK))
    zeta = jnp.broadcast_to(jnp.exp(lg[:, None] * (C - 1.0 - n))[:, :, None], (RET_HEADS, C, DK))
    gch = jnp.broadcast_to(jnp.exp(lg * C)[:, None, None], (RET_HEADS, 8, DK))
    return cosf, sinf, decay, xi, zeta, gch


def _layer(x2, p2, w, tabs, *, b, s):
    nc = s // CMP_STRIDE
    zr, zv, zm, qT, cz, ksa, kwa, vsa, vwa, gT = _proj(
        x2, w["norm_mix"], w["wn"], w["wt"], w["qg"], w["kgc"], tm=PROJ_TM)
    cz = cz.reshape(2 * NG, b, s, DH)
    last = jnp.tile(cz[:, :, s - 1:s, :], (1, 1, 8, CMP_STRIDE))
    kc, vcT = _compress(cz.reshape(2 * NG, b, nc, CMP_STRIDE * DH), last, w["pos_k"], w["pos_v"],
                        w["w1k"], w["w2k"], w["w1v"], w["w2vT"], w["kgr"], nc=nc)
    ya = _nsa(qT, kc, vcT, ksa, vsa, kwa, vwa, gT, b=b, s=s)
    yb = _retention(zr, zv, *tabs, b=b, s=s, ts=min(s, RET_TS))
    return _post(x2, ya, yb, zm, p2, w["wa"], w["wb"], w["wo"], w["norm_mlp"], w["w1"], w["w2"],
                 w["norm_ple"], w["wg"], w["wp"], tm=POST_TM)


def kernel(x, p, norm_mix, w_in, nsa_q_norm, nsa_k_norm, cmp_pos_k, cmp_pos_v, cmp_w1_k, cmp_w2_k,
           cmp_w1_v, cmp_w2_v, w_up_nsa, w_up_ret, w_out, norm_mlp, w_ff1, w_ff2, norm_ple, w_ple,
           w_ple_gate):
    b, s, _ = x.shape
    depth = w_in.shape[0]
    tabs = _retention_tables(s)
    x2 = x.reshape(b * s, D_MODEL)
    q_end = NSA_Q_W
    kv = NSA_KV_W
    g_end = NSA_Q_W + 6 * kv + NSA_GATE_W
    for i in range(depth):
        wi = w_in[i]
        wg = wi[:, q_end + 6 * kv:g_end].reshape(D_MODEL, 3, NG, HG).transpose(0, 2, 1, 3)
        wg = jnp.pad(wg.reshape(D_MODEL, NG, 3 * HG), ((0, 0), (0, 0), (0, GATE_ROWS // NG - 3 * HG)))
        wt = jnp.concatenate([
            wi[:, 0:q_end + 2 * kv],
            wi[:, q_end + 2 * kv:q_end + 3 * kv], wi[:, q_end + 4 * kv:q_end + 5 * kv],
            wi[:, q_end + 3 * kv:q_end + 4 * kv], wi[:, q_end + 5 * kv:q_end + 6 * kv],
            wg.reshape(D_MODEL, GATE_ROWS)], axis=1).T.astype(BF16)
        w = dict(
            norm_mix=norm_mix[i][None, :], wt=wt,
            wn=jnp.concatenate([wi[:, g_end:g_end + 2 * RET_W], wi[:, g_end + 3 * RET_W:g_end + 4 * RET_W],
                                wi[:, g_end + 2 * RET_W:g_end + 3 * RET_W], wi[:, g_end + 4 * RET_W:]],
                               axis=1).astype(BF16),
            qg=nsa_q_norm[i][:, None], kgc=nsa_k_norm[i][:, None],
            kgr=jnp.pad(nsa_k_norm[i][None, :], ((0, 0), (0, KA - DH))),
            pos_k=cmp_pos_k[i].reshape(1, CMP_BLOCK * DH), pos_v=cmp_pos_v[i].reshape(1, CMP_BLOCK * DH),
            w1k=cmp_w1_k[i].astype(BF16), w2k=jnp.pad(cmp_w2_k[i], ((0, 0), (0, KA - DH))).astype(BF16),
            w1v=cmp_w1_v[i].astype(BF16), w2vT=cmp_w2_v[i].T.astype(BF16),
            wa=w_up_nsa[i].astype(BF16), wb=w_up_ret[i].astype(BF16), wo=w_out[i].astype(BF16),
            norm_mlp=norm_mlp[i][None, :], w1=w_ff1[i].astype(BF16), w2=w_ff2[i].astype(BF16),
            norm_ple=norm_ple[i][None, :], wg=w_ple_gate[i].astype(BF16), wp=w_ple[i].astype(BF16),
        )
        x2 = _layer(x2, p[i].reshape(b * s, PLE_DIM), w, tabs, b=b, s=s)
    return x2.reshape(b, s, D_MODEL)
```
